```python
import jax, jax.numpy as jnp
from jax import lax
import numpy as np


D_MODEL = 1024
BATCH = 16
SEQ = 2048
DEPTH = 1

HEAD_DIM = 64
NSA_HEADS = 8
NSA_KV = 2
NSA_GROUP = NSA_HEADS // NSA_KV
CMP_LEN = 32
CMP_STRIDE = 16
CMP_HIDDEN = 256
SEL_BLOCK = 64
SEL_TOPN = 8
SEL_Q_BLOCK = 64
NSA_WINDOW = 512
SWA_HEADS = 8
SWA_KV = 2
SWA_GROUP = SWA_HEADS // SWA_KV
SWA_WINDOW = 128
Q_BLOCK = 128
ROPE_THETA = 10000.0
LN_EPS = 1e-5
NEG_INF = -1e30
FORCE_SCORE = 1e9
NSA_WIDTH = NSA_HEADS * HEAD_DIM
SWA_WIDTH = SWA_HEADS * HEAD_DIM
NSA_KV_WIDTH = NSA_KV * HEAD_DIM
SWA_KV_WIDTH = SWA_KV * HEAD_DIM
DEEPNORM_ALPHA = (2 * DEPTH) ** 0.25
DEEPNORM_BETA = (8 * DEPTH) ** -0.25
IN_SPLITS = [NSA_WIDTH,
             2 * NSA_KV_WIDTH,
             2 * NSA_KV_WIDTH,
             2 * NSA_KV_WIDTH,
             3 * NSA_HEADS,
             NSA_WIDTH,
             SWA_WIDTH,
             2 * SWA_KV_WIDTH,
             SWA_WIDTH,
             2 * D_MODEL]
IN_WIDTH = sum(IN_SPLITS)

kernel_name = 'nsa_swa_sink_griffin_merge_deepnorm_adaln'


def _layer_norm(x):
    xf = x.astype(jnp.float32)
    mu = jnp.mean(xf, axis=-1, keepdims=True)
    var = jnp.mean(jnp.square(xf - mu), axis=-1, keepdims=True)
    return ((xf - mu) * lax.rsqrt(var + LN_EPS)).astype(x.dtype)


def _rope(t, positions):
    half = HEAD_DIM // 2
    inv_freq = ROPE_THETA ** (-jnp.arange(half, dtype=jnp.float32) / half)
    ang = positions.astype(jnp.float32)[..., None] * inv_freq
    shp = ang.shape[:2] + (1,) * (t.ndim - 3) + (half,)
    cos = jnp.cos(ang).reshape(shp)
    sin = jnp.sin(ang).reshape(shp)
    tf = t.astype(jnp.float32)
    t1, t2 = tf[..., :half], tf[..., half:]
    return jnp.concatenate([t1 * cos - t2 * sin, t2 * cos + t1 * sin], axis=-1).astype(t.dtype)


def _split_columns(h):
    offs = [int(o) for o in np.cumsum([0] + IN_SPLITS)]
    return [h[..., offs[i]:offs[i + 1]] for i in range(len(IN_SPLITS))]


def _kv(t, n_kv, positions):
    B, S = t.shape[:2]
    k, v = jnp.split(t, 2, axis=-1)
    k = _rope(k.reshape(B, S, n_kv, HEAD_DIM), positions)
    return k, v.reshape(B, S, n_kv, HEAD_DIM)


def _banded_attention(qg, k, v, window, sinks=None):
    B, S, G, hpg, dh = qg.shape
    n_prev = -(-(window - 1) // Q_BLOCK)
    pad = n_prev * Q_BLOCK
    span = pad + Q_BLOCK
    kp = jnp.pad(k, ((0, 0), (pad, 0), (0, 0), (0, 0)))
    vp = jnp.pad(v, ((0, 0), (pad, 0), (0, 0), (0, 0)))
    rel = np.arange(Q_BLOCK)[:, None] + pad - np.arange(span)[None, :]
    in_band = jnp.asarray((rel >= 0) & (rel < window))
    scale = dh ** -0.5

    def block(i):
        start = i * Q_BLOCK
        qb = lax.dynamic_slice_in_dim(qg, start, Q_BLOCK, axis=1)
        kb = lax.dynamic_slice_in_dim(kp, start, span, axis=1)
        vb = lax.dynamic_slice_in_dim(vp, start, span, axis=1)
        mask = in_band & ((start - pad + jnp.arange(span)) >= 0)[None, :]
        s = jnp.einsum('bqghd,bkgd->bghqk', qb, kb).astype(jnp.float32) * scale
        s = jnp.where(mask, s, NEG_INF)
        if sinks is None:
            p = jax.nn.softmax(s, axis=-1)
        else:
            sink = jnp.broadcast_to(sinks.astype(jnp.float32).reshape(1, G, hpg, 1, 1), s.shape[:-1] + (1,))
            p = jax.nn.softmax(jnp.concatenate([s, sink], axis=-1), axis=-1)[..., :-1]
        return jnp.einsum('bghqk,bkgd->bqghd', p.astype(vb.dtype), vb)

    out = lax.map(block, jnp.arange(S // Q_BLOCK))
    return out.transpose(1, 0, 2, 3, 4, 5).reshape(B, S, G, hpg, dh)


def _compress(kv, pos_emb, w1, w2):
    S = kv.shape[1]
    n_cmp = (S - CMP_LEN) // CMP_STRIDE + 1
    idx = np.arange(n_cmp)[:, None] * CMP_STRIDE + np.arange(CMP_LEN)[None, :]
    blocks = kv[:, idx] + pos_emb[:, None, :]
    hid = jax.nn.silu(jnp.einsum('bclgd,ldf->bcgf', blocks, w1.reshape(CMP_LEN, HEAD_DIM, CMP_HIDDEN)))
    return jnp.einsum('bcgf,fd->bcgd', hid, w2)


def _compressed_attention(qg, kc, vc):
    S, n_cmp = qg.shape[1], kc.shape[1]
    visible = jnp.asarray((np.arange(n_cmp) * CMP_STRIDE + CMP_LEN - 1)[None, :] <= np.arange(S)[:, None])
    s = jnp.einsum('bsghd,bcgd->bghsc', qg, kc).astype(jnp.float32) * (HEAD_DIM ** -0.5)
    s = jnp.where(visible, s, NEG_INF)
    p = jax.nn.softmax(s, axis=-1) * visible
    o = jnp.einsum('bghsc,bcgd->bsghd', p.astype(vc.dtype), vc)
    return o, p


def _select_blocks(p_cmp, S):
    n_cmp = p_cmp.shape[-1]
    n_blk = S // SEL_BLOCK
    c0 = np.arange(n_cmp) * CMP_STRIDE
    j0 = np.arange(n_blk) * SEL_BLOCK
    overlap = (c0[:, None] < j0[None, :] + SEL_BLOCK) & (c0[:, None] + CMP_LEN > j0[None, :])
    imp = jnp.einsum('bghsc,cj->bgsj', p_cmp, jnp.asarray(overlap, jnp.float32))
    cur = np.arange(S) // SEL_BLOCK
    j = np.arange(n_blk)
    causal = j[None, :] <= cur[:, None]
    forced = (j[None, :] == 0) | (j[None, :] == cur[:, None]) | (j[None, :] == cur[:, None] - 1)
    score = jnp.where(jnp.asarray(forced), FORCE_SCORE, imp)
    score = jnp.where(jnp.asarray(causal), score, NEG_INF)
    _, idx = lax.top_k(score, min(SEL_TOPN, n_blk))
    return idx


def _selected_attention(qg, k, v, idx):
    B, S, G, hpg, dh = qg.shape
    n_blk = S // SEL_BLOCK
    n_top = idx.shape[-1]
    kb = k.reshape(B, n_blk, SEL_BLOCK, G, dh).transpose(0, 3, 1, 2, 4)
    vb = v.reshape(B, n_blk, SEL_BLOCK, G, dh).transpose(0, 3, 1, 2, 4)
    bi = jnp.arange(B)[:, None, None, None]
    gi = jnp.arange(G)[None, :, None, None]
    scale = dh ** -0.5

    def block(i):
        start = i * SEL_Q_BLOCK
        qb = lax.dynamic_slice_in_dim(qg, start, SEL_Q_BLOCK, axis=1)
        ib = lax.dynamic_slice_in_dim(idx, start, SEL_Q_BLOCK, axis=2)
        kg = kb[bi, gi, ib]
        vg = vb[bi, gi, ib].reshape(B, G, SEL_Q_BLOCK, n_top * SEL_BLOCK, dh)
        key_pos = ib[..., None] * SEL_BLOCK + jnp.arange(SEL_BLOCK)
        q_pos = start + jnp.arange(SEL_Q_BLOCK)
        mask = (key_pos <= q_pos[:, None, None]).reshape(B, G, 1, SEL_Q_BLOCK, n_top * SEL_BLOCK)
        s = jnp.einsum('bqghd,bgqnkd->bghqnk', qb, kg).astype(jnp.float32) * scale
        s = jnp.where(mask, s.reshape(B, G, hpg, SEL_Q_BLOCK, n_top * SEL_BLOCK), NEG_INF)
        p = jax.nn.softmax(s, axis=-1).astype(vg.dtype)
        return jnp.einsum('bghqm,bgqmd->bqghd', p, vg)

    out = lax.map(block, jnp.arange(S // SEL_Q_BLOCK))
    return out.transpose(1, 0, 2, 3, 4, 5).reshape(B, S, G, hpg, dh)


def setup_inputs(seed: int = 0) -> dict:
    key = jax.random.key(seed)
    ks = jax.random.split(key, 20)
    nrm = lambda k, shape, s: jax.random.normal(k, shape, jnp.float32) * s
    L = DEPTH
    x = jax.random.normal(ks[0], (BATCH, SEQ, D_MODEL), jnp.float32)
    c = jax.random.normal(ks[1], (BATCH, D_MODEL), jnp.float32)
    positions = (jnp.arange(SEQ, dtype=jnp.int32)[None, :]
                 + jax.random.randint(ks[2], (BATCH, 1), 0, 1024, dtype=jnp.int32))
    return {
        'x': x,
        'c': c,
        'positions': positions,
        'w_ada': nrm(ks[3], (L, D_MODEL, 3 * D_MODEL), 0.5 * D_MODEL ** -0.5),
        'b_ada': nrm(ks[4], (L, 3 * D_MODEL), 0.01),
        'w_in': nrm(ks[5], (L, D_MODEL, IN_WIDTH), D_MODEL ** -0.5),
        'cmp_pos_k': nrm(ks[6], (L, CMP_LEN, HEAD_DIM), 0.02),
        'cmp_w1_k': nrm(ks[7], (L, CMP_LEN * HEAD_DIM, CMP_HIDDEN), (CMP_LEN * HEAD_DIM) ** -0.5),
        'cmp_w2_k': nrm(ks[8], (L, CMP_HIDDEN, HEAD_DIM), CMP_HIDDEN ** -0.5),
        'cmp_pos_v': nrm(ks[9], (L, CMP_LEN, HEAD_DIM), 0.02),
        'cmp_w1_v': nrm(ks[10], (L, CMP_LEN * HEAD_DIM, CMP_HIDDEN), (CMP_LEN * HEAD_DIM) ** -0.5),
        'cmp_w2_v': nrm(ks[11], (L, CMP_HIDDEN, HEAD_DIM), CMP_HIDDEN ** -0.5),
        'sinks': nrm(ks[12], (L, SWA_HEADS), 0.5),
        'w_up_a': nrm(ks[13], (L, NSA_WIDTH, D_MODEL), DEEPNORM_BETA * NSA_WIDTH ** -0.5),
        'w_up_b': nrm(ks[14], (L, SWA_WIDTH, D_MODEL), DEEPNORM_BETA * SWA_WIDTH ** -0.5),
        'w_out': nrm(ks[15], (L, D_MODEL, D_MODEL), DEEPNORM_BETA * D_MODEL ** -0.5),
        'ln_g': 1.0 + nrm(ks[16], (L, D_MODEL), 0.01),
        'ln_b': nrm(ks[17], (L, D_MODEL), 0.01),
    }


def reference(x, c, positions, w_ada, b_ada, w_in, cmp_pos_k, cmp_w1_k, cmp_w2_k,
              cmp_pos_v, cmp_w1_v, cmp_w2_v, sinks, w_up_a, w_up_b, w_out, ln_g, ln_b):
    B, S, _ = x.shape
    for l in range(DEPTH):
        mod = jax.nn.silu(c) @ w_ada[l] + b_ada[l]
        shift, scale, gate = jnp.split(mod, 3, axis=-1)
        u = _layer_norm(x) * (1.0 + scale[:, None, :]) + shift[:, None, :]
        (q_a, kv_cmp, kv_sel, kv_win, g_nsa, z_a,
         q_b, kv_b, z_b, g_merge) = _split_columns(u @ w_in[l])

        qa = _rope(q_a.reshape(B, S, NSA_KV, NSA_GROUP, HEAD_DIM), positions)
        k_c, v_c = _kv(kv_cmp, NSA_KV, positions)
        k_s, v_s = _kv(kv_sel, NSA_KV, positions)
        k_w, v_w = _kv(kv_win, NSA_KV, positions)
        kc = _compress(k_c, cmp_pos_k[l], cmp_w1_k[l], cmp_w2_k[l])
        vc = _compress(v_c, cmp_pos_v[l], cmp_w1_v[l], cmp_w2_v[l])
        o_cmp, p_cmp = _compressed_attention(qa, kc, vc)
        sel_idx = _select_blocks(p_cmp, S)
        o_sel = _selected_attention(qa, k_s, v_s, sel_idx)
        o_win = _banded_attention(qa, k_w, v_w, NSA_WINDOW)
        gts = jax.nn.sigmoid(g_nsa.reshape(B, S, NSA_KV, NSA_GROUP, 3))
        o_a = (gts[..., 0:1] * o_cmp + gts[..., 1:2] * o_sel + gts[..., 2:3] * o_win).reshape(B, S, NSA_WIDTH)
        y_a = o_a * jax.nn.silu(z_a)

        qb = _rope(q_b.reshape(B, S, SWA_KV, SWA_GROUP, HEAD_DIM), positions)
        k_b, v_b = _kv(kv_b, SWA_KV, positions)
        o_b = _banded_attention(qb, k_b, v_b, SWA_WINDOW, sinks[l]).reshape(B, S, SWA_WIDTH)
        y_b = o_b * jax.nn.silu(z_b)

        gm_a, gm_b = jnp.split(g_merge, 2, axis=-1)
        merged = jax.nn.sigmoid(gm_a) * (y_a @ w_up_a[l]) + jax.nn.sigmoid(gm_b) * (y_b @ w_up_b[l])
        out = merged @ w_out[l]
        x = _layer_norm(DEEPNORM_ALPHA * x + gate[:, None, :] * out) * ln_g[l] + ln_b[l]
    return x
```

```python
import functools

import numpy as np
import jax
import jax.numpy as jnp
from jax import lax
from jax.experimental import pallas as pl
from jax.experimental.pallas import tpu as pltpu

D_MODEL = 1024
HEAD_DIM = 64
HALF = HEAD_DIM // 2
N_HEADS = 8
N_KV = 2
GROUP = N_HEADS // N_KV
Q_WIDTH = N_HEADS * HEAD_DIM
KV_WIDTH = N_KV * HEAD_DIM
CMP_LEN = 32
CMP_STRIDE = 16
CMP_HIDDEN = 256
SEL_BLOCK = 64
SEL_TOPN = 8
NSA_WINDOW = 512
SWA_WINDOW = 128
ROPE_THETA = 10000.0
LN_EPS = 1e-5
NEG_INF = -1e30
FORCE_SCORE = 1e9
DEPTH = 1
DEEPNORM_ALPHA = (2 * DEPTH) ** 0.25
LANES = 128
VMEM_LIMIT = 56 * 1024 * 1024

_SPLITS = [Q_WIDTH, 2 * KV_WIDTH, 2 * KV_WIDTH, 2 * KV_WIDTH, 3 * N_HEADS, Q_WIDTH,
           Q_WIDTH, 2 * KV_WIDTH, Q_WIDTH, 2 * D_MODEL]
_OFFS = [int(o) for o in np.cumsum([0] + _SPLITS)]
(OFF_QA, OFF_KVC, OFF_KVS, OFF_KVW, OFF_GN, OFF_ZA, OFF_QB, OFF_KVB, OFF_ZB, OFF_GM) = _OFFS[:10]


def _q_pair_perm():
    cols = []
    for j in range(GROUP):
        for half in range(2):
            for grp in range(N_KV):
                h = j + GROUP * grp
                cols += [h * HEAD_DIM + half * HALF + i for i in range(HALF)]
    return np.asarray(cols, np.int32)


def _k_pair_perm():
    cols = []
    for half in range(2):
        for grp in range(N_KV):
            cols += [grp * HEAD_DIM + half * HALF + i for i in range(HALF)]
    return np.asarray(cols, np.int32)


def _o_pair_perm():
    cols = []
    for j in range(GROUP):
        for grp in range(N_KV):
            h = j + GROUP * grp
            cols += [h * HEAD_DIM + d for d in range(HEAD_DIM)]
    return np.asarray(cols, np.int32)


Q_PERM = _q_pair_perm()
K_PERM = _k_pair_perm()
O_PERM = _o_pair_perm()


def _cparams(sem):
    return pltpu.CompilerParams(dimension_semantics=sem, vmem_limit_bytes=VMEM_LIMIT)


def _dot(a, b):
    return jnp.dot(a, b, preferred_element_type=jnp.float32)


def _dot_nt(a, b):
    return lax.dot_general(a, b, (((1,), (1,)), ((), ())), preferred_element_type=jnp.float32)


def _layer_norm(x):
    mu = jnp.mean(x, axis=-1, keepdims=True)
    xc = x - mu
    var = jnp.mean(xc * xc, axis=-1, keepdims=True)
    return xc * lax.rsqrt(var + LN_EPS)


def _sigmoid(x):
    return 1.0 / (1.0 + jnp.exp(-x))


def _adaln_kernel(c_ref, w_ref, b_ref, o_ref):
    c = c_ref[...]
    a = c * _sigmoid(c)
    o_ref[...] = jnp.dot(a, w_ref[...], preferred_element_type=jnp.float32,
                         precision=lax.Precision.HIGHEST) + b_ref[...]


def _adaln_mod(c, w_ada, b_ada):
    bsz = c.shape[0]
    n = w_ada.shape[1]
    tn = 512
    return pl.pallas_call(
        _adaln_kernel,
        out_shape=jax.ShapeDtypeStruct((bsz, n), jnp.float32),
        grid=(n // tn,),
        in_specs=[pl.BlockSpec((bsz, D_MODEL), lambda j: (0, 0)),
                  pl.BlockSpec((D_MODEL, tn), lambda j: (0, j)),
                  pl.BlockSpec((1, tn), lambda j: (0, j))],
        out_specs=pl.BlockSpec((bsz, tn), lambda j: (0, j)),
        compiler_params=_cparams(("arbitrary",)),
        name="adaln_mod",
    )(c, w_ada, b_ada.reshape(1, n))


def _rope_table_kernel(pos_ref, invf_ref, cos_ref, sin_ref):
    ang = pos_ref[...] * invf_ref[...]
    cos_ref[...] = jnp.cos(ang)
    sin_ref[...] = jnp.sin(ang)


def _rope_tables(positions):
    bsz, seq = positions.shape
    per_row = LANES // HALF
    rows = bsz * seq // per_row
    pos = jnp.repeat(positions.astype(jnp.float32).reshape(rows, per_row), HALF, axis=1)
    inv_freq = ROPE_THETA ** (-jnp.arange(HALF, dtype=jnp.float32) / HALF)
    invf = jnp.tile(inv_freq, per_row).reshape(1, LANES)
    tr = seq // per_row
    cos, sin = pl.pallas_call(
        _rope_table_kernel,
        out_shape=[jax.ShapeDtypeStruct((rows, LANES), jnp.float32)] * 2,
        grid=(rows // tr,),
        in_specs=[pl.BlockSpec((tr, LANES), lambda i: (i, 0)),
                  pl.BlockSpec((1, LANES), lambda i: (0, 0))],
        out_specs=[pl.BlockSpec((tr, LANES), lambda i: (i, 0))] * 2,
        compiler_params=_cparams(("arbitrary",)),
        name="rope_table",
    )(pos, invf)
    cos = cos.reshape(bsz, seq, HALF)
    sin = sin.reshape(bsz, seq, HALF)
    cos_t = jnp.tile(cos, (1, 1, 4))
    sin_t = jnp.concatenate([-sin, -sin, sin, sin], axis=-1)
    return cos_t, sin_t


N_ROPE_SLABS = 12
N_V_SLABS = 4
QKV_WIDTH = (N_ROPE_SLABS + N_V_SLABS) * LANES


def _qkv_kernel(x_ref, mod_ref, w_ref, cos_ref, sin_ref,
                qa_ref, kc_ref, ks_ref, kw_ref, qb_ref, kb_ref,
                vc_ref, vs_ref, vw_ref, vb_ref):
    x = x_ref[0]
    shift = mod_ref[0, 0:1, :]
    scale = mod_ref[0, 1:2, :]
    u = (_layer_norm(x) * (1.0 + scale) + shift).astype(jnp.bfloat16)
    cos = cos_ref[0]
    sin = sin_ref[0]

    def rope(t):
        return t * cos + pltpu.roll(t, LANES // 2, 1) * sin

    def proj(lo, n_slabs):
        return _dot(u, w_ref[:, lo * LANES:(lo + n_slabs) * LANES])

    acc = proj(0, 4)
    for j in range(4):
        qa_ref[0, :, j * LANES:(j + 1) * LANES] = rope(acc[:, j * LANES:(j + 1) * LANES]).astype(jnp.bfloat16)
    acc = proj(4, 3)
    for j, ref in enumerate((kc_ref, ks_ref, kw_ref)):
        ref[0] = rope(acc[:, j * LANES:(j + 1) * LANES]).astype(jnp.bfloat16)
    acc = proj(7, 4)
    for j in range(4):
        qb_ref[0, :, j * LANES:(j + 1) * LANES] = rope(acc[:, j * LANES:(j + 1) * LANES]).astype(jnp.bfloat16)
    acc = proj(11, 5)
    kb_ref[0] = rope(acc[:, 0:LANES]).astype(jnp.bfloat16)
    for j, ref in enumerate((vc_ref, vs_ref, vw_ref, vb_ref)):
        ref[0] = acc[:, (j + 1) * LANES:(j + 2) * LANES].astype(jnp.bfloat16)


def _qkv_proj(x, mod3, w_qkv, cos_t, sin_t, tm=512):
    bsz, seq, _ = x.shape
    q_shape = jax.ShapeDtypeStruct((bsz, seq, Q_WIDTH), jnp.bfloat16)
    k_shape = jax.ShapeDtypeStruct((bsz, seq, KV_WIDTH), jnp.bfloat16)
    q_spec = pl.BlockSpec((1, tm, Q_WIDTH), lambda b, i: (b, i, 0))
    k_spec = pl.BlockSpec((1, tm, KV_WIDTH), lambda b, i: (b, i, 0))
    return pl.pallas_call(
        _qkv_kernel,
        out_shape=[q_shape, k_shape, k_shape, k_shape, q_shape, k_shape,
                   k_shape, k_shape, k_shape, k_shape],
        grid=(bsz, seq // tm),
        in_specs=[pl.BlockSpec((1, tm, D_MODEL), lambda b, i: (b, i, 0)),
                  pl.BlockSpec((1, 3, D_MODEL), lambda b, i: (b, 0, 0)),
                  pl.BlockSpec((D_MODEL, QKV_WIDTH), lambda b, i: (0, 0)),
                  k_spec, k_spec],
        out_specs=[q_spec, k_spec, k_spec, k_spec, q_spec, k_spec,
                   k_spec, k_spec, k_spec, k_spec],
        compiler_params=_cparams(("arbitrary", "arbitrary")),
        name="qkv_proj",
    )(x, mod3, w_qkv, cos_t, sin_t)


def _compress_kernel(xk_ref, xv_ref, w1k_ref, w1v_ref, posk_ref, posv_ref, w2k_ref, w2v_ref,
                     kc_ref, vc_ref):
    def one(x_ref, w1_ref, pos_ref, w2_ref, o_ref):
        n_chunks = x_ref.shape[1]
        hw = w1_ref.shape[1] // 2
        p = _dot(x_ref[0], w1_ref[...])
        pt = _dot(pos_ref[...], w1_ref[...])
        pos_term = pt[0:1, 0:hw] + pt[1:2, hw:]
        first = p[:, 0:hw]
        second = pltpu.roll(p[:, hw:], n_chunks - 1, 0)
        hid = first + second + pos_term
        hid = hid * _sigmoid(hid)
        o_ref[0] = _dot(hid.astype(jnp.bfloat16), w2_ref[...]).astype(jnp.bfloat16)

    one(xk_ref, w1k_ref, posk_ref, w2k_ref, kc_ref)
    one(xv_ref, w1v_ref, posv_ref, w2v_ref, vc_ref)


def _compress(xk, xv, w1k, w1v, posk, posv, w2k, w2v):
    bsz, n_chunks, cw = xk.shape
    hw2 = w1k.shape[1]
    out = jax.ShapeDtypeStruct((bsz, n_chunks, LANES), jnp.bfloat16)
    x_spec = pl.BlockSpec((1, n_chunks, cw), lambda b: (b, 0, 0))
    w1_spec = pl.BlockSpec((cw, hw2), lambda b: (0, 0))
    pos_spec = pl.BlockSpec((8, cw), lambda b: (0, 0))
    w2_spec = pl.BlockSpec((hw2 // 2, LANES), lambda b: (0, 0))
    o_spec = pl.BlockSpec((1, n_chunks, LANES), lambda b: (b, 0, 0))
    return pl.pallas_call(
        _compress_kernel,
        out_shape=[out, out],
        grid=(bsz,),
        in_specs=[x_spec, x_spec, w1_spec, w1_spec, pos_spec, pos_spec, w2_spec, w2_spec],
        out_specs=[o_spec, o_spec],
        compiler_params=_cparams(("arbitrary",)),
        name="compress",
    )(xk, xv, w1k, w1v, posk, posv, w2k, w2v)


def _stack_heads(q):
    lane = lax.broadcasted_iota(jnp.int32, (1, LANES), 1)
    is_a = (lane & (HEAD_DIM - 1)) < HALF
    zero = jnp.zeros((), q.dtype)
    tiles = [q[:, j * LANES:(j + 1) * LANES] for j in range(GROUP)]
    parts = [jnp.where(is_a, t, zero) for t in tiles] + [jnp.where(is_a, zero, t) for t in tiles]
    return jnp.concatenate(parts, axis=0)


def _unstack_heads(o, tq):
    lane = lax.broadcasted_iota(jnp.int32, (1, LANES), 1)
    lo = lane < HEAD_DIM
    return [jnp.where(lo, o[j * tq:(j + 1) * tq], o[(j + GROUP) * tq:(j + GROUP + 1) * tq])
            for j in range(GROUP)]


def _cmp_kernel(q_ref, kc_ref, vc_ref, ovt_ref, o_ref, bias_ref, *, tq, n_cmp, n_blk):
    qi = pl.program_id(1)
    qs = _stack_heads(q_ref[0])
    s = _dot_nt(qs, kc_ref[0])
    t_row = qi * tq + lax.broadcasted_iota(jnp.int32, (tq, LANES), 0)
    c_col = lax.broadcasted_iota(jnp.int32, (tq, LANES), 1)
    vis = (c_col * CMP_STRIDE + (CMP_LEN - 1) <= t_row) & (c_col < n_cmp)
    s3 = jnp.where(vis[None], s.reshape(N_HEADS, tq, LANES), NEG_INF)
    m = jnp.max(s3, axis=-1, keepdims=True)
    e = jnp.exp(s3 - m)
    p3 = e / jnp.sum(e, axis=-1, keepdims=True) * vis[None].astype(jnp.float32)
    o = _dot(p3.reshape(N_HEADS * tq, LANES).astype(jnp.bfloat16), vc_ref[0])
    for j, tile in enumerate(_unstack_heads(o, tq)):
        o_ref[0, :, j * LANES:(j + 1) * LANES] = tile.astype(jnp.bfloat16)

    j_blk = lax.broadcasted_iota(jnp.int32, (n_blk, tq), 0)
    t_q = qi * tq + lax.broadcasted_iota(jnp.int32, (n_blk, tq), 1)
    cur = jnp.right_shift(t_q, SEL_BLOCK.bit_length() - 1)
    forced = (j_blk == 0) | (j_blk == cur) | (j_blk == cur - 1)
    causal = j_blk <= cur
    rows = []
    for g in range(N_KV):
        pg = p3[g * GROUP]
        for h in range(1, GROUP):
            pg = pg + p3[g * GROUP + h]
        hi = pg.astype(jnp.bfloat16)
        lo = (pg - hi.astype(jnp.float32)).astype(jnp.bfloat16)
        imp_t = _dot_nt(ovt_ref[...], hi) + _dot_nt(ovt_ref[...], lo)
        score = jnp.where(forced, FORCE_SCORE, imp_t[0:n_blk, :])
        score = jnp.where(causal, score, NEG_INF)
        rank = jnp.zeros((n_blk, tq), jnp.int32)
        for i in range(n_blk):
            si = score[i:i + 1, :]
            beats = (si > score) | ((si == score) & (j_blk > i))
            rank = rank + beats.astype(jnp.int32)
        keep = (rank < SEL_TOPN) & causal
        rows.append(jnp.where(keep, 0.0, NEG_INF))
    rows.append(jnp.zeros((LANES - N_KV * n_blk, tq), jnp.float32))
    bias_t = jnp.concatenate(rows, axis=0)
    bias_ref[0] = bias_t.T.astype(jnp.bfloat16)


def _cmp_attn(qa, kc, vc, ovt, n_cmp, tq=128):
    bsz, seq, _ = qa.shape
    n_blk = seq // SEL_BLOCK
    kern = functools.partial(_cmp_kernel, tq=tq, n_cmp=n_cmp, n_blk=n_blk)
    kv_spec = pl.BlockSpec((1, LANES, LANES), lambda b, i: (b, 0, 0))
    return pl.pallas_call(
        kern,
        out_shape=[jax.ShapeDtypeStruct((bsz, seq, Q_WIDTH), jnp.bfloat16),
                   jax.ShapeDtypeStruct((bsz, seq, LANES), jnp.bfloat16)],
        grid=(bsz, seq // tq),
        in_specs=[pl.BlockSpec((1, tq, Q_WIDTH), lambda b, i: (b, i, 0)),
                  kv_spec, kv_spec,
                  pl.BlockSpec((LANES, LANES), lambda b, i: (0, 0))],
        out_specs=[pl.BlockSpec((1, tq, Q_WIDTH), lambda b, i: (b, i, 0)),
                   pl.BlockSpec((1, tq, LANES), lambda b, i: (b, i, 0))],
        compiler_params=_cparams(("arbitrary", "arbitrary")),
        name="cmp_attn",
    )(qa, kc, vc, ovt)


def _sel_kernel(q_ref, bias_ref, k_ref, v_ref, blk_ref, o_ref, kaug_ref, vaug_ref, *, tq, tk):
    qi = pl.program_id(1)

    @pl.when(qi == 0)
    def _():
        kaug_ref[:, 0:LANES] = k_ref[0]
        kaug_ref[:, LANES:2 * LANES] = blk_ref[...]
        vaug_ref[:, 0:LANES] = v_ref[0]
        vaug_ref[:, LANES:2 * LANES] = jnp.ones((v_ref.shape[1], LANES), jnp.bfloat16)

    qs = _stack_heads(q_ref[0])
    lane = lax.broadcasted_iota(jnp.int32, (1, LANES), 1)
    n_blk = blk_ref.shape[0] // SEL_BLOCK
    bt = bias_ref[0]
    zero = jnp.zeros((), bt.dtype)
    b0 = jnp.where(lane < n_blk, bt, zero)
    b1 = jnp.where((lane >= n_blk) & (lane < 2 * n_blk), bt, zero)
    qb = jnp.concatenate([b0] * GROUP + [b1] * GROUP, axis=0)
    qaug = jnp.concatenate([qs, qb], axis=1)

    row = lax.broadcasted_iota(jnp.int32, (tq, tk), 0)
    col = lax.broadcasted_iota(jnp.int32, (tq, tk), 1)
    diff = col - row

    def body(c, carry):
        m, l, acc = carry
        k0 = pl.multiple_of(c * tk, tk)
        s = _dot_nt(qaug, kaug_ref[pl.ds(k0, tk), :])
        cbias = jnp.where(diff <= qi * tq - k0, 0.0, NEG_INF)
        s3 = s.reshape(N_HEADS, tq, tk) + cbias[None]
        m_new = jnp.maximum(m, jnp.max(s3, axis=-1, keepdims=True))
        alpha = jnp.exp(m - m_new)
        p = jnp.exp(s3 - m_new).astype(jnp.bfloat16).reshape(N_HEADS * tq, tk)
        pv = _dot(p, vaug_ref[pl.ds(k0, tk), :])
        alpha2 = alpha.reshape(N_HEADS * tq, 1)
        acc = alpha2 * acc + pv[:, 0:LANES]
        l = alpha2 * l + pv[:, LANES:2 * LANES]
        return m_new, l, acc

    n_chunks = (qi * tq + tq + tk - 1) // tk
    m0 = jnp.full((N_HEADS, tq, 1), NEG_INF, jnp.float32)
    l0 = jnp.zeros((N_HEADS * tq, LANES), jnp.float32)
    a0 = jnp.zeros((N_HEADS * tq, LANES), jnp.float32)
    _, l, acc = lax.fori_loop(0, n_chunks, body, (m0, l0, a0))
    o = acc / l
    for j, tile in enumerate(_unstack_heads(o, tq)):
        o_ref[0, :, j * LANES:(j + 1) * LANES] = tile.astype(jnp.bfloat16)


def _sel_attn(qa, bias, ks, vs, blk_onehot, tq=128, tk=256):
    bsz, seq, _ = qa.shape
    kern = functools.partial(_sel_kernel, tq=tq, tk=tk)
    kv_spec = pl.BlockSpec((1, seq, LANES), lambda b, i: (b, 0, 0))
    return pl.pallas_call(
        kern,
        out_shape=jax.ShapeDtypeStruct((bsz, seq, Q_WIDTH), jnp.bfloat16),
        grid=(bsz, seq // tq),
        in_specs=[pl.BlockSpec((1, tq, Q_WIDTH), lambda b, i: (b, i, 0)),
                  pl.BlockSpec((1, tq, LANES), lambda b, i: (b, i, 0)),
                  kv_spec, kv_spec,
                  pl.BlockSpec((seq, LANES), lambda b, i: (0, 0))],
        out_specs=pl.BlockSpec((1, tq, Q_WIDTH), lambda b, i: (b, i, 0)),
        scratch_shapes=[pltpu.VMEM((seq, 2 * LANES), jnp.bfloat16),
                        pltpu.VMEM((seq, 2 * LANES), jnp.bfloat16)],
        compiler_params=_cparams(("arbitrary", "arbitrary")),
        name="sel_attn",
    )(qa, bias, ks, vs, blk_onehot)


def _band_kernel(*refs, tq, span, window, use_sink):
    if use_sink:
        q_ref, k_ref, v_ref, sink_ref, o_ref = refs
    else:
        q_ref, k_ref, v_ref, o_ref = refs
    qi = pl.program_id(1)
    seq = k_ref.shape[1]
    start = jnp.clip(qi * tq + tq - span, 0, seq - span)
    start = pl.multiple_of(start, tq)
    qs = _stack_heads(q_ref[0])
    s = _dot_nt(qs, k_ref[0, pl.ds(start, span), :])
    row = lax.broadcasted_iota(jnp.int32, (tq, span), 0)
    col = lax.broadcasted_iota(jnp.int32, (tq, span), 1)
    rel = (row - col) + (qi * tq - start)
    bias = jnp.where((rel >= 0) & (rel < window), 0.0, NEG_INF)
    s3 = s.reshape(N_HEADS, tq, span) + bias[None]
    m = jnp.max(s3, axis=-1, keepdims=True)
    if use_sink:
        sink = sink_ref[...].reshape(N_HEADS, tq, 1)
        m = jnp.maximum(m, sink)
    p = jnp.exp(s3 - m)
    l = jnp.sum(p, axis=-1, keepdims=True)
    if use_sink:
        l = l + jnp.exp(sink - m)
    o = _dot(p.astype(jnp.bfloat16).reshape(N_HEADS * tq, span), v_ref[0, pl.ds(start, span), :])
    o = o / l.reshape(N_HEADS * tq, 1)
    for j, tile in enumerate(_unstack_heads(o, tq)):
        o_ref[0, :, j * LANES:(j + 1) * LANES] = tile.astype(jnp.bfloat16)


def _band_attn(q, k, v, window, sink_rows=None, tq=128):
    bsz, seq, _ = q.shape
    span = (-(-(window - 1) // tq)) * tq + tq
    use_sink = sink_rows is not None
    kern = functools.partial(_band_kernel, tq=tq, span=span, window=window, use_sink=use_sink)
    kv_spec = pl.BlockSpec((1, seq, LANES), lambda b, i: (b, 0, 0))
    in_specs = [pl.BlockSpec((1, tq, Q_WIDTH), lambda b, i: (b, i, 0)), kv_spec, kv_spec]
    args = [q, k, v]
    if use_sink:
        in_specs.append(pl.BlockSpec((N_HEADS * tq, 1), lambda b, i: (0, 0)))
        args.append(sink_rows)
    return pl.pallas_call(
        kern,
        out_shape=jax.ShapeDtypeStruct((bsz, seq, Q_WIDTH), jnp.bfloat16),
        grid=(bsz, seq // tq),
        in_specs=in_specs,
        out_specs=pl.BlockSpec((1, tq, Q_WIDTH), lambda b, i: (b, i, 0)),
        compiler_params=_cparams(("arbitrary", "arbitrary")),
        name="band_attn_sink" if use_sink else "band_attn",
    )(*args)


GATE_WIDTH = LANES + 2 * Q_WIDTH + 2 * D_MODEL


def _merge_kernel(x_ref, mod_ref, wg_ref, gexp_ref, ocmp_ref, osel_ref, owin_ref, ob_ref,
                  wua_ref, wub_ref, wo_ref, lng_ref, lnb_ref, o_ref):
    x = x_ref[0]
    shift = mod_ref[0, 0:1, :]
    scale = mod_ref[0, 1:2, :]
    gate = mod_ref[0, 2:3, :]
    u = (_layer_norm(x) * (1.0 + scale) + shift).astype(jnp.bfloat16)

    sig = _sigmoid(_dot(u, wg_ref[:, 0:LANES]))
    hi = sig.astype(jnp.bfloat16)
    lo = (sig - hi.astype(jnp.float32)).astype(jnp.bfloat16)
    gx = _dot(hi, gexp_ref[...]) + _dot(lo, gexp_ref[...])
    o_a = (gx[:, 0:Q_WIDTH] * ocmp_ref[0].astype(jnp.float32)
           + gx[:, Q_WIDTH:2 * Q_WIDTH] * osel_ref[0].astype(jnp.float32)
           + gx[:, 2 * Q_WIDTH:3 * Q_WIDTH] * owin_ref[0].astype(jnp.float32))
    c0 = LANES
    z_a = _dot(u, wg_ref[:, c0:c0 + Q_WIDTH])
    y_a = (o_a * (z_a * _sigmoid(z_a))).astype(jnp.bfloat16)
    c0 += Q_WIDTH
    z_b = _dot(u, wg_ref[:, c0:c0 + Q_WIDTH])
    y_b = (ob_ref[0].astype(jnp.float32) * (z_b * _sigmoid(z_b))).astype(jnp.bfloat16)
    c0 += Q_WIDTH
    gm_a = _sigmoid(_dot(u, wg_ref[:, c0:c0 + D_MODEL]))
    merged = gm_a * _dot(y_a, wua_ref[...])
    c0 += D_MODEL
    gm_b = _sigmoid(_dot(u, wg_ref[:, c0:c0 + D_MODEL]))
    merged = merged + gm_b * _dot(y_b, wub_ref[...])
    out = _dot(merged.astype(jnp.bfloat16), wo_ref[...])
    y = DEEPNORM_ALPHA * x + gate * out
    o_ref[0] = _layer_norm(y) * lng_ref[...] + lnb_ref[...]


def _merge_out(x, mod3, wg, gexp, o_cmp, o_sel, o_win, o_b, wua, wub, wo, ln_g, ln_b, tm=256):
    bsz, seq, _ = x.shape
    tok = lambda w: pl.BlockSpec((1, tm, w), lambda b, i: (b, i, 0))
    full = lambda a: pl.BlockSpec(a.shape, lambda b, i: (0,) * a.ndim)
    return pl.pallas_call(
        _merge_kernel,
        out_shape=jax.ShapeDtypeStruct(x.shape, jnp.float32),
        grid=(bsz, seq // tm),
        in_specs=[tok(D_MODEL), pl.BlockSpec((1, 3, D_MODEL), lambda b, i: (b, 0, 0)),
                  full(wg), full(gexp), tok(Q_WIDTH), tok(Q_WIDTH), tok(Q_WIDTH), tok(Q_WIDTH),
                  full(wua), full(wub), full(wo), full(ln_g), full(ln_b)],
        out_specs=tok(D_MODEL),
        compiler_params=_cparams(("arbitrary", "arbitrary")),
        name="merge_out",
    )(x, mod3, wg, gexp, o_cmp, o_sel, o_win, o_b, wua, wub, wo, ln_g, ln_b)


def _prep_qkv_weight(w_in):
    q_scale = HEAD_DIM ** -0.5
    cols = [w_in[:, OFF_QA + Q_PERM] * q_scale,
            w_in[:, OFF_KVC + K_PERM], w_in[:, OFF_KVS + K_PERM], w_in[:, OFF_KVW + K_PERM],
            w_in[:, OFF_QB + Q_PERM] * q_scale,
            w_in[:, OFF_KVB + K_PERM],
            w_in[:, OFF_KVC + KV_WIDTH:OFF_KVC + 2 * KV_WIDTH],
            w_in[:, OFF_KVS + KV_WIDTH:OFF_KVS + 2 * KV_WIDTH],
            w_in[:, OFF_KVW + KV_WIDTH:OFF_KVW + 2 * KV_WIDTH],
            w_in[:, OFF_KVB + KV_WIDTH:OFF_KVB + 2 * KV_WIDTH]]
    return jnp.concatenate(cols, axis=1).astype(jnp.bfloat16)


def _prep_gate_weight(w_in):
    pad = jnp.zeros((D_MODEL, LANES - 3 * N_HEADS), w_in.dtype)
    cols = [w_in[:, OFF_GN:OFF_GN + 3 * N_HEADS], pad,
            w_in[:, OFF_ZA + O_PERM], w_in[:, OFF_ZB + O_PERM],
            w_in[:, OFF_GM:OFF_GM + 2 * D_MODEL]]
    return jnp.concatenate(cols, axis=1).astype(jnp.bfloat16)


def _gate_expand_matrix():
    e = np.zeros((LANES, 3 * Q_WIDTH), np.float32)
    for br in range(3):
        for c in range(Q_WIDTH):
            h = int(O_PERM[c]) // HEAD_DIM
            e[h * 3 + br, br * Q_WIDTH + c] = 1.0
    return jnp.asarray(e, jnp.bfloat16)


def _prep_compress(pos, w1, w2, paired):
    half_len = CMP_LEN // 2
    lane_src = K_PERM if paired else np.arange(KV_WIDTH, dtype=np.int32)
    grp = lane_src // HEAD_DIM
    dim = lane_src % HEAD_DIM
    w1r = w1.reshape(CMP_LEN, HEAD_DIM, CMP_HIDDEN)
    w1l = w1r[:, dim, :]
    own = jnp.asarray(grp[:, None] == np.arange(N_KV)[None, :], w1.dtype)
    w1x = (w1l[:, :, None, :] * own[None, :, :, None]).reshape(CMP_LEN, LANES, N_KV * CMP_HIDDEN)
    first = w1x[:half_len].reshape(half_len * LANES, N_KV * CMP_HIDDEN)
    second = w1x[half_len:].reshape(half_len * LANES, N_KV * CMP_HIDDEN)
    w1e = jnp.concatenate([first, second], axis=1).astype(jnp.bfloat16)
    pos_l = pos[:, dim]
    pos_rows = jnp.concatenate([pos_l[:half_len].reshape(1, -1), pos_l[half_len:].reshape(1, -1),
                                jnp.zeros((6, half_len * LANES), pos.dtype)], axis=0).astype(jnp.bfloat16)
    w2e = jnp.zeros((N_KV * CMP_HIDDEN, LANES), w2.dtype)
    for g in range(N_KV):
        lanes_g = np.nonzero(grp == g)[0]
        w2e = w2e.at[g * CMP_HIDDEN:(g + 1) * CMP_HIDDEN, lanes_g].set(w2[:, dim[lanes_g]])
    return w1e, pos_rows, w2e.astype(jnp.bfloat16)


def _overlap_t(n_cmp, n_blk):
    c0 = np.arange(n_cmp) * CMP_STRIDE
    j0 = np.arange(n_blk) * SEL_BLOCK
    overlap = (c0[:, None] < j0[None, :] + SEL_BLOCK) & (c0[:, None] + CMP_LEN > j0[None, :])
    ovt = np.zeros((LANES, LANES), np.float32)
    ovt[:n_blk, :n_cmp] = overlap.T
    return jnp.asarray(ovt, jnp.bfloat16)


def _block_onehot(seq):
    n_blk = seq // SEL_BLOCK
    e = np.zeros((seq, LANES), np.float32)
    pos = np.arange(seq)
    e[pos, pos // SEL_BLOCK] = 1.0
    e[pos, n_blk + pos // SEL_BLOCK] = 1.0
    return jnp.asarray(e, jnp.bfloat16)


def kernel(x, c, positions, w_ada, b_ada, w_in, cmp_pos_k, cmp_w1_k, cmp_w2_k,
           cmp_pos_v, cmp_w1_v, cmp_w2_v, sinks, w_up_a, w_up_b, w_out, ln_g, ln_b):
    bsz, seq, _ = x.shape
    n_cmp = (seq - CMP_LEN) // CMP_STRIDE + 1
    n_blk = seq // SEL_BLOCK
    assert seq // CMP_STRIDE == LANES and 2 * n_blk <= LANES
    cos_t, sin_t = _rope_tables(positions)
    for l in range(DEPTH):
        mod3 = _adaln_mod(c, w_ada[l], b_ada[l]).reshape(bsz, 3, D_MODEL)
        qa, k_c, k_s, k_w, qb, k_b, v_c, v_s, v_w, v_b = _qkv_proj(
            x, mod3, _prep_qkv_weight(w_in[l]), cos_t, sin_t)

        w1k, posk, w2k = _prep_compress(cmp_pos_k[l], cmp_w1_k[l], cmp_w2_k[l], paired=True)
        w1v, posv, w2v = _prep_compress(cmp_pos_v[l], cmp_w1_v[l], cmp_w2_v[l], paired=False)
        chunk = lambda t: t.reshape(bsz, seq // CMP_STRIDE, CMP_STRIDE * KV_WIDTH)
        kc, vc = _compress(chunk(k_c), chunk(v_c), w1k, w1v, posk, posv, w2k, w2v)

        o_cmp, sel_bias = _cmp_attn(qa, kc, vc, _overlap_t(n_cmp, n_blk), n_cmp)
        o_sel = _sel_attn(qa, sel_bias, k_s, v_s, _block_onehot(seq))
        o_win = _band_attn(qa, k_w, v_w, NSA_WINDOW)
        tq = 128
        sink_rows = jnp.repeat(sinks[l].astype(jnp.float32), tq).reshape(N_HEADS * tq, 1)
        o_b = _band_attn(qb, k_b, v_b, SWA_WINDOW, sink_rows, tq=tq)

        x = _merge_out(x, mod3, _prep_gate_weight(w_in[l]), _gate_expand_matrix(),
                       o_cmp, o_sel, o_win, o_b,
                       w_up_a[l][O_PERM].astype(jnp.bfloat16), w_up_b[l][O_PERM].astype(jnp.bfloat16),
                       w_out[l].astype(jnp.bfloat16), ln_g[l].reshape(1, D_MODEL), ln_b[l].reshape(1, D_MODEL))
    return x
```

```python
import functools

import numpy as np
import jax
import jax.numpy as jnp
from jax import lax
from jax.experimental import pallas as pl
from jax.experimental.pallas import tpu as pltpu

D_MODEL = 1024
HEAD_DIM = 64
HALF = HEAD_DIM // 2
N_HEADS = 8
N_KV = 2
GROUP = N_HEADS // N_KV
Q_WIDTH = N_HEADS * HEAD_DIM
KV_WIDTH = N_KV * HEAD_DIM
CMP_LEN = 32
CMP_STRIDE = 16
CMP_HIDDEN = 256
SEL_BLOCK = 64
SEL_TOPN = 8
NSA_WINDOW = 512
SWA_WINDOW = 128
ROPE_THETA = 10000.0
LN_EPS = 1e-5
NEG_INF = -1e30
FORCE_SCORE = 1e9
DEPTH = 1
DEEPNORM_ALPHA = (2 * DEPTH) ** 0.25
LANES = 128
VMEM_LIMIT = 56 * 1024 * 1024

_SPLITS = [Q_WIDTH, 2 * KV_WIDTH, 2 * KV_WIDTH, 2 * KV_WIDTH, 3 * N_HEADS, Q_WIDTH,
           Q_WIDTH, 2 * KV_WIDTH, Q_WIDTH, 2 * D_MODEL]
_OFFS = [int(o) for o in np.cumsum([0] + _SPLITS)]
(OFF_QA, OFF_KVC, OFF_KVS, OFF_KVW, OFF_GN, OFF_ZA, OFF_QB, OFF_KVB, OFF_ZB, OFF_GM) = _OFFS[:10]


def _q_pair_perm():
    cols = []
    for j in range(GROUP):
        for half in range(2):
            for grp in range(N_KV):
                h = j + GROUP * grp
                cols += [h * HEAD_DIM + half * HALF + i for i in range(HALF)]
    return np.asarray(cols, np.int32)


def _k_pair_perm():
    cols = []
    for half in range(2):
        for grp in range(N_KV):
            cols += [grp * HEAD_DIM + half * HALF + i for i in range(HALF)]
    return np.asarray(cols, np.int32)


def _o_pair_perm():
    cols = []
    for j in range(GROUP):
        for grp in range(N_KV):
            h = j + GROUP * grp
            cols += [h * HEAD_DIM + d for d in range(HEAD_DIM)]
    return np.asarray(cols, np.int32)


Q_PERM = _q_pair_perm()
K_PERM = _k_pair_perm()
O_PERM = _o_pair_perm()


def _cparams(sem):
    return pltpu.CompilerParams(dimension_semantics=sem, vmem_limit_bytes=VMEM_LIMIT)


def _dot(a, b):
    return jnp.dot(a, b, preferred_element_type=jnp.float32)


def _dot_nt(a, b):
    return lax.dot_general(a, b, (((1,), (1,)), ((), ())), preferred_element_type=jnp.float32)


def _layer_norm(x):
    mu = jnp.mean(x, axis=-1, keepdims=True)
    xc = x - mu
    var = jnp.mean(xc * xc, axis=-1, keepdims=True)
    return xc * lax.rsqrt(var + LN_EPS)


def _sigmoid(x):
    return 1.0 / (1.0 + jnp.exp(-x))


def _adaln_kernel(c_ref, w_ref, b_ref, o_ref):
    c = c_ref[...]
    a = c * _sigmoid(c)
    o_ref[...] = jnp.dot(a, w_ref[...], preferred_element_type=jnp.float32,
                         precision=lax.Precision.HIGHEST) + b_ref[...]


def _adaln_mod(c, w_ada, b_ada):
    bsz = c.shape[0]
    n = w_ada.shape[1]
    tn = 512
    return pl.pallas_call(
        _adaln_kernel,
        out_shape=jax.ShapeDtypeStruct((bsz, n), jnp.float32),
        grid=(n // tn,),
        in_specs=[pl.BlockSpec((bsz, D_MODEL), lambda j: (0, 0)),
                  pl.BlockSpec((D_MODEL, tn), lambda j: (0, j)),
                  pl.BlockSpec((1, tn), lambda j: (0, j))],
        out_specs=pl.BlockSpec((bsz, tn), lambda j: (0, j)),
        compiler_params=_cparams(("arbitrary",)),
        name="adaln_mod",
    )(c, w_ada, b_ada.reshape(1, n))


def _rope_table_kernel(pos_ref, invf_ref, cos_ref, sin_ref):
    ang = pos_ref[...] * invf_ref[...]
    cos_ref[...] = jnp.cos(ang)
    sin_ref[...] = jnp.sin(ang)


def _rope_tables(positions):
    bsz, seq = positions.shape
    per_row = LANES // HALF
    rows = bsz * seq // per_row
    pos = jnp.broadcast_to(positions.astype(jnp.float32).reshape(rows, per_row, 1),
                           (rows, per_row, HALF)).reshape(rows, LANES)
    inv_freq = ROPE_THETA ** (-jnp.arange(HALF, dtype=jnp.float32) / HALF)
    invf = jnp.broadcast_to(inv_freq[None, :], (per_row, HALF)).reshape(1, LANES)
    tr = seq // per_row
    cos, sin = pl.pallas_call(
        _rope_table_kernel,
        out_shape=[jax.ShapeDtypeStruct((rows, LANES), jnp.float32)] * 2,
        grid=(rows // tr,),
        in_specs=[pl.BlockSpec((tr, LANES), lambda i: (i, 0)),
                  pl.BlockSpec((1, LANES), lambda i: (0, 0))],
        out_specs=[pl.BlockSpec((tr, LANES), lambda i: (i, 0))] * 2,
        compiler_params=_cparams(("arbitrary",)),
        name="rope_table",
    )(pos, invf)
    return cos.reshape(bsz, seq, HALF), sin.reshape(bsz, seq, HALF)


N_ROPE_SLABS = 12
N_V_SLABS = 4
QKV_WIDTH = (N_ROPE_SLABS + N_V_SLABS) * LANES


def _qkv_kernel(x_ref, mod_ref, w_ref, cos_ref, sin_ref,
                qa_ref, kc_ref, ks_ref, kw_ref, qb_ref, kb_ref,
                vc_ref, vs_ref, vw_ref, vb_ref):
    x = x_ref[0]
    shift = mod_ref[0, 0:1, :]
    scale = mod_ref[0, 1:2, :]
    u = (_layer_norm(x) * (1.0 + scale) + shift).astype(jnp.bfloat16)
    c32 = cos_ref[0]
    s32 = sin_ref[0]
    cos = jnp.concatenate([c32, c32, c32, c32], axis=1)
    sin = jnp.concatenate([-s32, -s32, s32, s32], axis=1)

    def rope(t):
        return t * cos + pltpu.roll(t, LANES // 2, 1) * sin

    def proj(lo, n_slabs):
        return _dot(u, w_ref[:, lo * LANES:(lo + n_slabs) * LANES])

    acc = proj(0, 4)
    for j in range(4):
        qa_ref[0, :, j * LANES:(j + 1) * LANES] = rope(acc[:, j * LANES:(j + 1) * LANES]).astype(jnp.bfloat16)
    acc = proj(4, 3)
    for j, ref in enumerate((kc_ref, ks_ref, kw_ref)):
        ref[0] = rope(acc[:, j * LANES:(j + 1) * LANES]).astype(jnp.bfloat16)
    acc = proj(7, 4)
    for j in range(4):
        qb_ref[0, :, j * LANES:(j + 1) * LANES] = rope(acc[:, j * LANES:(j + 1) * LANES]).astype(jnp.bfloat16)
    acc = proj(11, 5)
    kb_ref[0] = rope(acc[:, 0:LANES]).astype(jnp.bfloat16)
    for j, ref in enumerate((vc_ref, vs_ref, vw_ref, vb_ref)):
        ref[0] = acc[:, (j + 1) * LANES:(j + 2) * LANES].astype(jnp.bfloat16)


def _qkv_proj(x, mod3, w_qkv, cos_t, sin_t, tm=512):
    bsz, seq, _ = x.shape
    q_shape = jax.ShapeDtypeStruct((bsz, seq, Q_WIDTH), jnp.bfloat16)
    k_shape = jax.ShapeDtypeStruct((bsz, seq, KV_WIDTH), jnp.bfloat16)
    q_spec = pl.BlockSpec((1, tm, Q_WIDTH), lambda b, i: (b, i, 0))
    k_spec = pl.BlockSpec((1, tm, KV_WIDTH), lambda b, i: (b, i, 0))
    t_spec = pl.BlockSpec((1, tm, HALF), lambda b, i: (b, i, 0))
    return pl.pallas_call(
        _qkv_kernel,
        out_shape=[q_shape, k_shape, k_shape, k_shape, q_shape, k_shape,
                   k_shape, k_shape, k_shape, k_shape],
        grid=(bsz, seq // tm),
        in_specs=[pl.BlockSpec((1, tm, D_MODEL), lambda b, i: (b, i, 0)),
                  pl.BlockSpec((1, 3, D_MODEL), lambda b, i: (b, 0, 0)),
                  pl.BlockSpec((D_MODEL, QKV_WIDTH), lambda b, i: (0, 0)),
                  t_spec, t_spec],
        out_specs=[q_spec, k_spec, k_spec, k_spec, q_spec, k_spec,
                   k_spec, k_spec, k_spec, k_spec],
        compiler_params=_cparams(("arbitrary", "arbitrary")),
        name="qkv_proj",
    )(x, mod3, w_qkv, cos_t, sin_t)


def _compress_kernel(xk_ref, xv_ref, w1k_ref, w1v_ref, posk_ref, posv_ref, w2k_ref, w2v_ref,
                     kc_ref, vc_ref):
    def one(x_ref, w1_ref, pos_ref, w2_ref, o_ref):
        n_chunks = x_ref.shape[1]
        hw = w1_ref.shape[1] // 2
        p = _dot(x_ref[0], w1_ref[...])
        pt = _dot(pos_ref[...], w1_ref[...])
        pos_term = pt[0:1, 0:hw] + pt[1:2, hw:]
        first = p[:, 0:hw]
        second = pltpu.roll(p[:, hw:], n_chunks - 1, 0)
        hid = first + second + pos_term
        hid = hid * _sigmoid(hid)
        o_ref[0] = _dot(hid.astype(jnp.bfloat16), w2_ref[...]).astype(jnp.bfloat16)

    one(xk_ref, w1k_ref, posk_ref, w2k_ref, kc_ref)
    one(xv_ref, w1v_ref, posv_ref, w2v_ref, vc_ref)


def _compress(xk, xv, w1k, w1v, posk, posv, w2k, w2v):
    bsz, n_chunks, cw = xk.shape
    hw2 = w1k.shape[1]
    out = jax.ShapeDtypeStruct((bsz, n_chunks, LANES), jnp.bfloat16)
    x_spec = pl.BlockSpec((1, n_chunks, cw), lambda b: (b, 0, 0))
    w1_spec = pl.BlockSpec((cw, hw2), lambda b: (0, 0))
    pos_spec = pl.BlockSpec((8, cw), lambda b: (0, 0))
    w2_spec = pl.BlockSpec((hw2 // 2, LANES), lambda b: (0, 0))
    o_spec = pl.BlockSpec((1, n_chunks, LANES), lambda b: (b, 0, 0))
    return pl.pallas_call(
        _compress_kernel,
        out_shape=[out, out],
        grid=(bsz,),
        in_specs=[x_spec, x_spec, w1_spec, w1_spec, pos_spec, pos_spec, w2_spec, w2_spec],
        out_specs=[o_spec, o_spec],
        compiler_params=_cparams(("arbitrary",)),
        name="compress",
    )(xk, xv, w1k, w1v, posk, posv, w2k, w2v)


def _stack_heads(q):
    lane = lax.broadcasted_iota(jnp.int32, (1, LANES), 1)
    is_a = (lane & (HEAD_DIM - 1)) < HALF
    zero = jnp.zeros((), q.dtype)
    tiles = [q[:, j * LANES:(j + 1) * LANES] for j in range(GROUP)]
    parts = [jnp.where(is_a, t, zero) for t in tiles] + [jnp.where(is_a, zero, t) for t in tiles]
    return jnp.concatenate(parts, axis=0)


def _unstack_heads(o, tq):
    lane = lax.broadcasted_iota(jnp.int32, (1, LANES), 1)
    lo = lane < HEAD_DIM
    return [jnp.where(lo, o[j * tq:(j + 1) * tq], o[(j + GROUP) * tq:(j + GROUP + 1) * tq])
            for j in range(GROUP)]


def _cmp_kernel(q_ref, kc_ref, vc_ref, ovt_ref, o_ref, bias_ref, *, tq, n_cmp, n_blk):
    qi = pl.program_id(1)
    qs = _stack_heads(q_ref[0])
    s = _dot_nt(qs, kc_ref[0])
    t_row = qi * tq + lax.broadcasted_iota(jnp.int32, (tq, LANES), 0)
    c_col = lax.broadcasted_iota(jnp.int32, (tq, LANES), 1)
    vis = (c_col * CMP_STRIDE + (CMP_LEN - 1) <= t_row) & (c_col < n_cmp)
    s3 = jnp.where(vis[None], s.reshape(N_HEADS, tq, LANES), NEG_INF)
    m = jnp.max(s3, axis=-1, keepdims=True)
    e = jnp.exp(s3 - m)
    p3 = e / jnp.sum(e, axis=-1, keepdims=True) * vis[None].astype(jnp.float32)
    o = _dot(p3.reshape(N_HEADS * tq, LANES).astype(jnp.bfloat16), vc_ref[0])
    for j, tile in enumerate(_unstack_heads(o, tq)):
        o_ref[0, :, j * LANES:(j + 1) * LANES] = tile.astype(jnp.bfloat16)

    j_blk = lax.broadcasted_iota(jnp.int32, (n_blk, tq), 0)
    t_q = qi * tq + lax.broadcasted_iota(jnp.int32, (n_blk, tq), 1)
    cur = jnp.right_shift(t_q, SEL_BLOCK.bit_length() - 1)
    forced = (j_blk == 0) | (j_blk == cur) | (j_blk == cur - 1)
    causal = j_blk <= cur
    rows = []
    for g in range(N_KV):
        pg = p3[g * GROUP]
        for h in range(1, GROUP):
            pg = pg + p3[g * GROUP + h]
        hi = pg.astype(jnp.bfloat16)
        lo = (pg - hi.astype(jnp.float32)).astype(jnp.bfloat16)
        imp_t = _dot_nt(ovt_ref[...], hi) + _dot_nt(ovt_ref[...], lo)
        score = jnp.where(forced, FORCE_SCORE, imp_t[0:n_blk, :])
        score = jnp.where(causal, score, NEG_INF)
        rank = jnp.zeros((n_blk, tq), jnp.int32)
        for i in range(n_blk):
            si = score[i:i + 1, :]
            beats = (si > score) | ((si == score) & (j_blk > i))
            rank = rank + beats.astype(jnp.int32)
        keep = (rank < SEL_TOPN) & causal
        rows.append(jnp.where(keep, 0.0, NEG_INF))
    rows.append(jnp.zeros((LANES - N_KV * n_blk, tq), jnp.float32))
    bias_t = jnp.concatenate(rows, axis=0)
    bias_ref[0] = bias_t.T.astype(jnp.bfloat16)


def _cmp_attn(qa, kc, vc, ovt, n_cmp, tq=128):
    bsz, seq, _ = qa.shape
    n_blk = seq // SEL_BLOCK
    kern = functools.partial(_cmp_kernel, tq=tq, n_cmp=n_cmp, n_blk=n_blk)
    kv_spec = pl.BlockSpec((1, LANES, LANES), lambda b, i: (b, 0, 0))
    return pl.pallas_call(
        kern,
        out_shape=[jax.ShapeDtypeStruct((bsz, seq, Q_WIDTH), jnp.bfloat16),
                   jax.ShapeDtypeStruct((bsz, seq, LANES), jnp.bfloat16)],
        grid=(bsz, seq // tq),
        in_specs=[pl.BlockSpec((1, tq, Q_WIDTH), lambda b, i: (b, i, 0)),
                  kv_spec, kv_spec,
                  pl.BlockSpec((LANES, LANES), lambda b, i: (0, 0))],
        out_specs=[pl.BlockSpec((1, tq, Q_WIDTH), lambda b, i: (b, i, 0)),
                   pl.BlockSpec((1, tq, LANES), lambda b, i: (b, i, 0))],
        compiler_params=_cparams(("arbitrary", "arbitrary")),
        name="cmp_attn",
    )(qa, kc, vc, ovt)


def _sel_kernel(q_ref, bias_ref, k_ref, v_ref, blk_ref, o_ref, kaug_ref, vaug_ref, *, tq, tk):
    qi = pl.program_id(1)

    @pl.when(qi == 0)
    def _():
        kaug_ref[:, 0:LANES] = k_ref[0]
        kaug_ref[:, LANES:2 * LANES] = blk_ref[...]
        vaug_ref[:, 0:LANES] = v_ref[0]
        vaug_ref[:, LANES:2 * LANES] = jnp.ones((v_ref.shape[1], LANES), jnp.bfloat16)

    qs = _stack_heads(q_ref[0])
    lane = lax.broadcasted_iota(jnp.int32, (1, LANES), 1)
    n_blk = blk_ref.shape[0] // SEL_BLOCK
    bt = bias_ref[0]
    zero = jnp.zeros((), bt.dtype)
    b0 = jnp.where(lane < n_blk, bt, zero)
    b1 = jnp.where((lane >= n_blk) & (lane < 2 * n_blk), bt, zero)
    qb = jnp.concatenate([b0] * GROUP + [b1] * GROUP, axis=0)
    qaug = jnp.concatenate([qs, qb], axis=1)

    row = lax.broadcasted_iota(jnp.int32, (tq, tk), 0)
    col = lax.broadcasted_iota(jnp.int32, (tq, tk), 1)
    diff = col - row

    def body(c, carry):
        m, l, acc = carry
        k0 = pl.multiple_of(c * tk, tk)
        s = _dot_nt(qaug, kaug_ref[pl.ds(k0, tk), :])
        cbias = jnp.where(diff <= qi * tq - k0, 0.0, NEG_INF)
        s3 = s.reshape(N_HEADS, tq, tk) + cbias[None]
        m_new = jnp.maximum(m, jnp.max(s3, axis=-1, keepdims=True))
        alpha = jnp.exp(m - m_new)
        p = jnp.exp(s3 - m_new).astype(jnp.bfloat16).reshape(N_HEADS * tq, tk)
        pv = _dot(p, vaug_ref[pl.ds(k0, tk), :])
        alpha2 = alpha.reshape(N_HEADS * tq, 1)
        acc = alpha2 * acc + pv[:, 0:LANES]
        l = alpha2 * l + pv[:, LANES:2 * LANES]
        return m_new, l, acc

    n_chunks = (qi * tq + tq + tk - 1) // tk
    m0 = jnp.full((N_HEADS, tq, 1), NEG_INF, jnp.float32)
    l0 = jnp.zeros((N_HEADS * tq, LANES), jnp.float32)
    a0 = jnp.zeros((N_HEADS * tq, LANES), jnp.float32)
    _, l, acc = lax.fori_loop(0, n_chunks, body, (m0, l0, a0))
    o = acc / l
    for j, tile in enumerate(_unstack_heads(o, tq)):
        o_ref[0, :, j * LANES:(j + 1) * LANES] = tile.astype(jnp.bfloat16)


def _sel_attn(qa, bias, ks, vs, blk_onehot, tq=128, tk=256):
    bsz, seq, _ = qa.shape
    kern = functools.partial(_sel_kernel, tq=tq, tk=tk)
    kv_spec = pl.BlockSpec((1, seq, LANES), lambda b, i: (b, 0, 0))
    return pl.pallas_call(
        kern,
        out_shape=jax.ShapeDtypeStruct((bsz, seq, Q_WIDTH), jnp.bfloat16),
        grid=(bsz, seq // tq),
        in_specs=[pl.BlockSpec((1, tq, Q_WIDTH), lambda b, i: (b, i, 0)),
                  pl.BlockSpec((1, tq, LANES), lambda b, i: (b, i, 0)),
                  kv_spec, kv_spec,
                  pl.BlockSpec((seq, LANES), lambda b, i: (0, 0))],
        out_specs=pl.BlockSpec((1, tq, Q_WIDTH), lambda b, i: (b, i, 0)),
        scratch_shapes=[pltpu.VMEM((seq, 2 * LANES), jnp.bfloat16),
                        pltpu.VMEM((seq, 2 * LANES), jnp.bfloat16)],
        compiler_params=_cparams(("arbitrary", "arbitrary")),
        name="sel_attn",
    )(qa, bias, ks, vs, blk_onehot)


def _band_kernel(*refs, tq, span, window, use_sink):
    if use_sink:
        q_ref, k_ref, v_ref, sink_ref, o_ref = refs
    else:
        q_ref, k_ref, v_ref, o_ref = refs
    qi = pl.program_id(1)
    seq = k_ref.shape[1]
    start = jnp.clip(qi * tq + tq - span, 0, seq - span)
    start = pl.multiple_of(start, tq)
    qs = _stack_heads(q_ref[0])
    s = _dot_nt(qs, k_ref[0, pl.ds(start, span), :])
    row = lax.broadcasted_iota(jnp.int32, (tq, span), 0)
    col = lax.broadcasted_iota(jnp.int32, (tq, span), 1)
    rel = (row - col) + (qi * tq - start)
    bias = jnp.where((rel >= 0) & (rel < window), 0.0, NEG_INF)
    s3 = s.reshape(N_HEADS, tq, span) + bias[None]
    m = jnp.max(s3, axis=-1, keepdims=True)
    if use_sink:
        sink = sink_ref[...].reshape(N_HEADS, tq, 1)
        m = jnp.maximum(m, sink)
    p = jnp.exp(s3 - m)
    l = jnp.sum(p, axis=-1, keepdims=True)
    if use_sink:
        l = l + jnp.exp(sink - m)
    o = _dot(p.astype(jnp.bfloat16).reshape(N_HEADS * tq, span), v_ref[0, pl.ds(start, span), :])
    o = o / l.reshape(N_HEADS * tq, 1)
    for j, tile in enumerate(_unstack_heads(o, tq)):
        o_ref[0, :, j * LANES:(j + 1) * LANES] = tile.astype(jnp.bfloat16)


def _band_attn(q, k, v, window, sink_rows=None, tq=128):
    bsz, seq, _ = q.shape
    span = (-(-(window - 1) // tq)) * tq + tq
    use_sink = sink_rows is not None
    kern = functools.partial(_band_kernel, tq=tq, span=span, window=window, use_sink=use_sink)
    kv_spec = pl.BlockSpec((1, seq, LANES), lambda b, i: (b, 0, 0))
    in_specs = [pl.BlockSpec((1, tq, Q_WIDTH), lambda b, i: (b, i, 0)), kv_spec, kv_spec]
    args = [q, k, v]
    if use_sink:
        in_specs.append(pl.BlockSpec((N_HEADS * tq, 1), lambda b, i: (0, 0)))
        args.append(sink_rows)
    return pl.pallas_call(
        kern,
        out_shape=jax.ShapeDtypeStruct((bsz, seq, Q_WIDTH), jnp.bfloat16),
        grid=(bsz, seq // tq),
        in_specs=in_specs,
        out_specs=pl.BlockSpec((1, tq, Q_WIDTH), lambda b, i: (b, i, 0)),
        compiler_params=_cparams(("arbitrary", "arbitrary")),
        name="band_attn_sink" if use_sink else "band_attn",
    )(*args)


GATE_WIDTH = LANES + 2 * Q_WIDTH + 2 * D_MODEL


def _merge_kernel(x_ref, mod_ref, wg_ref, gexp_ref, ocmp_ref, osel_ref, owin_ref, ob_ref,
                  wua_ref, wub_ref, wo_ref, lng_ref, lnb_ref, o_ref):
    x = x_ref[0]
    shift = mod_ref[0, 0:1, :]
    scale = mod_ref[0, 1:2, :]
    gate = mod_ref[0, 2:3, :]
    u = (_layer_norm(x) * (1.0 + scale) + shift).astype(jnp.bfloat16)

    sig = _sigmoid(_dot(u, wg_ref[:, 0:LANES]))
    hi = sig.astype(jnp.bfloat16)
    lo = (sig - hi.astype(jnp.float32)).astype(jnp.bfloat16)
    gx = _dot(hi, gexp_ref[...]) + _dot(lo, gexp_ref[...])
    o_a = (gx[:, 0:Q_WIDTH] * ocmp_ref[0].astype(jnp.float32)
           + gx[:, Q_WIDTH:2 * Q_WIDTH] * osel_ref[0].astype(jnp.float32)
           + gx[:, 2 * Q_WIDTH:3 * Q_WIDTH] * owin_ref[0].astype(jnp.float32))
    c0 = LANES
    z_a = _dot(u, wg_ref[:, c0:c0 + Q_WIDTH])
    y_a = (o_a * (z_a * _sigmoid(z_a))).astype(jnp.bfloat16)
    c0 += Q_WIDTH
    z_b = _dot(u, wg_ref[:, c0:c0 + Q_WIDTH])
    y_b = (ob_ref[0].astype(jnp.float32) * (z_b * _sigmoid(z_b))).astype(jnp.bfloat16)
    c0 += Q_WIDTH
    gm_a = _sigmoid(_dot(u, wg_ref[:, c0:c0 + D_MODEL]))
    merged = gm_a * _dot(y_a, wua_ref[...])
    c0 += D_MODEL
    gm_b = _sigmoid(_dot(u, wg_ref[:, c0:c0 + D_MODEL]))
    merged = merged + gm_b * _dot(y_b, wub_ref[...])
    out = _dot(merged.astype(jnp.bfloat16), wo_ref[...])
    y = DEEPNORM_ALPHA * x + gate * out
    o_ref[0] = _layer_norm(y) * lng_ref[...] + lnb_ref[...]


def _merge_out(x, mod3, wg, gexp, o_cmp, o_sel, o_win, o_b, wua, wub, wo, ln_g, ln_b, tm=256):
    bsz, seq, _ = x.shape
    tok = lambda w: pl.BlockSpec((1, tm, w), lambda b, i: (b, i, 0))
    full = lambda a: pl.BlockSpec(a.shape, lambda b, i: (0,) * a.ndim)
    return pl.pallas_call(
        _merge_kernel,
        out_shape=jax.ShapeDtypeStruct(x.shape, jnp.float32),
        grid=(bsz, seq // tm),
        in_specs=[tok(D_MODEL), pl.BlockSpec((1, 3, D_MODEL), lambda b, i: (b, 0, 0)),
                  full(wg), full(gexp), tok(Q_WIDTH), tok(Q_WIDTH), tok(Q_WIDTH), tok(Q_WIDTH),
                  full(wua), full(wub), full(wo), full(ln_g), full(ln_b)],
        out_specs=tok(D_MODEL),
        compiler_params=_cparams(("arbitrary", "arbitrary")),
        name="merge_out",
    )(x, mod3, wg, gexp, o_cmp, o_sel, o_win, o_b, wua, wub, wo, ln_g, ln_b)


def _pair_q_cols(w):
    k = w.shape[0]
    return w.reshape(k, N_KV, GROUP, 2, HALF).transpose(0, 2, 3, 1, 4).reshape(k, Q_WIDTH)


def _pair_k_cols(w):
    k = w.shape[0]
    return w.reshape(k, N_KV, 2, HALF).transpose(0, 2, 1, 3).reshape(k, KV_WIDTH)


def _pair_o_cols(w):
    k = w.shape[0]
    return w.reshape(k, N_KV, GROUP, HEAD_DIM).transpose(0, 2, 1, 3).reshape(k, Q_WIDTH)


def _pair_o_rows(w):
    n = w.shape[1]
    return w.reshape(N_KV, GROUP, HEAD_DIM, n).transpose(1, 0, 2, 3).reshape(Q_WIDTH, n)


def _prep_qkv_weight(w_in):
    q_scale = HEAD_DIM ** -0.5
    sec = lambda off, width: w_in[:, off:off + width]
    cols = [_pair_q_cols(sec(OFF_QA, Q_WIDTH)) * q_scale,
            _pair_k_cols(sec(OFF_KVC, KV_WIDTH)), _pair_k_cols(sec(OFF_KVS, KV_WIDTH)),
            _pair_k_cols(sec(OFF_KVW, KV_WIDTH)),
            _pair_q_cols(sec(OFF_QB, Q_WIDTH)) * q_scale,
            _pair_k_cols(sec(OFF_KVB, KV_WIDTH)),
            sec(OFF_KVC + KV_WIDTH, KV_WIDTH), sec(OFF_KVS + KV_WIDTH, KV_WIDTH),
            sec(OFF_KVW + KV_WIDTH, KV_WIDTH), sec(OFF_KVB + KV_WIDTH, KV_WIDTH)]
    return jnp.concatenate(cols, axis=1).astype(jnp.bfloat16)


def _prep_gate_weight(w_in):
    pad = jnp.zeros((D_MODEL, LANES - 3 * N_HEADS), w_in.dtype)
    cols = [w_in[:, OFF_GN:OFF_GN + 3 * N_HEADS], pad,
            _pair_o_cols(w_in[:, OFF_ZA:OFF_ZA + Q_WIDTH]), _pair_o_cols(w_in[:, OFF_ZB:OFF_ZB + Q_WIDTH]),
            w_in[:, OFF_GM:OFF_GM + 2 * D_MODEL]]
    return jnp.concatenate(cols, axis=1).astype(jnp.bfloat16)


def _gate_expand_matrix():
    e = np.zeros((LANES, 3 * Q_WIDTH), np.float32)
    for br in range(3):
        for c in range(Q_WIDTH):
            h = int(O_PERM[c]) // HEAD_DIM
            e[h * 3 + br, br * Q_WIDTH + c] = 1.0
    return jnp.asarray(e, jnp.bfloat16)


def _prep_compress(pos, w1, w2, paired):
    half_len = CMP_LEN // 2
    eye = jnp.eye(N_KV, dtype=w1.dtype)
    if paired:
        w1x = jnp.einsum("lhif,gk->lhgikf", w1.reshape(CMP_LEN, 2, HALF, CMP_HIDDEN), eye)
        w2e = jnp.einsum("fhi,kg->kfhgi", w2.reshape(CMP_HIDDEN, 2, HALF), eye)
        pos_l = jnp.broadcast_to(pos.reshape(CMP_LEN, 2, 1, HALF), (CMP_LEN, 2, N_KV, HALF))
    else:
        w1x = jnp.einsum("ldf,gk->lgdkf", w1.reshape(CMP_LEN, HEAD_DIM, CMP_HIDDEN), eye)
        w2e = jnp.einsum("fd,kg->kfgd", w2, eye)
        pos_l = jnp.broadcast_to(pos.reshape(CMP_LEN, 1, HEAD_DIM), (CMP_LEN, N_KV, HEAD_DIM))
    w1x = w1x.reshape(CMP_LEN, LANES, N_KV * CMP_HIDDEN)
    first = w1x[:half_len].reshape(half_len * LANES, N_KV * CMP_HIDDEN)
    second = w1x[half_len:].reshape(half_len * LANES, N_KV * CMP_HIDDEN)
    w1e = jnp.concatenate([first, second], axis=1).astype(jnp.bfloat16)
    pos_l = pos_l.reshape(CMP_LEN, LANES)
    pos_rows = jnp.concatenate([pos_l[:half_len].reshape(1, -1), pos_l[half_len:].reshape(1, -1),
                                jnp.zeros((6, half_len * LANES), pos.dtype)], axis=0).astype(jnp.bfloat16)
    return w1e, pos_rows, w2e.reshape(N_KV * CMP_HIDDEN, LANES).astype(jnp.bfloat16)


def _overlap_t(n_cmp, n_blk):
    c0 = np.arange(n_cmp) * CMP_STRIDE
    j0 = np.arange(n_blk) * SEL_BLOCK
    overlap = (c0[:, None] < j0[None, :] + SEL_BLOCK) & (c0[:, None] + CMP_LEN > j0[None, :])
    ovt = np.zeros((LANES, LANES), np.float32)
    ovt[:n_blk, :n_cmp] = overlap.T
    return jnp.asarray(ovt, jnp.bfloat16)


def _block_onehot(seq):
    n_blk = seq // SEL_BLOCK
    e = np.zeros((seq, LANES), np.float32)
    pos = np.arange(seq)
    e[pos, pos // SEL_BLOCK] = 1.0
    e[pos, n_blk + pos // SEL_BLOCK] = 1.0
    return jnp.asarray(e, jnp.bfloat16)


def kernel(x, c, positions, w_ada, b_ada, w_in, cmp_pos_k, cmp_w1_k, cmp_w2_k,
           cmp_pos_v, cmp_w1_v, cmp_w2_v, sinks, w_up_a, w_up_b, w_out, ln_g, ln_b):
    bsz, seq, _ = x.shape
    n_cmp = (seq - CMP_LEN) // CMP_STRIDE + 1
    n_blk = seq // SEL_BLOCK
    assert seq // CMP_STRIDE == LANES and 2 * n_blk <= LANES
    cos_t, sin_t = _rope_tables(positions)
    for l in range(DEPTH):
        mod3 = _adaln_mod(c, w_ada[l], b_ada[l]).reshape(bsz, 3, D_MODEL)
        qa, k_c, k_s, k_w, qb, k_b, v_c, v_s, v_w, v_b = _qkv_proj(
            x, mod3, _prep_qkv_weight(w_in[l]), cos_t, sin_t)

        w1k, posk, w2k = _prep_compress(cmp_pos_k[l], cmp_w1_k[l], cmp_w2_k[l], paired=True)
        w1v, posv, w2v = _prep_compress(cmp_pos_v[l], cmp_w1_v[l], cmp_w2_v[l], paired=False)
        chunk = lambda t: t.reshape(bsz, seq // CMP_STRIDE, CMP_STRIDE * KV_WIDTH)
        kc, vc = _compress(chunk(k_c), chunk(v_c), w1k, w1v, posk, posv, w2k, w2v)

        o_cmp, sel_bias = _cmp_attn(qa, kc, vc, _overlap_t(n_cmp, n_blk), n_cmp)
        o_sel = _sel_attn(qa, sel_bias, k_s, v_s, _block_onehot(seq))
        o_win = _band_attn(qa, k_w, v_w, NSA_WINDOW)
        tq = 128
        sink_rows = jnp.broadcast_to(sinks[l].astype(jnp.float32)[:, None], (N_HEADS, tq)).reshape(N_HEADS * tq, 1)
        o_b = _band_attn(qb, k_b, v_b, SWA_WINDOW, sink_rows, tq=tq)

        x = _merge_out(x, mod3, _prep_gate_weight(w_in[l]), _gate_expand_matrix(),
                       o_cmp, o_sel, o_win, o_b,
                       _pair_o_rows(w_up_a[l]).astype(jnp.bfloat16), _pair_o_rows(w_up_b[l]).astype(jnp.bfloat16),
                       w_out[l].astype(jnp.bfloat16), ln_g[l].reshape(1, D_MODEL), ln_b[l].reshape(1, D_MODEL))
    return x
```

```python
import functools

import numpy as np
import jax
import jax.numpy as jnp
from jax import lax
from jax.experimental import pallas as pl
from jax.experimental.pallas import tpu as pltpu

D_MODEL = 1024
HEAD_DIM = 64
HALF = HEAD_DIM // 2
N_HEADS = 8
N_KV = 2
GROUP = N_HEADS // N_KV
Q_WIDTH = N_HEADS * HEAD_DIM
KV_WIDTH = N_KV * HEAD_DIM
CMP_LEN = 32
CMP_STRIDE = 16
CMP_HIDDEN = 256
SEL_BLOCK = 64
SEL_TOPN = 8
NSA_WINDOW = 512
SWA_WINDOW = 128
ROPE_THETA = 10000.0
LN_EPS = 1e-5
NEG_INF = -1e30
FORCE_SCORE = 1e9
DEPTH = 1
DEEPNORM_ALPHA = (2 * DEPTH) ** 0.25
LANES = 128
BF16_SUBLANES = 16
V_ROWS = KV_WIDTH + BF16_SUBLANES
VMEM_LIMIT = 56 * 1024 * 1024

_SPLITS = [Q_WIDTH, 2 * KV_WIDTH, 2 * KV_WIDTH, 2 * KV_WIDTH, 3 * N_HEADS, Q_WIDTH,
           Q_WIDTH, 2 * KV_WIDTH, Q_WIDTH, 2 * D_MODEL]
_OFFS = [int(o) for o in np.cumsum([0] + _SPLITS)]
(OFF_QA, OFF_KVC, OFF_KVS, OFF_KVW, OFF_GN, OFF_ZA, OFF_QB, OFF_KVB, OFF_ZB, OFF_GM) = _OFFS[:10]


def _q_pair_perm():
    cols = []
    for j in range(GROUP):
        for half in range(2):
            for grp in range(N_KV):
                h = j + GROUP * grp
                cols += [h * HEAD_DIM + half * HALF + i for i in range(HALF)]
    return np.asarray(cols, np.int32)


def _k_pair_perm():
    cols = []
    for half in range(2):
        for grp in range(N_KV):
            cols += [grp * HEAD_DIM + half * HALF + i for i in range(HALF)]
    return np.asarray(cols, np.int32)


def _o_pair_perm():
    cols = []
    for j in range(GROUP):
        for grp in range(N_KV):
            h = j + GROUP * grp
            cols += [h * HEAD_DIM + d for d in range(HEAD_DIM)]
    return np.asarray(cols, np.int32)


Q_PERM = _q_pair_perm()
K_PERM = _k_pair_perm()
O_PERM = _o_pair_perm()


def _cparams(sem):
    return pltpu.CompilerParams(dimension_semantics=sem, vmem_limit_bytes=VMEM_LIMIT)


def _dot(a, b):
    return jnp.dot(a, b, preferred_element_type=jnp.float32)


def _dot_nt(a, b):
    return lax.dot_general(a, b, (((1,), (1,)), ((), ())), preferred_element_type=jnp.float32)


def _layer_norm(x):
    mu = jnp.mean(x, axis=-1, keepdims=True)
    xc = x - mu
    var = jnp.mean(xc * xc, axis=-1, keepdims=True)
    return xc * lax.rsqrt(var + LN_EPS)


def _sigmoid(x):
    return 1.0 / (1.0 + jnp.exp(-x))


def _adaln_kernel(c_ref, w_ref, b_ref, o_ref):
    c = c_ref[...]
    a = c * _sigmoid(c)
    o_ref[...] = jnp.dot(a, w_ref[...], preferred_element_type=jnp.float32,
                         precision=lax.Precision.HIGHEST) + b_ref[...]


def _adaln_mod(c, w_ada, b_ada):
    bsz = c.shape[0]
    n = w_ada.shape[1]
    tn = 512
    return pl.pallas_call(
        _adaln_kernel,
        out_shape=jax.ShapeDtypeStruct((bsz, n), jnp.float32),
        grid=(n // tn,),
        in_specs=[pl.BlockSpec((bsz, D_MODEL), lambda j: (0, 0)),
                  pl.BlockSpec((D_MODEL, tn), lambda j: (0, j)),
                  pl.BlockSpec((1, tn), lambda j: (0, j))],
        out_specs=pl.BlockSpec((bsz, tn), lambda j: (0, j)),
        compiler_params=_cparams(("arbitrary",)),
        name="adaln_mod",
    )(c, w_ada, b_ada.reshape(1, n))


def _rope_table_kernel(pos_ref, invf_ref, cos_ref, sin_ref):
    ang = pos_ref[...] * invf_ref[...]
    cos_ref[...] = jnp.cos(ang)
    sin_ref[...] = jnp.sin(ang)


def _rope_tables(positions):
    bsz, seq = positions.shape
    per_row = LANES // HALF
    rows = bsz * seq // per_row
    pos = jnp.broadcast_to(positions.astype(jnp.float32).reshape(rows, per_row, 1),
                           (rows, per_row, HALF)).reshape(rows, LANES)
    inv_freq = ROPE_THETA ** (-jnp.arange(HALF, dtype=jnp.float32) / HALF)
    invf = jnp.broadcast_to(inv_freq[None, :], (per_row, HALF)).reshape(1, LANES)
    tr = seq // per_row
    cos, sin = pl.pallas_call(
        _rope_table_kernel,
        out_shape=[jax.ShapeDtypeStruct((rows, LANES), jnp.float32)] * 2,
        grid=(rows // tr,),
        in_specs=[pl.BlockSpec((tr, LANES), lambda i: (i, 0)),
                  pl.BlockSpec((1, LANES), lambda i: (0, 0))],
        out_specs=[pl.BlockSpec((tr, LANES), lambda i: (i, 0))] * 2,
        compiler_params=_cparams(("arbitrary",)),
        name="rope_table",
    )(pos, invf)
    return cos.reshape(bsz, seq, HALF), sin.reshape(bsz, seq, HALF)


N_QK_SLABS = 13
N_VT = 3


def _qkv_kernel(x_ref, mod_ref, w_ref, wvt_ref, cos_ref, sin_ref,
                qa_ref, kc_ref, ks_ref, kw_ref, qb_ref, kb_ref, vc_ref,
                vst_ref, vwt_ref, vbt_ref):
    x = x_ref[0]
    shift = mod_ref[0, 0:1, :]
    scale = mod_ref[0, 1:2, :]
    u = (_layer_norm(x) * (1.0 + scale) + shift).astype(jnp.bfloat16)
    c32 = cos_ref[0]
    s32 = sin_ref[0]
    cos = jnp.concatenate([c32, c32, c32, c32], axis=1)
    sin = jnp.concatenate([-s32, -s32, s32, s32], axis=1)

    def rope(t):
        return t * cos + pltpu.roll(t, LANES // 2, 1) * sin

    def proj(lo, n_slabs):
        return _dot(u, w_ref[:, lo * LANES:(lo + n_slabs) * LANES])

    acc = proj(0, 4)
    for j in range(4):
        qa_ref[0, :, j * LANES:(j + 1) * LANES] = rope(acc[:, j * LANES:(j + 1) * LANES]).astype(jnp.bfloat16)
    acc = proj(4, 3)
    for j, ref in enumerate((kc_ref, ks_ref, kw_ref)):
        ref[0] = rope(acc[:, j * LANES:(j + 1) * LANES]).astype(jnp.bfloat16)
    acc = proj(7, 4)
    for j in range(4):
        qb_ref[0, :, j * LANES:(j + 1) * LANES] = rope(acc[:, j * LANES:(j + 1) * LANES]).astype(jnp.bfloat16)
    acc = proj(11, 2)
    kb_ref[0] = rope(acc[:, 0:LANES]).astype(jnp.bfloat16)
    vc_ref[0] = acc[:, LANES:2 * LANES].astype(jnp.bfloat16)
    vt = _dot_nt(wvt_ref[...], u)
    ones = jnp.ones((V_ROWS - KV_WIDTH, vt.shape[1]), jnp.bfloat16)
    for j, ref in enumerate((vst_ref, vwt_ref, vbt_ref)):
        ref[0, 0:KV_WIDTH, :] = vt[j * KV_WIDTH:(j + 1) * KV_WIDTH].astype(jnp.bfloat16)
        ref[0, KV_WIDTH:V_ROWS, :] = ones


def _qkv_proj(x, mod3, w_qk, w_vt, cos_t, sin_t, tm=512):
    bsz, seq, _ = x.shape
    q_shape = jax.ShapeDtypeStruct((bsz, seq, Q_WIDTH), jnp.bfloat16)
    k_shape = jax.ShapeDtypeStruct((bsz, seq, KV_WIDTH), jnp.bfloat16)
    vt_shape = jax.ShapeDtypeStruct((bsz, V_ROWS, seq), jnp.bfloat16)
    q_spec = pl.BlockSpec((1, tm, Q_WIDTH), lambda b, i: (b, i, 0))
    k_spec = pl.BlockSpec((1, tm, KV_WIDTH), lambda b, i: (b, i, 0))
    t_spec = pl.BlockSpec((1, tm, HALF), lambda b, i: (b, i, 0))
    vt_spec = pl.BlockSpec((1, V_ROWS, tm), lambda b, i: (b, 0, i))
    return pl.pallas_call(
        _qkv_kernel,
        out_shape=[q_shape, k_shape, k_shape, k_shape, q_shape, k_shape, k_shape,
                   vt_shape, vt_shape, vt_shape],
        grid=(bsz, seq // tm),
        in_specs=[pl.BlockSpec((1, tm, D_MODEL), lambda b, i: (b, i, 0)),
                  pl.BlockSpec((1, 3, D_MODEL), lambda b, i: (b, 0, 0)),
                  pl.BlockSpec((D_MODEL, N_QK_SLABS * LANES), lambda b, i: (0, 0)),
                  pl.BlockSpec((N_VT * KV_WIDTH, D_MODEL), lambda b, i: (0, 0)),
                  t_spec, t_spec],
        out_specs=[q_spec, k_spec, k_spec, k_spec, q_spec, k_spec, k_spec,
                   vt_spec, vt_spec, vt_spec],
        compiler_params=_cparams(("arbitrary", "arbitrary")),
        name="qkv_proj",
    )(x, mod3, w_qk, w_vt, cos_t, sin_t)


def _compress_kernel(xk_ref, xv_ref, w1k_ref, w1v_ref, posk_ref, posv_ref, w2k_ref, w2vt_ref,
                     kc_ref, vct_ref):
    def hidden(x_ref, w1_ref, pos_ref):
        n_chunks = x_ref.shape[1]
        hw = w1_ref.shape[1] // 2
        p = _dot(x_ref[0], w1_ref[...])
        pt = _dot(pos_ref[...], w1_ref[...])
        pos_term = pt[0:1, 0:hw] + pt[1:2, hw:]
        first = p[:, 0:hw]
        second = pltpu.roll(p[:, hw:], n_chunks - 1, 0)
        hid = first + second + pos_term
        return (hid * _sigmoid(hid)).astype(jnp.bfloat16)

    kc_ref[0] = _dot(hidden(xk_ref, w1k_ref, posk_ref), w2k_ref[...]).astype(jnp.bfloat16)
    vct_ref[0] = _dot_nt(w2vt_ref[...], hidden(xv_ref, w1v_ref, posv_ref)).astype(jnp.bfloat16)


def _compress(xk, xv, w1k, w1v, posk, posv, w2k, w2vt):
    bsz, n_chunks, cw = xk.shape
    hw2 = w1k.shape[1]
    out = jax.ShapeDtypeStruct((bsz, n_chunks, LANES), jnp.bfloat16)
    x_spec = pl.BlockSpec((1, n_chunks, cw), lambda b: (b, 0, 0))
    w1_spec = pl.BlockSpec((cw, hw2), lambda b: (0, 0))
    pos_spec = pl.BlockSpec((8, cw), lambda b: (0, 0))
    o_spec = pl.BlockSpec((1, n_chunks, LANES), lambda b: (b, 0, 0))
    return pl.pallas_call(
        _compress_kernel,
        out_shape=[out, out],
        grid=(bsz,),
        in_specs=[x_spec, x_spec, w1_spec, w1_spec, pos_spec, pos_spec,
                  pl.BlockSpec(w2k.shape, lambda b: (0, 0)), pl.BlockSpec(w2vt.shape, lambda b: (0, 0))],
        out_specs=[o_spec, o_spec],
        compiler_params=_cparams(("arbitrary",)),
        name="compress",
    )(xk, xv, w1k, w1v, posk, posv, w2k, w2vt)


def _stack_heads(q):
    lane = lax.broadcasted_iota(jnp.int32, (1, LANES), 1)
    is_a = (lane & (HEAD_DIM - 1)) < HALF
    zero = jnp.zeros((), q.dtype)
    tiles = [q[:, j * LANES:(j + 1) * LANES] for j in range(GROUP)]
    parts = [jnp.where(is_a, t, zero) for t in tiles] + [jnp.where(is_a, zero, t) for t in tiles]
    return jnp.concatenate(parts, axis=0)


def _store_heads(o_ref, o_t, inv_l, tq):
    for j in range(GROUP):
        a = slice(j * tq, (j + 1) * tq)
        b = slice((j + GROUP) * tq, (j + GROUP + 1) * tq)
        tile_t = jnp.concatenate([o_t[0:HEAD_DIM, a] * inv_l[:, a],
                                  o_t[HEAD_DIM:2 * HEAD_DIM, b] * inv_l[:, b]], axis=0)
        o_ref[0, :, j * LANES:(j + 1) * LANES] = tile_t.T.astype(jnp.bfloat16)


def _cmp_kernel(q_ref, kc_ref, vct_ref, ovt_ref, o_ref, bias_ref, *, tq, n_cmp, n_blk):
    qi = pl.program_id(1)
    nq = tq // LANES
    qs = _stack_heads(q_ref[0])
    s_t = _dot_nt(kc_ref[0], qs)
    c_row = lax.broadcasted_iota(jnp.int32, (LANES, tq), 0)
    t_col = qi * tq + lax.broadcasted_iota(jnp.int32, (LANES, tq), 1)
    vis = (c_row * CMP_STRIDE + (CMP_LEN - 1) <= t_col) & (c_row < n_cmp)
    visf = vis.astype(jnp.float32)
    ps = []
    for cb in range(N_HEADS * nq):
        qq = slice((cb % nq) * LANES, (cb % nq + 1) * LANES)
        s = jnp.where(vis[:, qq], s_t[:, cb * LANES:(cb + 1) * LANES], NEG_INF)
        e = jnp.exp(s - jnp.max(s, axis=0, keepdims=True))
        ps.append(e / jnp.sum(e, axis=0, keepdims=True) * visf[:, qq])
    p_t = jnp.concatenate(ps, axis=1)
    o_t = _dot(vct_ref[0], p_t.astype(jnp.bfloat16))
    _store_heads(o_ref, o_t, jnp.ones((1, N_HEADS * tq), jnp.float32), tq)

    j_blk = lax.broadcasted_iota(jnp.int32, (n_blk, tq), 0)
    t_q = qi * tq + lax.broadcasted_iota(jnp.int32, (n_blk, tq), 1)
    cur = jnp.right_shift(t_q, SEL_BLOCK.bit_length() - 1)
    forced = (j_blk == 0) | (j_blk == cur) | (j_blk == cur - 1)
    causal = j_blk <= cur
    rows = []
    for g in range(N_KV):
        pg = p_t[:, g * GROUP * tq:(g * GROUP + 1) * tq]
        for h in range(1, GROUP):
            pg = pg + p_t[:, (g * GROUP + h) * tq:(g * GROUP + h + 1) * tq]
        hi = pg.astype(jnp.bfloat16)
        lo = (pg - hi.astype(jnp.float32)).astype(jnp.bfloat16)
        imp_t = _dot(ovt_ref[...], hi) + _dot(ovt_ref[...], lo)
        score = jnp.where(forced, FORCE_SCORE, imp_t[0:n_blk, :])
        score = jnp.where(causal, score, NEG_INF)
        rank = jnp.zeros((n_blk, tq), jnp.int32)
        for i in range(n_blk):
            si = score[i:i + 1, :]
            beats = (si > score) | ((si == score) & (j_blk > i))
            rank = rank + beats.astype(jnp.int32)
        keep = (rank < SEL_TOPN) & causal
        rows.append(jnp.where(keep, 0.0, NEG_INF))
    rows.append(jnp.zeros((LANES - N_KV * n_blk, tq), jnp.float32))
    bias_t = jnp.concatenate(rows, axis=0)
    bias_ref[0] = bias_t.T.astype(jnp.bfloat16)


def _cmp_attn(qa, kc, vct, ovt, n_cmp, tq=256):
    bsz, seq, _ = qa.shape
    n_blk = seq // SEL_BLOCK
    kern = functools.partial(_cmp_kernel, tq=tq, n_cmp=n_cmp, n_blk=n_blk)
    kv_spec = pl.BlockSpec((1, LANES, LANES), lambda b, i: (b, 0, 0))
    return pl.pallas_call(
        kern,
        out_shape=[jax.ShapeDtypeStruct((bsz, seq, Q_WIDTH), jnp.bfloat16),
                   jax.ShapeDtypeStruct((bsz, seq, LANES), jnp.bfloat16)],
        grid=(bsz, seq // tq),
        in_specs=[pl.BlockSpec((1, tq, Q_WIDTH), lambda b, i: (b, i, 0)),
                  kv_spec, kv_spec,
                  pl.BlockSpec((LANES, LANES), lambda b, i: (0, 0))],
        out_specs=[pl.BlockSpec((1, tq, Q_WIDTH), lambda b, i: (b, i, 0)),
                   pl.BlockSpec((1, tq, LANES), lambda b, i: (b, i, 0))],
        compiler_params=_cparams(("arbitrary", "arbitrary")),
        name="cmp_attn",
    )(qa, kc, vct, ovt)


def _sel_kernel(q_ref, bias_ref, k_ref, vt_ref, blk_ref, o_ref, kaug_ref, acc_ref, m_ref, *, tq):
    qi = pl.program_id(1)
    tk = tq
    nq = tq // LANES
    n_cb = N_HEADS * nq

    @pl.when(qi == 0)
    def _():
        kaug_ref[:, 0:LANES] = k_ref[0]
        kaug_ref[:, LANES:2 * LANES] = blk_ref[...]

    qs = _stack_heads(q_ref[0])
    lane = lax.broadcasted_iota(jnp.int32, (1, LANES), 1)
    n_blk = blk_ref.shape[0] // SEL_BLOCK
    bt = bias_ref[0]
    zero = jnp.zeros((), bt.dtype)
    b0 = jnp.where(lane < n_blk, bt, zero)
    b1 = jnp.where((lane >= n_blk) & (lane < 2 * n_blk), bt, zero)
    qb = jnp.concatenate([b0] * GROUP + [b1] * GROUP, axis=0)
    qaug = jnp.concatenate([qs, qb], axis=1)

    acc_ref[...] = jnp.zeros_like(acc_ref)
    m_ref[...] = jnp.full_like(m_ref, NEG_INF)
    row = lax.broadcasted_iota(jnp.int32, (tk, tq), 0)
    col = lax.broadcasted_iota(jnp.int32, (tk, tq), 1)
    causal_bias = jnp.where(row <= col, 0.0, NEG_INF)

    def chunk(c, diagonal):
        k0 = pl.multiple_of(c * tk, tk)
        s_t = _dot_nt(kaug_ref[pl.ds(k0, tk), :], qaug)
        ps, alphas = [], []
        for cb in range(n_cb):
            cols = slice(cb * LANES, (cb + 1) * LANES)
            s = s_t[:, cols]
            if diagonal:
                s = s + causal_bias[:, (cb % nq) * LANES:(cb % nq + 1) * LANES]
            m_old = m_ref[0:1, cols]
            m_new = jnp.maximum(m_old, jnp.max(s, axis=0, keepdims=True))
            alphas.append(jnp.exp(m_old - m_new))
            ps.append(jnp.exp(s - m_new).astype(jnp.bfloat16))
            m_ref[0:1, cols] = m_new
        p_t = jnp.concatenate(ps, axis=1)
        alpha = jnp.concatenate(alphas, axis=1)
        pv = _dot(vt_ref[0, :, pl.ds(k0, tk)], p_t)
        acc_ref[...] = alpha * acc_ref[...] + pv

    def body(c, carry):
        chunk(c, False)
        return carry

    lax.fori_loop(0, qi, body, 0)
    chunk(qi, True)
    inv_l = 1.0 / acc_ref[KV_WIDTH:KV_WIDTH + 1, :]
    _store_heads(o_ref, acc_ref[0:KV_WIDTH, :], inv_l, tq)


def _sel_attn(qa, bias, ks, vst, blk_onehot, tq=256):
    bsz, seq, _ = qa.shape
    kern = functools.partial(_sel_kernel, tq=tq)
    return pl.pallas_call(
        kern,
        out_shape=jax.ShapeDtypeStruct((bsz, seq, Q_WIDTH), jnp.bfloat16),
        grid=(bsz, seq // tq),
        in_specs=[pl.BlockSpec((1, tq, Q_WIDTH), lambda b, i: (b, i, 0)),
                  pl.BlockSpec((1, tq, LANES), lambda b, i: (b, i, 0)),
                  pl.BlockSpec((1, seq, LANES), lambda b, i: (b, 0, 0)),
                  pl.BlockSpec((1, V_ROWS, seq), lambda b, i: (b, 0, 0)),
                  pl.BlockSpec((seq, LANES), lambda b, i: (0, 0))],
        out_specs=pl.BlockSpec((1, tq, Q_WIDTH), lambda b, i: (b, i, 0)),
        scratch_shapes=[pltpu.VMEM((seq, 2 * LANES), jnp.bfloat16),
                        pltpu.VMEM((V_ROWS, N_HEADS * tq), jnp.float32),
                        pltpu.VMEM((8, N_HEADS * tq), jnp.float32)],
        compiler_params=_cparams(("arbitrary", "arbitrary")),
        name="sel_attn",
    )(qa, bias, ks, vst, blk_onehot)


def _band_kernel(*refs, tq, span, window, use_sink):
    if use_sink:
        q_ref, k_ref, vt_ref, sink_ref, o_ref = refs
    else:
        q_ref, k_ref, vt_ref, o_ref = refs
    qi = pl.program_id(1)
    seq = k_ref.shape[1]
    start = jnp.clip(qi * tq + tq - span, 0, seq - span)
    start = pl.multiple_of(start, LANES)
    qs = _stack_heads(q_ref[0])
    s_t = _dot_nt(k_ref[0, pl.ds(start, span), :], qs)
    row = lax.broadcasted_iota(jnp.int32, (span, tq), 0)
    col = lax.broadcasted_iota(jnp.int32, (span, tq), 1)
    rel = (col - row) + (qi * tq - start)
    bias = jnp.where((rel >= 0) & (rel < window), 0.0, NEG_INF)
    nq = tq // LANES
    ps, extra = [], []
    for cb in range(N_HEADS * nq):
        s = s_t[:, cb * LANES:(cb + 1) * LANES] + bias[:, (cb % nq) * LANES:(cb % nq + 1) * LANES]
        m = jnp.max(s, axis=0, keepdims=True)
        if use_sink:
            sink = sink_ref[cb // nq]
            m = jnp.maximum(m, sink)
            extra.append(jnp.exp(sink - m))
        ps.append(jnp.exp(s - m).astype(jnp.bfloat16))
    p_t = jnp.concatenate(ps, axis=1)
    o_t = _dot(vt_ref[0, :, pl.ds(start, span)], p_t)
    l = o_t[KV_WIDTH:KV_WIDTH + 1, :]
    if use_sink:
        l = l + jnp.concatenate(extra, axis=1)
    _store_heads(o_ref, o_t[0:KV_WIDTH, :], 1.0 / l, tq)


def _band_attn(q, k, vt, window, sinks=None, tq=128):
    bsz, seq, _ = q.shape
    span = (-(-(window - 1) // tq)) * tq + tq
    use_sink = sinks is not None
    kern = functools.partial(_band_kernel, tq=tq, span=span, window=window, use_sink=use_sink)
    in_specs = [pl.BlockSpec((1, tq, Q_WIDTH), lambda b, i: (b, i, 0)),
                pl.BlockSpec((1, seq, LANES), lambda b, i: (b, 0, 0)),
                pl.BlockSpec((1, V_ROWS, seq), lambda b, i: (b, 0, 0))]
    args = [q, k, vt]
    if use_sink:
        in_specs.append(pl.BlockSpec(memory_space=pltpu.SMEM))
        args.append(sinks)
    return pl.pallas_call(
        kern,
        out_shape=jax.ShapeDtypeStruct((bsz, seq, Q_WIDTH), jnp.bfloat16),
        grid=(bsz, seq // tq),
        in_specs=in_specs,
        out_specs=pl.BlockSpec((1, tq, Q_WIDTH), lambda b, i: (b, i, 0)),
        compiler_params=_cparams(("arbitrary", "arbitrary")),
        name="band_attn_sink" if use_sink else "band_attn",
    )(*args)


GATE_WIDTH = LANES + 2 * Q_WIDTH + 2 * D_MODEL


def _merge_kernel(x_ref, mod_ref, wg_ref, gexp_ref, ocmp_ref, osel_ref, owin_ref, ob_ref,
                  wua_ref, wub_ref, wo_ref, lng_ref, lnb_ref, o_ref):
    x = x_ref[0]
    shift = mod_ref[0, 0:1, :]
    scale = mod_ref[0, 1:2, :]
    gate = mod_ref[0, 2:3, :]
    u = (_layer_norm(x) * (1.0 + scale) + shift).astype(jnp.bfloat16)

    sig = _sigmoid(_dot(u, wg_ref[:, 0:LANES]))
    hi = sig.astype(jnp.bfloat16)
    lo = (sig - hi.astype(jnp.float32)).astype(jnp.bfloat16)
    gx = _dot(hi, gexp_ref[...]) + _dot(lo, gexp_ref[...])
    o_a = (gx[:, 0:Q_WIDTH] * ocmp_ref[0].astype(jnp.float32)
           + gx[:, Q_WIDTH:2 * Q_WIDTH] * osel_ref[0].astype(jnp.float32)
           + gx[:, 2 * Q_WIDTH:3 * Q_WIDTH] * owin_ref[0].astype(jnp.float32))
    c0 = LANES
    z_a = _dot(u, wg_ref[:, c0:c0 + Q_WIDTH])
    y_a = (o_a * (z_a * _sigmoid(z_a))).astype(jnp.bfloat16)
    c0 += Q_WIDTH
    z_b = _dot(u, wg_ref[:, c0:c0 + Q_WIDTH])
    y_b = (ob_ref[0].astype(jnp.float32) * (z_b * _sigmoid(z_b))).astype(jnp.bfloat16)
    c0 += Q_WIDTH
    gm_a = _sigmoid(_dot(u, wg_ref[:, c0:c0 + D_MODEL]))
    merged = gm_a * _dot(y_a, wua_ref[...])
    c0 += D_MODEL
    gm_b = _sigmoid(_dot(u, wg_ref[:, c0:c0 + D_MODEL]))
    merged = merged + gm_b * _dot(y_b, wub_ref[...])
    out = _dot(merged.astype(jnp.bfloat16), wo_ref[...])
    y = DEEPNORM_ALPHA * x + gate * out
    o_ref[0] = _layer_norm(y) * lng_ref[...] + lnb_ref[...]


def _merge_out(x, mod3, wg, gexp, o_cmp, o_sel, o_win, o_b, wua, wub, wo, ln_g, ln_b, tm=256):
    bsz, seq, _ = x.shape
    tok = lambda w: pl.BlockSpec((1, tm, w), lambda b, i: (b, i, 0))
    full = lambda a: pl.BlockSpec(a.shape, lambda b, i: (0,) * a.ndim)
    return pl.pallas_call(
        _merge_kernel,
        out_shape=jax.ShapeDtypeStruct(x.shape, jnp.float32),
        grid=(bsz, seq // tm),
        in_specs=[tok(D_MODEL), pl.BlockSpec((1, 3, D_MODEL), lambda b, i: (b, 0, 0)),
                  full(wg), full(gexp), tok(Q_WIDTH), tok(Q_WIDTH), tok(Q_WIDTH), tok(Q_WIDTH),
                  full(wua), full(wub), full(wo), full(ln_g), full(ln_b)],
        out_specs=tok(D_MODEL),
        compiler_params=_cparams(("arbitrary", "arbitrary")),
        name="merge_out",
    )(x, mod3, wg, gexp, o_cmp, o_sel, o_win, o_b, wua, wub, wo, ln_g, ln_b)


def _pair_q_cols(w):
    k = w.shape[0]
    return w.reshape(k, N_KV, GROUP, 2, HALF).transpose(0, 2, 3, 1, 4).reshape(k, Q_WIDTH)


def _pair_k_cols(w):
    k = w.shape[0]
    return w.reshape(k, N_KV, 2, HALF).transpose(0, 2, 1, 3).reshape(k, KV_WIDTH)


def _pair_o_cols(w):
    k = w.shape[0]
    return w.reshape(k, N_KV, GROUP, HEAD_DIM).transpose(0, 2, 1, 3).reshape(k, Q_WIDTH)


def _pair_o_rows(w):
    n = w.shape[1]
    return w.reshape(N_KV, GROUP, HEAD_DIM, n).transpose(1, 0, 2, 3).reshape(Q_WIDTH, n)


def _prep_qkv_weight(w_in):
    q_scale = HEAD_DIM ** -0.5
    sec = lambda off, width: w_in[:, off:off + width]
    cols = [_pair_q_cols(sec(OFF_QA, Q_WIDTH)) * q_scale,
            _pair_k_cols(sec(OFF_KVC, KV_WIDTH)), _pair_k_cols(sec(OFF_KVS, KV_WIDTH)),
            _pair_k_cols(sec(OFF_KVW, KV_WIDTH)),
            _pair_q_cols(sec(OFF_QB, Q_WIDTH)) * q_scale,
            _pair_k_cols(sec(OFF_KVB, KV_WIDTH)),
            sec(OFF_KVC + KV_WIDTH, KV_WIDTH)]
    v_cols = [sec(OFF_KVS + KV_WIDTH, KV_WIDTH), sec(OFF_KVW + KV_WIDTH, KV_WIDTH),
              sec(OFF_KVB + KV_WIDTH, KV_WIDTH)]
    return (jnp.concatenate(cols, axis=1).astype(jnp.bfloat16),
            jnp.concatenate(v_cols, axis=1).T.astype(jnp.bfloat16))


def _prep_gate_weight(w_in):
    pad = jnp.zeros((D_MODEL, LANES - 3 * N_HEADS), w_in.dtype)
    cols = [w_in[:, OFF_GN:OFF_GN + 3 * N_HEADS], pad,
            _pair_o_cols(w_in[:, OFF_ZA:OFF_ZA + Q_WIDTH]), _pair_o_cols(w_in[:, OFF_ZB:OFF_ZB + Q_WIDTH]),
            w_in[:, OFF_GM:OFF_GM + 2 * D_MODEL]]
    return jnp.concatenate(cols, axis=1).astype(jnp.bfloat16)


def _gate_expand_matrix():
    e = np.zeros((LANES, 3 * Q_WIDTH), np.float32)
    for br in range(3):
        for c in range(Q_WIDTH):
            h = int(O_PERM[c]) // HEAD_DIM
            e[h * 3 + br, br * Q_WIDTH + c] = 1.0
    return jnp.asarray(e, jnp.bfloat16)


def _prep_compress(pos, w1, w2, paired):
    half_len = CMP_LEN // 2
    eye = jnp.eye(N_KV, dtype=w1.dtype)
    if paired:
        w1x = jnp.einsum("lhif,gk->lhgikf", w1.reshape(CMP_LEN, 2, HALF, CMP_HIDDEN), eye)
        w2e = jnp.einsum("fhi,kg->kfhgi", w2.reshape(CMP_HIDDEN, 2, HALF), eye)
        pos_l = jnp.broadcast_to(pos.reshape(CMP_LEN, 2, 1, HALF), (CMP_LEN, 2, N_KV, HALF))
    else:
        w1x = jnp.einsum("ldf,gk->lgdkf", w1.reshape(CMP_LEN, HEAD_DIM, CMP_HIDDEN), eye)
        w2e = jnp.einsum("fd,kg->kfgd", w2, eye)
        pos_l = jnp.broadcast_to(pos.reshape(CMP_LEN, 1, HEAD_DIM), (CMP_LEN, N_KV, HEAD_DIM))
    w1x = w1x.reshape(CMP_LEN, LANES, N_KV * CMP_HIDDEN)
    first = w1x[:half_len].reshape(half_len * LANES, N_KV * CMP_HIDDEN)
    second = w1x[half_len:].reshape(half_len * LANES, N_KV * CMP_HIDDEN)
    w1e = jnp.concatenate([first, second], axis=1).astype(jnp.bfloat16)
    pos_l = pos_l.reshape(CMP_LEN, LANES)
    pos_rows = jnp.concatenate([pos_l[:half_len].reshape(1, -1), pos_l[half_len:].reshape(1, -1),
                                jnp.zeros((6, half_len * LANES), pos.dtype)], axis=0).astype(jnp.bfloat16)
    return w1e, pos_rows, w2e.reshape(N_KV * CMP_HIDDEN, LANES).astype(jnp.bfloat16)


def _overlap_t(n_cmp, n_blk):
    c0 = np.arange(n_cmp) * CMP_STRIDE
    j0 = np.arange(n_blk) * SEL_BLOCK
    overlap = (c0[:, None] < j0[None, :] + SEL_BLOCK) & (c0[:, None] + CMP_LEN > j0[None, :])
    ovt = np.zeros((LANES, LANES), np.float32)
    ovt[:n_blk, :n_cmp] = overlap.T
    return jnp.asarray(ovt, jnp.bfloat16)


def _block_onehot(seq):
    n_blk = seq // SEL_BLOCK
    e = np.zeros((seq, LANES), np.float32)
    pos = np.arange(seq)
    e[pos, pos // SEL_BLOCK] = 1.0
    e[pos, n_blk + pos // SEL_BLOCK] = 1.0
    return jnp.asarray(e, jnp.bfloat16)


def kernel(x, c, positions, w_ada, b_ada, w_in, cmp_pos_k, cmp_w1_k, cmp_w2_k,
           cmp_pos_v, cmp_w1_v, cmp_w2_v, sinks, w_up_a, w_up_b, w_out, ln_g, ln_b):
    bsz, seq, _ = x.shape
    n_cmp = (seq - CMP_LEN) // CMP_STRIDE + 1
    n_blk = seq // SEL_BLOCK
    assert seq // CMP_STRIDE == LANES and 2 * n_blk <= LANES
    cos_t, sin_t = _rope_tables(positions)
    for l in range(DEPTH):
        mod3 = _adaln_mod(c, w_ada[l], b_ada[l]).reshape(bsz, 3, D_MODEL)
        w_qk, w_vt = _prep_qkv_weight(w_in[l])
        qa, k_c, k_s, k_w, qb, k_b, v_c, vt_s, vt_w, vt_b = _qkv_proj(x, mod3, w_qk, w_vt, cos_t, sin_t)

        w1k, posk, w2k = _prep_compress(cmp_pos_k[l], cmp_w1_k[l], cmp_w2_k[l], paired=True)
        w1v, posv, w2v = _prep_compress(cmp_pos_v[l], cmp_w1_v[l], cmp_w2_v[l], paired=False)
        chunk = lambda t: t.reshape(bsz, seq // CMP_STRIDE, CMP_STRIDE * KV_WIDTH)
        kc, vct = _compress(chunk(k_c), chunk(v_c), w1k, w1v, posk, posv, w2k, w2v.T)

        o_cmp, sel_bias = _cmp_attn(qa, kc, vct, _overlap_t(n_cmp, n_blk), n_cmp)
        o_sel = _sel_attn(qa, sel_bias, k_s, vt_s, _block_onehot(seq))
        o_win = _band_attn(qa, k_w, vt_w, NSA_WINDOW)
        o_b = _band_attn(qb, k_b, vt_b, SWA_WINDOW, sinks[l].astype(jnp.float32))

        x = _merge_out(x, mod3, _prep_gate_weight(w_in[l]), _gate_expand_matrix(),
                       o_cmp, o_sel, o_win, o_b,
                       _pair_o_rows(w_up_a[l]).astype(jnp.bfloat16), _pair_o_rows(w_up_b[l]).astype(jnp.bfloat16),
                       w_out[l].astype(jnp.bfloat16), ln_g[l].reshape(1, D_MODEL), ln_b[l].reshape(1, D_MODEL))
    return x
```

```python
import functools

import numpy as np
import jax
import jax.numpy as jnp
from jax import lax
from jax.experimental import pallas as pl
from jax.experimental.pallas import tpu as pltpu

D_MODEL = 1024
HEAD_DIM = 64
HALF = HEAD_DIM // 2
N_HEADS = 8
N_KV = 2
GROUP = N_HEADS // N_KV
Q_WIDTH = N_HEADS * HEAD_DIM
KV_WIDTH = N_KV * HEAD_DIM
CMP_LEN = 32
CMP_STRIDE = 16
CMP_HIDDEN = 256
SEL_BLOCK = 64
SEL_TOPN = 8
NSA_WINDOW = 512
SWA_WINDOW = 128
ROPE_THETA = 10000.0
LN_EPS = 1e-5
NEG_INF = -1e30
FORCE_SCORE = 1e9
DEPTH = 1
DEEPNORM_ALPHA = (2 * DEPTH) ** 0.25
LANES = 128
BF16_SUBLANES = 16
LOG2E = 1.4426950408889634
G_ROWS = HEAD_DIM + BF16_SUBLANES
V_ROWS = N_KV * G_ROWS
VMEM_LIMIT = 56 * 1024 * 1024

_SPLITS = [Q_WIDTH, 2 * KV_WIDTH, 2 * KV_WIDTH, 2 * KV_WIDTH, 3 * N_HEADS, Q_WIDTH,
           Q_WIDTH, 2 * KV_WIDTH, Q_WIDTH, 2 * D_MODEL]
_OFFS = [int(o) for o in np.cumsum([0] + _SPLITS)]
(OFF_QA, OFF_KVC, OFF_KVS, OFF_KVW, OFF_GN, OFF_ZA, OFF_QB, OFF_KVB, OFF_ZB, OFF_GM) = _OFFS[:10]


def _q_pair_perm():
    cols = []
    for j in range(GROUP):
        for half in range(2):
            for grp in range(N_KV):
                h = j + GROUP * grp
                cols += [h * HEAD_DIM + half * HALF + i for i in range(HALF)]
    return np.asarray(cols, np.int32)


def _k_pair_perm():
    cols = []
    for half in range(2):
        for grp in range(N_KV):
            cols += [grp * HEAD_DIM + half * HALF + i for i in range(HALF)]
    return np.asarray(cols, np.int32)


def _o_pair_perm():
    cols = []
    for j in range(GROUP):
        for grp in range(N_KV):
            h = j + GROUP * grp
            cols += [h * HEAD_DIM + d for d in range(HEAD_DIM)]
    return np.asarray(cols, np.int32)


Q_PERM = _q_pair_perm()
K_PERM = _k_pair_perm()
O_PERM = _o_pair_perm()


def _cparams(sem):
    return pltpu.CompilerParams(dimension_semantics=sem, vmem_limit_bytes=VMEM_LIMIT)


def _dot(a, b):
    return jnp.dot(a, b, preferred_element_type=jnp.float32)


def _dot_nt(a, b):
    return lax.dot_general(a, b, (((1,), (1,)), ((), ())), preferred_element_type=jnp.float32)


def _layer_norm(x):
    mu = jnp.mean(x, axis=-1, keepdims=True)
    xc = x - mu
    var = jnp.mean(xc * xc, axis=-1, keepdims=True)
    return xc * lax.rsqrt(var + LN_EPS)


def _sigmoid(x):
    return 1.0 / (1.0 + jnp.exp(-x))


def _adaln_kernel(c_ref, w_ref, b_ref, o_ref):
    c = c_ref[...]
    a = c * _sigmoid(c)
    o_ref[...] = jnp.dot(a, w_ref[...], preferred_element_type=jnp.float32,
                         precision=lax.Precision.HIGHEST) + b_ref[...]


def _adaln_mod(c, w_ada, b_ada):
    bsz = c.shape[0]
    n = w_ada.shape[1]
    tn = 512
    return pl.pallas_call(
        _adaln_kernel,
        out_shape=jax.ShapeDtypeStruct((bsz, n), jnp.float32),
        grid=(n // tn,),
        in_specs=[pl.BlockSpec((bsz, D_MODEL), lambda j: (0, 0)),
                  pl.BlockSpec((D_MODEL, tn), lambda j: (0, j)),
                  pl.BlockSpec((1, tn), lambda j: (0, j))],
        out_specs=pl.BlockSpec((bsz, tn), lambda j: (0, j)),
        compiler_params=_cparams(("arbitrary",)),
        name="adaln_mod",
    )(c, w_ada, b_ada.reshape(1, n))


def _rope_table_kernel(pos_ref, invf_ref, cos_ref, sin_ref):
    ang = pos_ref[...] * invf_ref[...]
    cos_ref[...] = jnp.cos(ang)
    sin_ref[...] = jnp.sin(ang)


def _rope_tables(positions):
    bsz, seq = positions.shape
    per_row = LANES // HALF
    rows = bsz * seq // per_row
    pos = jnp.broadcast_to(positions.astype(jnp.float32).reshape(rows, per_row, 1),
                           (rows, per_row, HALF)).reshape(rows, LANES)
    inv_freq = ROPE_THETA ** (-jnp.arange(HALF, dtype=jnp.float32) / HALF)
    invf = jnp.broadcast_to(inv_freq[None, :], (per_row, HALF)).reshape(1, LANES)
    tr = seq // per_row
    cos, sin = pl.pallas_call(
        _rope_table_kernel,
        out_shape=[jax.ShapeDtypeStruct((rows, LANES), jnp.float32)] * 2,
        grid=(rows // tr,),
        in_specs=[pl.BlockSpec((tr, LANES), lambda i: (i, 0)),
                  pl.BlockSpec((1, LANES), lambda i: (0, 0))],
        out_specs=[pl.BlockSpec((tr, LANES), lambda i: (i, 0))] * 2,
        compiler_params=_cparams(("arbitrary",)),
        name="rope_table",
    )(pos, invf)
    return cos.reshape(bsz, seq, HALF), sin.reshape(bsz, seq, HALF)


N_QK_SLABS = 13
N_VT = 3


def _qkv_kernel(x_ref, mod_ref, w_ref, wvt_ref, cos_ref, sin_ref,
                qa_ref, kc_ref, ks_ref, kw_ref, qb_ref, kb_ref, vc_ref,
                vst_ref, vwt_ref, vbt_ref):
    x = x_ref[0]
    shift = mod_ref[0, 0:1, :]
    scale = mod_ref[0, 1:2, :]
    u = (_layer_norm(x) * (1.0 + scale) + shift).astype(jnp.bfloat16)
    c32 = cos_ref[0]
    s32 = sin_ref[0]
    cos = jnp.concatenate([c32, c32, c32, c32], axis=1)
    sin = jnp.concatenate([-s32, -s32, s32, s32], axis=1)

    def rope(t):
        return t * cos + pltpu.roll(t, LANES // 2, 1) * sin

    def proj(lo, n_slabs):
        return _dot(u, w_ref[:, lo * LANES:(lo + n_slabs) * LANES])

    acc = proj(0, 4)
    for j in range(4):
        qa_ref[0, :, j * LANES:(j + 1) * LANES] = (rope(acc[:, j * LANES:(j + 1) * LANES]) * LOG2E).astype(jnp.bfloat16)
    acc = proj(4, 3)
    for j, ref in enumerate((kc_ref, ks_ref, kw_ref)):
        ref[0] = rope(acc[:, j * LANES:(j + 1) * LANES]).astype(jnp.bfloat16)
    acc = proj(7, 4)
    for j in range(4):
        qb_ref[0, :, j * LANES:(j + 1) * LANES] = (rope(acc[:, j * LANES:(j + 1) * LANES]) * LOG2E).astype(jnp.bfloat16)
    acc = proj(11, 2)
    kb_ref[0] = rope(acc[:, 0:LANES]).astype(jnp.bfloat16)
    vc_ref[0] = acc[:, LANES:2 * LANES].astype(jnp.bfloat16)
    vt = _dot_nt(wvt_ref[...], u)
    ones = jnp.ones((G_ROWS - HEAD_DIM, vt.shape[1]), jnp.bfloat16)
    for j, ref in enumerate((vst_ref, vwt_ref, vbt_ref)):
        for g in range(N_KV):
            lo = j * KV_WIDTH + g * HEAD_DIM
            ref[0, g * G_ROWS:g * G_ROWS + HEAD_DIM, :] = vt[lo:lo + HEAD_DIM].astype(jnp.bfloat16)
            ref[0, g * G_ROWS + HEAD_DIM:(g + 1) * G_ROWS, :] = ones


def _qkv_proj(x, mod3, w_qk, w_vt, cos_t, sin_t, tm=512):
    bsz, seq, _ = x.shape
    q_shape = jax.ShapeDtypeStruct((bsz, seq, Q_WIDTH), jnp.bfloat16)
    k_shape = jax.ShapeDtypeStruct((bsz, seq, KV_WIDTH), jnp.bfloat16)
    vt_shape = jax.ShapeDtypeStruct((bsz, V_ROWS, seq), jnp.bfloat16)
    q_spec = pl.BlockSpec((1, tm, Q_WIDTH), lambda b, i: (b, i, 0))
    k_spec = pl.BlockSpec((1, tm, KV_WIDTH), lambda b, i: (b, i, 0))
    t_spec = pl.BlockSpec((1, tm, HALF), lambda b, i: (b, i, 0))
    vt_spec = pl.BlockSpec((1, V_ROWS, tm), lambda b, i: (b, 0, i))
    return pl.pallas_call(
        _qkv_kernel,
        out_shape=[q_shape, k_shape, k_shape, k_shape, q_shape, k_shape, k_shape,
                   vt_shape, vt_shape, vt_shape],
        grid=(bsz, seq // tm),
        in_specs=[pl.BlockSpec((1, tm, D_MODEL), lambda b, i: (b, i, 0)),
                  pl.BlockSpec((1, 3, D_MODEL), lambda b, i: (b, 0, 0)),
                  pl.BlockSpec((D_MODEL, N_QK_SLABS * LANES), lambda b, i: (0, 0)),
                  pl.BlockSpec((N_VT * KV_WIDTH, D_MODEL), lambda b, i: (0, 0)),
                  t_spec, t_spec],
        out_specs=[q_spec, k_spec, k_spec, k_spec, q_spec, k_spec, k_spec,
                   vt_spec, vt_spec, vt_spec],
        compiler_params=_cparams(("arbitrary", "arbitrary")),
        name="qkv_proj",
    )(x, mod3, w_qk, w_vt, cos_t, sin_t)


def _compress_kernel(xk_ref, xv_ref, w1k_ref, w1v_ref, posk_ref, posv_ref, w2k_ref, w2vt_ref,
                     kc_ref, vct_ref):
    def hidden(x_ref, w1_ref, pos_ref):
        n_chunks = x_ref.shape[1]
        hw = w1_ref.shape[1] // 2
        p = _dot(x_ref[0], w1_ref[...])
        pt = _dot(pos_ref[...], w1_ref[...])
        pos_term = pt[0:1, 0:hw] + pt[1:2, hw:]
        first = p[:, 0:hw]
        second = pltpu.roll(p[:, hw:], n_chunks - 1, 0)
        hid = first + second + pos_term
        return (hid * _sigmoid(hid)).astype(jnp.bfloat16)

    kc_ref[0] = _dot(hidden(xk_ref, w1k_ref, posk_ref), w2k_ref[...]).astype(jnp.bfloat16)
    vct_ref[0] = _dot_nt(w2vt_ref[...], hidden(xv_ref, w1v_ref, posv_ref)).astype(jnp.bfloat16)


def _compress(xk, xv, w1k, w1v, posk, posv, w2k, w2vt):
    bsz, n_chunks, cw = xk.shape
    hw2 = w1k.shape[1]
    out = jax.ShapeDtypeStruct((bsz, n_chunks, LANES), jnp.bfloat16)
    x_spec = pl.BlockSpec((1, n_chunks, cw), lambda b: (b, 0, 0))
    w1_spec = pl.BlockSpec((cw, hw2), lambda b: (0, 0))
    pos_spec = pl.BlockSpec((8, cw), lambda b: (0, 0))
    o_spec = pl.BlockSpec((1, n_chunks, LANES), lambda b: (b, 0, 0))
    return pl.pallas_call(
        _compress_kernel,
        out_shape=[out, out],
        grid=(bsz,),
        in_specs=[x_spec, x_spec, w1_spec, w1_spec, pos_spec, pos_spec,
                  pl.BlockSpec(w2k.shape, lambda b: (0, 0)), pl.BlockSpec(w2vt.shape, lambda b: (0, 0))],
        out_specs=[o_spec, o_spec],
        compiler_params=_cparams(("arbitrary",)),
        name="compress",
    )(xk, xv, w1k, w1v, posk, posv, w2k, w2vt)


def _stack_heads(q):
    lane = lax.broadcasted_iota(jnp.int32, (1, LANES), 1)
    is_a = (lane & (HEAD_DIM - 1)) < HALF
    zero = jnp.zeros((), q.dtype)
    tiles = [q[:, j * LANES:(j + 1) * LANES] for j in range(GROUP)]
    parts = [jnp.where(is_a, t, zero) for t in tiles] + [jnp.where(is_a, zero, t) for t in tiles]
    return jnp.concatenate(parts, axis=0)


def _store_heads(o_ref, rows, o_g, inv_l, tq):
    for j in range(GROUP):
        cols = slice(j * tq, (j + 1) * tq)
        halves = [o_g[g][0:HEAD_DIM, cols] for g in range(N_KV)]
        if inv_l is not None:
            halves = [h * inv_l[g][:, cols] for g, h in enumerate(halves)]
        tile_t = jnp.concatenate(halves, axis=0)
        o_ref[0, rows, j * LANES:(j + 1) * LANES] = tile_t.T.astype(jnp.bfloat16)


def _group_cols(g, tq):
    return slice(g * GROUP * tq, (g + 1) * GROUP * tq)


def _cmp_kernel(q_ref, kc_ref, vct_ref, ovt_ref, o_ref, bias_ref, *, tq, n_cmp, n_blk):
    qi = pl.program_id(1)
    nq = tq // LANES
    qs = _stack_heads(q_ref[0])
    c_row = lax.broadcasted_iota(jnp.int32, (LANES, tq), 0)
    t_col = qi * tq + lax.broadcasted_iota(jnp.int32, (LANES, tq), 1)
    vis = (c_row * CMP_STRIDE + (CMP_LEN - 1) <= t_col) & (c_row < n_cmp)
    visf = vis.astype(jnp.float32)
    s_t = _dot_nt(kc_ref[0], qs)
    p_parts = []
    for cb in range(N_HEADS * nq):
        qq = slice((cb % nq) * LANES, (cb % nq + 1) * LANES)
        s = jnp.where(vis[:, qq], s_t[:, cb * LANES:(cb + 1) * LANES], NEG_INF)
        e = jnp.exp2(s - jnp.max(s, axis=0, keepdims=True))
        p_parts.append(e / jnp.sum(e, axis=0, keepdims=True) * visf[:, qq])
    p_t = jnp.concatenate(p_parts, axis=1)
    o_g = [_dot(vct_ref[0, g * HEAD_DIM:(g + 1) * HEAD_DIM, :], p_t[:, _group_cols(g, tq)].astype(jnp.bfloat16))
           for g in range(N_KV)]
    _store_heads(o_ref, slice(None), o_g, None, tq)

    j_blk = lax.broadcasted_iota(jnp.int32, (n_blk, tq), 0)
    t_q = qi * tq + lax.broadcasted_iota(jnp.int32, (n_blk, tq), 1)
    cur = jnp.right_shift(t_q, SEL_BLOCK.bit_length() - 1)
    forced = (j_blk == 0) | (j_blk == cur) | (j_blk == cur - 1)
    causal = j_blk <= cur
    rows = []
    for g in range(N_KV):
        pg = p_t[:, g * GROUP * tq:(g * GROUP + 1) * tq]
        for h in range(1, GROUP):
            pg = pg + p_t[:, (g * GROUP + h) * tq:(g * GROUP + h + 1) * tq]
        hi = pg.astype(jnp.bfloat16)
        lo = (pg - hi.astype(jnp.float32)).astype(jnp.bfloat16)
        imp_t = _dot(ovt_ref[...], hi) + _dot(ovt_ref[...], lo)
        score = jnp.where(forced, FORCE_SCORE, imp_t[0:n_blk, :])
        score = jnp.where(causal, score, NEG_INF)
        rank = jnp.zeros((n_blk, tq), jnp.int32)
        for i in range(n_blk):
            si = score[i:i + 1, :]
            beats = (si > score) | ((si == score) & (j_blk > i))
            rank = rank + beats.astype(jnp.int32)
        keep = (rank < SEL_TOPN) & causal
        rows.append(jnp.where(keep, 0.0, NEG_INF))
    rows.append(jnp.zeros((LANES - N_KV * n_blk, tq), jnp.float32))
    bias_t = jnp.concatenate(rows, axis=0)
    bias_ref[0] = bias_t.T.astype(jnp.bfloat16)


def _cmp_attn(qa, kc, vct, ovt, n_cmp, tq=256):
    bsz, seq, _ = qa.shape
    n_blk = seq // SEL_BLOCK
    kern = functools.partial(_cmp_kernel, tq=tq, n_cmp=n_cmp, n_blk=n_blk)
    kv_spec = pl.BlockSpec((1, LANES, LANES), lambda b, i: (b, 0, 0))
    return pl.pallas_call(
        kern,
        out_shape=[jax.ShapeDtypeStruct((bsz, seq, Q_WIDTH), jnp.bfloat16),
                   jax.ShapeDtypeStruct((bsz, seq, LANES), jnp.bfloat16)],
        grid=(bsz, seq // tq),
        in_specs=[pl.BlockSpec((1, tq, Q_WIDTH), lambda b, i: (b, i, 0)),
                  kv_spec, kv_spec,
                  pl.BlockSpec((LANES, LANES), lambda b, i: (0, 0))],
        out_specs=[pl.BlockSpec((1, tq, Q_WIDTH), lambda b, i: (b, i, 0)),
                   pl.BlockSpec((1, tq, LANES), lambda b, i: (b, i, 0))],
        compiler_params=_cparams(("arbitrary", "arbitrary")),
        name="cmp_attn",
    )(qa, kc, vct, ovt)


def _sel_kernel(q_ref, bias_ref, k_ref, vt_ref, blk_ref, o_ref, kaug_ref, qaug_ref, s0_ref, s1_ref, acc_ref, m_ref,
                *, tq):
    qi = pl.program_id(1)
    tk = tq
    nq = tq // LANES
    per_g = GROUP * nq

    @pl.when(qi == 0)
    def _():
        kaug_ref[:, 0:LANES] = k_ref[0]
        kaug_ref[:, LANES:2 * LANES] = blk_ref[...]

    lane = lax.broadcasted_iota(jnp.int32, (1, LANES), 1)
    n_blk = blk_ref.shape[0] // SEL_BLOCK
    bt = bias_ref[0]
    zero = jnp.zeros((), bt.dtype)
    b0 = jnp.where(lane < n_blk, bt, zero)
    b1 = jnp.where((lane >= n_blk) & (lane < 2 * n_blk), bt, zero)
    qaug_ref[:, 0:LANES] = _stack_heads(q_ref[0])
    qaug_ref[:, LANES:2 * LANES] = jnp.concatenate([b0] * GROUP + [b1] * GROUP, axis=0)

    acc_ref[...] = jnp.zeros_like(acc_ref)
    m_ref[...] = jnp.full_like(m_ref, NEG_INF)
    row = lax.broadcasted_iota(jnp.int32, (tk, tq), 0)
    col = lax.broadcasted_iota(jnp.int32, (tk, tq), 1)
    causal_bias = jnp.where(row <= col, 0.0, NEG_INF)

    def scores(c, s_ref):
        k0 = pl.multiple_of(c * tk, tk)
        s_ref[...] = _dot_nt(kaug_ref[pl.ds(k0, tk), :], qaug_ref[...])

    def accumulate(c, s_ref, diagonal):
        k0 = pl.multiple_of(c * tk, tk)
        for g in range(N_KV):
            ps, alphas = [], []
            for i in range(per_g):
                cb = g * per_g + i
                cols = slice(cb * LANES, (cb + 1) * LANES)
                s = s_ref[:, cols]
                if diagonal:
                    s = s + causal_bias[:, (cb % nq) * LANES:(cb % nq + 1) * LANES]
                m_old = m_ref[0:1, cols]
                m_new = jnp.maximum(m_old, jnp.max(s, axis=0, keepdims=True))
                m_ref[0:1, cols] = m_new
                alphas.append(jnp.exp2(m_old - m_new))
                ps.append(jnp.exp2(s - m_new).astype(jnp.bfloat16))
            vt_g = vt_ref[0, g * G_ROWS:(g + 1) * G_ROWS, pl.ds(k0, tk)]
            pv = _dot(vt_g, jnp.concatenate(ps, axis=1))
            acc_ref[g] = jnp.concatenate(alphas, axis=1) * acc_ref[g] + pv

    scores(0, s0_ref)

    def body(i, carry):
        c = 2 * i
        scores(c + 1, s1_ref)
        accumulate(c, s0_ref, False)
        scores(c + 2, s0_ref)
        accumulate(c + 1, s1_ref, False)
        return carry

    lax.fori_loop(0, qi // 2, body, 0)

    @pl.when(qi % 2 == 1)
    def _():
        scores(qi, s1_ref)
        accumulate(qi - 1, s0_ref, False)
        accumulate(qi, s1_ref, True)

    @pl.when(qi % 2 == 0)
    def _():
        accumulate(qi, s0_ref, True)

    o_g = [acc_ref[g] for g in range(N_KV)]
    inv_l = [1.0 / o[HEAD_DIM:HEAD_DIM + 1, :] for o in o_g]
    _store_heads(o_ref, slice(None), o_g, inv_l, tq)


def _sel_attn(qa, bias, ks, vst, blk_onehot, tq=256):
    bsz, seq, _ = qa.shape
    kern = functools.partial(_sel_kernel, tq=tq)
    return pl.pallas_call(
        kern,
        out_shape=jax.ShapeDtypeStruct((bsz, seq, Q_WIDTH), jnp.bfloat16),
        grid=(bsz, seq // tq),
        in_specs=[pl.BlockSpec((1, tq, Q_WIDTH), lambda b, i: (b, i, 0)),
                  pl.BlockSpec((1, tq, LANES), lambda b, i: (b, i, 0)),
                  pl.BlockSpec((1, seq, LANES), lambda b, i: (b, 0, 0)),
                  pl.BlockSpec((1, V_ROWS, seq), lambda b, i: (b, 0, 0)),
                  pl.BlockSpec((seq, LANES), lambda b, i: (0, 0))],
        out_specs=pl.BlockSpec((1, tq, Q_WIDTH), lambda b, i: (b, i, 0)),
        scratch_shapes=[pltpu.VMEM((seq, 2 * LANES), jnp.bfloat16),
                        pltpu.VMEM((N_HEADS * tq, 2 * LANES), jnp.bfloat16),
                        pltpu.VMEM((tq, N_HEADS * tq), jnp.float32),
                        pltpu.VMEM((tq, N_HEADS * tq), jnp.float32),
                        pltpu.VMEM((N_KV, G_ROWS, GROUP * tq), jnp.float32),
                        pltpu.VMEM((8, N_HEADS * tq), jnp.float32)],
        compiler_params=_cparams(("arbitrary", "arbitrary")),
        name="sel_attn",
    )(qa, bias, ks, vst, blk_onehot)


def _band_kernel(*refs, tq, n_sub, span, window, use_sink):
    if use_sink:
        q_ref, k_ref, vt_ref, sink_ref, o_ref = refs
    else:
        q_ref, k_ref, vt_ref, o_ref = refs
    step = pl.program_id(1)
    seq = k_ref.shape[1]
    nq = tq // LANES
    row = lax.broadcasted_iota(jnp.int32, (span, tq), 0)
    col = lax.broadcasted_iota(jnp.int32, (span, tq), 1)
    diff = col - row
    def scores(sub):
        q0 = (step * n_sub + sub) * tq
        start = pl.multiple_of(jnp.clip(q0 + tq - span, 0, seq - span), LANES)
        rel = diff + (q0 - start)
        bias = jnp.where((rel >= 0) & (rel < window), 0.0, NEG_INF)
        qs = _stack_heads(q_ref[0, sub * tq:(sub + 1) * tq, :])
        return start, bias, _dot_nt(k_ref[0, pl.ds(start, span), :], qs)

    def finish(sub, start, bias, s_t):
        ps, extra = [], []
        for cb in range(N_HEADS * nq):
            s = s_t[:, cb * LANES:(cb + 1) * LANES] + bias[:, (cb % nq) * LANES:(cb % nq + 1) * LANES]
            m = jnp.max(s, axis=0, keepdims=True)
            if use_sink:
                sink = sink_ref[cb // nq] * LOG2E
                m = jnp.maximum(m, sink)
                extra.append(jnp.exp2(sink - m))
            ps.append(jnp.exp2(s - m).astype(jnp.bfloat16))
        o_g, inv_l = [], []
        for g in range(N_KV):
            per_g = GROUP * nq
            vt_g = vt_ref[0, g * G_ROWS:(g + 1) * G_ROWS, pl.ds(start, span)]
            o = _dot(vt_g, jnp.concatenate(ps[g * per_g:(g + 1) * per_g], axis=1))
            l = o[HEAD_DIM:HEAD_DIM + 1, :]
            if use_sink:
                l = l + jnp.concatenate(extra[g * per_g:(g + 1) * per_g], axis=1)
            o_g.append(o)
            inv_l.append(1.0 / l)
        _store_heads(o_ref, slice(sub * tq, (sub + 1) * tq), o_g, inv_l, tq)

    nxt = scores(0)
    for sub in range(n_sub):
        cur = nxt
        if sub + 1 < n_sub:
            nxt = scores(sub + 1)
        finish(sub, *cur)


def _band_attn(q, k, vt, window, sinks=None, tq=128, n_sub=4):
    bsz, seq, _ = q.shape
    span = (-(-(window - 1) // tq)) * tq + tq
    use_sink = sinks is not None
    kern = functools.partial(_band_kernel, tq=tq, n_sub=n_sub, span=span, window=window, use_sink=use_sink)
    ts = tq * n_sub
    in_specs = [pl.BlockSpec((1, ts, Q_WIDTH), lambda b, i: (b, i, 0)),
                pl.BlockSpec((1, seq, LANES), lambda b, i: (b, 0, 0)),
                pl.BlockSpec((1, V_ROWS, seq), lambda b, i: (b, 0, 0))]
    args = [q, k, vt]
    if use_sink:
        in_specs.append(pl.BlockSpec(memory_space=pltpu.SMEM))
        args.append(sinks)
    return pl.pallas_call(
        kern,
        out_shape=jax.ShapeDtypeStruct((bsz, seq, Q_WIDTH), jnp.bfloat16),
        grid=(bsz, seq // ts),
        in_specs=in_specs,
        out_specs=pl.BlockSpec((1, ts, Q_WIDTH), lambda b, i: (b, i, 0)),
        compiler_params=_cparams(("arbitrary", "arbitrary")),
        name="band_attn_sink" if use_sink else "band_attn",
    )(*args)


GATE_WIDTH = LANES + 2 * Q_WIDTH + 2 * D_MODEL


def _merge_kernel(x_ref, mod_ref, wg_ref, gexp_ref, ocmp_ref, osel_ref, owin_ref, ob_ref,
                  wua_ref, wub_ref, wo_ref, lng_ref, lnb_ref, o_ref):
    x = x_ref[0]
    shift = mod_ref[0, 0:1, :]
    scale = mod_ref[0, 1:2, :]
    gate = mod_ref[0, 2:3, :]
    u = (_layer_norm(x) * (1.0 + scale) + shift).astype(jnp.bfloat16)

    sig = _sigmoid(_dot(u, wg_ref[:, 0:LANES]))
    hi = sig.astype(jnp.bfloat16)
    lo = (sig - hi.astype(jnp.float32)).astype(jnp.bfloat16)
    gx = _dot(hi, gexp_ref[...]) + _dot(lo, gexp_ref[...])
    o_a = (gx[:, 0:Q_WIDTH] * ocmp_ref[0].astype(jnp.float32)
           + gx[:, Q_WIDTH:2 * Q_WIDTH] * osel_ref[0].astype(jnp.float32)
           + gx[:, 2 * Q_WIDTH:3 * Q_WIDTH] * owin_ref[0].astype(jnp.float32))
    c0 = LANES
    z_a = _dot(u, wg_ref[:, c0:c0 + Q_WIDTH])
    y_a = (o_a * (z_a * _sigmoid(z_a))).astype(jnp.bfloat16)
    c0 += Q_WIDTH
    z_b = _dot(u, wg_ref[:, c0:c0 + Q_WIDTH])
    y_b = (ob_ref[0].astype(jnp.float32) * (z_b * _sigmoid(z_b))).astype(jnp.bfloat16)
    c0 += Q_WIDTH
    gm_a = _sigmoid(_dot(u, wg_ref[:, c0:c0 + D_MODEL]))
    merged = gm_a * _dot(y_a, wua_ref[...])
    c0 += D_MODEL
    gm_b = _sigmoid(_dot(u, wg_ref[:, c0:c0 + D_MODEL]))
    merged = merged + gm_b * _dot(y_b, wub_ref[...])
    out = _dot(merged.astype(jnp.bfloat16), wo_ref[...])
    y = DEEPNORM_ALPHA * x + gate * out
    o_ref[0] = _layer_norm(y) * lng_ref[...] + lnb_ref[...]


def _merge_out(x, mod3, wg, gexp, o_cmp, o_sel, o_win, o_b, wua, wub, wo, ln_g, ln_b, tm=256):
    bsz, seq, _ = x.shape
    tok = lambda w: pl.BlockSpec((1, tm, w), lambda b, i: (b, i, 0))
    full = lambda a: pl.BlockSpec(a.shape, lambda b, i: (0,) * a.ndim)
    return pl.pallas_call(
        _merge_kernel,
        out_shape=jax.ShapeDtypeStruct(x.shape, jnp.float32),
        grid=(bsz, seq // tm),
        in_specs=[tok(D_MODEL), pl.BlockSpec((1, 3, D_MODEL), lambda b, i: (b, 0, 0)),
                  full(wg), full(gexp), tok(Q_WIDTH), tok(Q_WIDTH), tok(Q_WIDTH), tok(Q_WIDTH),
                  full(wua), full(wub), full(wo), full(ln_g), full(ln_b)],
        out_specs=tok(D_MODEL),
        compiler_params=_cparams(("arbitrary", "arbitrary")),
        name="merge_out",
    )(x, mod3, wg, gexp, o_cmp, o_sel, o_win, o_b, wua, wub, wo, ln_g, ln_b)


def _pair_q_cols(w):
    k = w.shape[0]
    return w.reshape(k, N_KV, GROUP, 2, HALF).transpose(0, 2, 3, 1, 4).reshape(k, Q_WIDTH)


def _pair_k_cols(w):
    k = w.shape[0]
    return w.reshape(k, N_KV, 2, HALF).transpose(0, 2, 1, 3).reshape(k, KV_WIDTH)


def _pair_o_cols(w):
    k = w.shape[0]
    return w.reshape(k, N_KV, GROUP, HEAD_DIM).transpose(0, 2, 1, 3).reshape(k, Q_WIDTH)


def _pair_o_rows(w):
    n = w.shape[1]
    return w.reshape(N_KV, GROUP, HEAD_DIM, n).transpose(1, 0, 2, 3).reshape(Q_WIDTH, n)


def _prep_qkv_weight(w_in):
    q_scale = HEAD_DIM ** -0.5
    sec = lambda off, width: w_in[:, off:off + width]
    cols = [_pair_q_cols(sec(OFF_QA, Q_WIDTH)) * q_scale,
            _pair_k_cols(sec(OFF_KVC, KV_WIDTH)), _pair_k_cols(sec(OFF_KVS, KV_WIDTH)),
            _pair_k_cols(sec(OFF_KVW, KV_WIDTH)),
            _pair_q_cols(sec(OFF_QB, Q_WIDTH)) * q_scale,
            _pair_k_cols(sec(OFF_KVB, KV_WIDTH)),
            sec(OFF_KVC + KV_WIDTH, KV_WIDTH)]
    v_cols = [sec(OFF_KVS + KV_WIDTH, KV_WIDTH), sec(OFF_KVW + KV_WIDTH, KV_WIDTH),
              sec(OFF_KVB + KV_WIDTH, KV_WIDTH)]
    return (jnp.concatenate(cols, axis=1).astype(jnp.bfloat16),
            jnp.concatenate(v_cols, axis=1).T.astype(jnp.bfloat16))


def _prep_gate_weight(w_in):
    pad = jnp.zeros((D_MODEL, LANES - 3 * N_HEADS), w_in.dtype)
    cols = [w_in[:, OFF_GN:OFF_GN + 3 * N_HEADS], pad,
            _pair_o_cols(w_in[:, OFF_ZA:OFF_ZA + Q_WIDTH]), _pair_o_cols(w_in[:, OFF_ZB:OFF_ZB + Q_WIDTH]),
            w_in[:, OFF_GM:OFF_GM + 2 * D_MODEL]]
    return jnp.concatenate(cols, axis=1).astype(jnp.bfloat16)


def _gate_expand_matrix():
    e = np.zeros((LANES, 3 * Q_WIDTH), np.float32)
    for br in range(3):
        for c in range(Q_WIDTH):
            h = int(O_PERM[c]) // HEAD_DIM
            e[h * 3 + br, br * Q_WIDTH + c] = 1.0
    return jnp.asarray(e, jnp.bfloat16)


def _prep_compress(pos, w1, w2, paired):
    half_len = CMP_LEN // 2
    eye = jnp.eye(N_KV, dtype=w1.dtype)
    if paired:
        w1x = jnp.einsum("lhif,gk->lhgikf", w1.reshape(CMP_LEN, 2, HALF, CMP_HIDDEN), eye)
        w2e = jnp.einsum("fhi,kg->kfhgi", w2.reshape(CMP_HIDDEN, 2, HALF), eye)
        pos_l = jnp.broadcast_to(pos.reshape(CMP_LEN, 2, 1, HALF), (CMP_LEN, 2, N_KV, HALF))
    else:
        w1x = jnp.einsum("ldf,gk->lgdkf", w1.reshape(CMP_LEN, HEAD_DIM, CMP_HIDDEN), eye)
        w2e = jnp.einsum("fd,kg->kfgd", w2, eye)
        pos_l = jnp.broadcast_to(pos.reshape(CMP_LEN, 1, HEAD_DIM), (CMP_LEN, N_KV, HEAD_DIM))
    w1x = w1x.reshape(CMP_LEN, LANES, N_KV * CMP_HIDDEN)
    first = w1x[:half_len].reshape(half_len * LANES, N_KV * CMP_HIDDEN)
    second = w1x[half_len:].reshape(half_len * LANES, N_KV * CMP_HIDDEN)
    w1e = jnp.concatenate([first, second], axis=1).astype(jnp.bfloat16)
    pos_l = pos_l.reshape(CMP_LEN, LANES)
    pos_rows = jnp.concatenate([pos_l[:half_len].reshape(1, -1), pos_l[half_len:].reshape(1, -1),
                                jnp.zeros((6, half_len * LANES), pos.dtype)], axis=0).astype(jnp.bfloat16)
    return w1e, pos_rows, w2e.reshape(N_KV * CMP_HIDDEN, LANES).astype(jnp.bfloat16)


def _overlap_t(n_cmp, n_blk):
    c0 = np.arange(n_cmp) * CMP_STRIDE
    j0 = np.arange(n_blk) * SEL_BLOCK
    overlap = (c0[:, None] < j0[None, :] + SEL_BLOCK) & (c0[:, None] + CMP_LEN > j0[None, :])
    ovt = np.zeros((LANES, LANES), np.float32)
    ovt[:n_blk, :n_cmp] = overlap.T
    return jnp.asarray(ovt, jnp.bfloat16)


def _block_onehot(seq):
    n_blk = seq // SEL_BLOCK
    e = np.zeros((seq, LANES), np.float32)
    pos = np.arange(seq)
    e[pos, pos // SEL_BLOCK] = 1.0
    e[pos, n_blk + pos // SEL_BLOCK] = 1.0
    return jnp.asarray(e, jnp.bfloat16)


def kernel(x, c, positions, w_ada, b_ada, w_in, cmp_pos_k, cmp_w1_k, cmp_w2_k,
           cmp_pos_v, cmp_w1_v, cmp_w2_v, sinks, w_up_a, w_up_b, w_out, ln_g, ln_b):
    bsz, seq, _ = x.shape
    n_cmp = (seq - CMP_LEN) // CMP_STRIDE + 1
    n_blk = seq // SEL_BLOCK
    assert seq // CMP_STRIDE == LANES and 2 * n_blk <= LANES
    cos_t, sin_t = _rope_tables(positions)
    for l in range(DEPTH):
        mod3 = _adaln_mod(c, w_ada[l], b_ada[l]).reshape(bsz, 3, D_MODEL)
        w_qk, w_vt = _prep_qkv_weight(w_in[l])
        qa, k_c, k_s, k_w, qb, k_b, v_c, vt_s, vt_w, vt_b = _qkv_proj(x, mod3, w_qk, w_vt, cos_t, sin_t)

        w1k, posk, w2k = _prep_compress(cmp_pos_k[l], cmp_w1_k[l], cmp_w2_k[l], paired=True)
        w1v, posv, w2v = _prep_compress(cmp_pos_v[l], cmp_w1_v[l], cmp_w2_v[l], paired=False)
        chunk = lambda t: t.reshape(bsz, seq // CMP_STRIDE, CMP_STRIDE * KV_WIDTH)
        kc, vct = _compress(chunk(k_c), chunk(v_c), w1k, w1v, posk, posv, w2k, w2v.T)

        o_cmp, sel_bias = _cmp_attn(qa, kc, vct, _overlap_t(n_cmp, n_blk), n_cmp)
        o_sel = _sel_attn(qa, sel_bias, k_s, vt_s, _block_onehot(seq))
        o_win = _band_attn(qa, k_w, vt_w, NSA_WINDOW)
        o_b = _band_attn(qb, k_b, vt_b, SWA_WINDOW, sinks[l].astype(jnp.float32))

        x = _merge_out(x, mod3, _prep_gate_weight(w_in[l]), _gate_expand_matrix(),
                       o_cmp, o_sel, o_win, o_b,
                       _pair_o_rows(w_up_a[l]).astype(jnp.bfloat16), _pair_o_rows(w_up_b[l]).astype(jnp.bfloat16),
                       w_out[l].astype(jnp.bfloat16), ln_g[l].reshape(1, D_MODEL), ln_b[l].reshape(1, D_MODEL))
    return x
```

```python
import functools

import numpy as np
import jax
import jax.numpy as jnp
from jax import lax
from jax.experimental import pallas as pl
from jax.experimental.pallas import tpu as pltpu

D_MODEL = 1024
HEAD_DIM = 64
HALF = HEAD_DIM // 2
N_HEADS = 8
N_KV = 2
GROUP = N_HEADS // N_KV
Q_WIDTH = N_HEADS * HEAD_DIM
KV_WIDTH = N_KV * HEAD_DIM
CMP_LEN = 32
CMP_STRIDE = 16
CMP_HIDDEN = 256
SEL_BLOCK = 64
SEL_TOPN = 8
NSA_WINDOW = 512
SWA_WINDOW = 128
ROPE_THETA = 10000.0
LN_EPS = 1e-5
NEG_INF = -1e30
FORCE_SCORE = 1e9
DEPTH = 1
DEEPNORM_ALPHA = (2 * DEPTH) ** 0.25
LANES = 128
SUBLANES = 8
BF16_SUBLANES = 16
LOG2E = 1.4426950408889634
G_ROWS = HEAD_DIM + BF16_SUBLANES
V_ROWS = N_KV * G_ROWS
VMEM_LIMIT = 56 * 1024 * 1024

_SPLITS = [Q_WIDTH, 2 * KV_WIDTH, 2 * KV_WIDTH, 2 * KV_WIDTH, 3 * N_HEADS, Q_WIDTH,
           Q_WIDTH, 2 * KV_WIDTH, Q_WIDTH, 2 * D_MODEL]
_OFFS = [int(o) for o in np.cumsum([0] + _SPLITS)]
(OFF_QA, OFF_KVC, OFF_KVS, OFF_KVW, OFF_GN, OFF_ZA, OFF_QB, OFF_KVB, OFF_ZB, OFF_GM) = _OFFS[:10]


def _q_pair_perm():
    cols = []
    for j in range(GROUP):
        for half in range(2):
            for grp in range(N_KV):
                h = j + GROUP * grp
                cols += [h * HEAD_DIM + half * HALF + i for i in range(HALF)]
    return np.asarray(cols, np.int32)


def _k_pair_perm():
    cols = []
    for half in range(2):
        for grp in range(N_KV):
            cols += [grp * HEAD_DIM + half * HALF + i for i in range(HALF)]
    return np.asarray(cols, np.int32)


def _o_pair_perm():
    cols = []
    for j in range(GROUP):
        for grp in range(N_KV):
            h = j + GROUP * grp
            cols += [h * HEAD_DIM + d for d in range(HEAD_DIM)]
    return np.asarray(cols, np.int32)


Q_PERM = _q_pair_perm()
K_PERM = _k_pair_perm()
O_PERM = _o_pair_perm()


def _cparams(sem):
    return pltpu.CompilerParams(dimension_semantics=sem, vmem_limit_bytes=VMEM_LIMIT)


def _dot(a, b):
    return jnp.dot(a, b, preferred_element_type=jnp.float32)


def _dot_nt(a, b):
    return lax.dot_general(a, b, (((1,), (1,)), ((), ())), preferred_element_type=jnp.float32)


def _layer_norm(x):
    mu = jnp.mean(x, axis=-1, keepdims=True)
    xc = x - mu
    var = jnp.mean(xc * xc, axis=-1, keepdims=True)
    return xc * lax.rsqrt(var + LN_EPS)


def _sigmoid(x):
    return 1.0 / (1.0 + jnp.exp(-x))


def _adaln_kernel(c_ref, w_ref, b_ref, o_ref):
    c = c_ref[...]
    a = c * _sigmoid(c)
    o_ref[...] = jnp.dot(a, w_ref[...], preferred_element_type=jnp.float32,
                         precision=lax.Precision.HIGHEST) + b_ref[...]


def _adaln_mod(c, w_ada, b_ada):
    bsz = c.shape[0]
    n = w_ada.shape[1]
    tn = 512
    return pl.pallas_call(
        _adaln_kernel,
        out_shape=jax.ShapeDtypeStruct((bsz, n), jnp.float32),
        grid=(n // tn,),
        in_specs=[pl.BlockSpec((bsz, D_MODEL), lambda j: (0, 0)),
                  pl.BlockSpec((D_MODEL, tn), lambda j: (0, j)),
                  pl.BlockSpec((1, tn), lambda j: (0, j))],
        out_specs=pl.BlockSpec((bsz, tn), lambda j: (0, j)),
        compiler_params=_cparams(("arbitrary",)),
        name="adaln_mod",
    )(c, w_ada, b_ada.reshape(1, n))


def _rope_table_kernel(pos_ref, invf_ref, cos_ref, sin_ref):
    ang = pos_ref[...] * invf_ref[...]
    cos_ref[...] = jnp.cos(ang)
    sin_ref[...] = jnp.sin(ang)


def _rope_tables(positions):
    bsz, seq = positions.shape
    per_row = LANES // HALF
    rows = bsz * seq // per_row
    pos = jnp.broadcast_to(positions.astype(jnp.float32).reshape(rows, per_row, 1),
                           (rows, per_row, HALF)).reshape(rows, LANES)
    inv_freq = ROPE_THETA ** (-jnp.arange(HALF, dtype=jnp.float32) / HALF)
    invf = jnp.broadcast_to(inv_freq[None, :], (per_row, HALF)).reshape(1, LANES)
    tr = seq // per_row
    cos, sin = pl.pallas_call(
        _rope_table_kernel,
        out_shape=[jax.ShapeDtypeStruct((rows, LANES), jnp.float32)] * 2,
        grid=(rows // tr,),
        in_specs=[pl.BlockSpec((tr, LANES), lambda i: (i, 0)),
                  pl.BlockSpec((1, LANES), lambda i: (0, 0))],
        out_specs=[pl.BlockSpec((tr, LANES), lambda i: (i, 0))] * 2,
        compiler_params=_cparams(("arbitrary",)),
        name="rope_table",
    )(pos, invf)
    return cos.reshape(bsz, seq, HALF), sin.reshape(bsz, seq, HALF)


N_QK_SLABS = 13
N_VT = 3


def _qkv_kernel(x_ref, mod_ref, w_ref, wvt_ref, cos_ref, sin_ref,
                qa_ref, kc_ref, ks_ref, kw_ref, qb_ref, kb_ref, vc_ref,
                vst_ref, vwt_ref, vbt_ref, *, ts, n_sub):
    shift = mod_ref[0, 0:1, :]
    scale = mod_ref[0, 1:2, :]

    def project(r):
        u = (_layer_norm(x_ref[0, r * ts:(r + 1) * ts, :]) * (1.0 + scale) + shift).astype(jnp.bfloat16)
        return _dot(u, w_ref[...]), _dot_nt(wvt_ref[...], u)

    def finish(r, acc, vt):
        rows = slice(r * ts, (r + 1) * ts)
        c32 = cos_ref[0, rows, :]
        s32 = sin_ref[0, rows, :]
        cos = jnp.concatenate([c32, c32, c32, c32], axis=1)
        sin = jnp.concatenate([-s32, -s32, s32, s32], axis=1)

        def rope(slab):
            t = acc[:, slab * LANES:(slab + 1) * LANES]
            return t * cos + pltpu.roll(t, LANES // 2, 1) * sin

        for j in range(4):
            qa_ref[0, rows, j * LANES:(j + 1) * LANES] = (rope(j) * LOG2E).astype(jnp.bfloat16)
            qb_ref[0, rows, j * LANES:(j + 1) * LANES] = (rope(7 + j) * LOG2E).astype(jnp.bfloat16)
        for slab, ref in ((4, kc_ref), (5, ks_ref), (6, kw_ref), (11, kb_ref)):
            ref[0, rows, :] = rope(slab).astype(jnp.bfloat16)
        vc_ref[0, rows, :] = acc[:, 12 * LANES:13 * LANES].astype(jnp.bfloat16)
        ones = jnp.ones((G_ROWS - HEAD_DIM, ts), jnp.bfloat16)
        for j, ref in enumerate((vst_ref, vwt_ref, vbt_ref)):
            for g in range(N_KV):
                lo = j * KV_WIDTH + g * HEAD_DIM
                ref[0, g * G_ROWS:g * G_ROWS + HEAD_DIM, rows] = vt[lo:lo + HEAD_DIM].astype(jnp.bfloat16)
                ref[0, g * G_ROWS + HEAD_DIM:(g + 1) * G_ROWS, rows] = ones

    nxt = project(0)
    for r in range(n_sub):
        cur = nxt
        if r + 1 < n_sub:
            nxt = project(r + 1)
        finish(r, *cur)


def _qkv_proj(x, mod3, w_qk, w_vt, cos_t, sin_t, ts=256, n_sub=4):
    bsz, seq, _ = x.shape
    tm = ts * n_sub
    q_shape = jax.ShapeDtypeStruct((bsz, seq, Q_WIDTH), jnp.bfloat16)
    k_shape = jax.ShapeDtypeStruct((bsz, seq, KV_WIDTH), jnp.bfloat16)
    vt_shape = jax.ShapeDtypeStruct((bsz, V_ROWS, seq), jnp.bfloat16)
    q_spec = pl.BlockSpec((1, tm, Q_WIDTH), lambda b, i: (b, i, 0))
    k_spec = pl.BlockSpec((1, tm, KV_WIDTH), lambda b, i: (b, i, 0))
    t_spec = pl.BlockSpec((1, tm, HALF), lambda b, i: (b, i, 0))
    vt_spec = pl.BlockSpec((1, V_ROWS, tm), lambda b, i: (b, 0, i))
    return pl.pallas_call(
        functools.partial(_qkv_kernel, ts=ts, n_sub=n_sub),
        out_shape=[q_shape, k_shape, k_shape, k_shape, q_shape, k_shape, k_shape,
                   vt_shape, vt_shape, vt_shape],
        grid=(bsz, seq // tm),
        in_specs=[pl.BlockSpec((1, tm, D_MODEL), lambda b, i: (b, i, 0)),
                  pl.BlockSpec((1, 3, D_MODEL), lambda b, i: (b, 0, 0)),
                  pl.BlockSpec((D_MODEL, N_QK_SLABS * LANES), lambda b, i: (0, 0)),
                  pl.BlockSpec((N_VT * KV_WIDTH, D_MODEL), lambda b, i: (0, 0)),
                  t_spec, t_spec],
        out_specs=[q_spec, k_spec, k_spec, k_spec, q_spec, k_spec, k_spec,
                   vt_spec, vt_spec, vt_spec],
        compiler_params=_cparams(("arbitrary", "arbitrary")),
        name="qkv_proj",
    )(x, mod3, w_qk, w_vt, cos_t, sin_t)


def _compress_kernel(xk_ref, xv_ref, w1k_ref, w1v_ref, posk_ref, posv_ref, w2k_ref, w2vt_ref,
                     kc_ref, vct_ref):
    def hidden(x_ref, w1_ref, pos_ref):
        n_chunks = x_ref.shape[1]
        hw = w1_ref.shape[1] // 2
        p = _dot(x_ref[0], w1_ref[...])
        pt = _dot(pos_ref[...], w1_ref[...])
        pos_term = pt[0:1, 0:hw] + pt[1:2, hw:]
        first = p[:, 0:hw]
        second = pltpu.roll(p[:, hw:], n_chunks - 1, 0)
        hid = first + second + pos_term
        return (hid * _sigmoid(hid)).astype(jnp.bfloat16)

    kc_ref[0] = _dot(hidden(xk_ref, w1k_ref, posk_ref), w2k_ref[...]).astype(jnp.bfloat16)
    vct_ref[0] = _dot_nt(w2vt_ref[...], hidden(xv_ref, w1v_ref, posv_ref)).astype(jnp.bfloat16)


def _compress(xk, xv, w1k, w1v, posk, posv, w2k, w2vt):
    bsz, n_chunks, cw = xk.shape
    hw2 = w1k.shape[1]
    out = jax.ShapeDtypeStruct((bsz, n_chunks, LANES), jnp.bfloat16)
    x_spec = pl.BlockSpec((1, n_chunks, cw), lambda b: (b, 0, 0))
    w1_spec = pl.BlockSpec((cw, hw2), lambda b: (0, 0))
    pos_spec = pl.BlockSpec((8, cw), lambda b: (0, 0))
    o_spec = pl.BlockSpec((1, n_chunks, LANES), lambda b: (b, 0, 0))
    return pl.pallas_call(
        _compress_kernel,
        out_shape=[out, out],
        grid=(bsz,),
        in_specs=[x_spec, x_spec, w1_spec, w1_spec, pos_spec, pos_spec,
                  pl.BlockSpec(w2k.shape, lambda b: (0, 0)), pl.BlockSpec(w2vt.shape, lambda b: (0, 0))],
        out_specs=[o_spec, o_spec],
        compiler_params=_cparams(("arbitrary",)),
        name="compress",
    )(xk, xv, w1k, w1v, posk, posv, w2k, w2vt)


def _stack_heads(q):
    lane = lax.broadcasted_iota(jnp.int32, (1, LANES), 1)
    is_a = (lane & (HEAD_DIM - 1)) < HALF
    zero = jnp.zeros((), q.dtype)
    tiles = [q[:, j * LANES:(j + 1) * LANES] for j in range(GROUP)]
    parts = [jnp.where(is_a, t, zero) for t in tiles] + [jnp.where(is_a, zero, t) for t in tiles]
    return jnp.concatenate(parts, axis=0)


def _store_heads(o_ref, rows, o_g, inv_l, tq):
    for j in range(GROUP):
        cols = slice(j * tq, (j + 1) * tq)
        halves = [o_g[g][0:HEAD_DIM, cols] for g in range(N_KV)]
        if inv_l is not None:
            halves = [h * inv_l[g][:, cols] for g, h in enumerate(halves)]
        tile_t = jnp.concatenate(halves, axis=0)
        o_ref[0, rows, j * LANES:(j + 1) * LANES] = tile_t.T.astype(jnp.bfloat16)


def _group_cols(g, tq):
    return slice(g * GROUP * tq, (g + 1) * GROUP * tq)


def _cmp_kernel(q_ref, kc_ref, vct_ref, ovt_ref, o_ref, bias_ref, *, tq, n_cmp, n_blk):
    qi = pl.program_id(1)
    nq = tq // LANES
    qs = _stack_heads(q_ref[0])
    c_row = lax.broadcasted_iota(jnp.int32, (LANES, tq), 0)
    t_col = qi * tq + lax.broadcasted_iota(jnp.int32, (LANES, tq), 1)
    vis = (c_row * CMP_STRIDE + (CMP_LEN - 1) <= t_col) & (c_row < n_cmp)
    any_vis = (t_col[0:1, :] >= CMP_LEN - 1).astype(jnp.float32)
    s_t = _dot_nt(kc_ref[0], qs)
    p_parts = []
    for cb in range(N_HEADS * nq):
        qq = slice((cb % nq) * LANES, (cb % nq + 1) * LANES)
        s = jnp.where(vis[:, qq], s_t[:, cb * LANES:(cb + 1) * LANES], NEG_INF)
        e = jnp.exp2(s - jnp.max(s, axis=0, keepdims=True))
        p_parts.append(e * (any_vis[:, qq] / jnp.sum(e, axis=0, keepdims=True)))
    p_t = jnp.concatenate(p_parts, axis=1)
    o_g = [_dot(vct_ref[0, g * HEAD_DIM:(g + 1) * HEAD_DIM, :], p_t[:, _group_cols(g, tq)].astype(jnp.bfloat16))
           for g in range(N_KV)]
    _store_heads(o_ref, slice(None), o_g, None, tq)

    j_blk = lax.broadcasted_iota(jnp.int32, (n_blk, tq), 0)
    t_q = qi * tq + lax.broadcasted_iota(jnp.int32, (n_blk, tq), 1)
    cur = jnp.right_shift(t_q, SEL_BLOCK.bit_length() - 1)
    forced = (j_blk == 0) | (j_blk == cur) | (j_blk == cur - 1)
    causal = j_blk <= cur
    n_free = SEL_TOPN - 3
    take_all = cur < SEL_TOPN
    candidate = causal & jnp.logical_not(forced)
    rows = []
    for g in range(N_KV):
        pg = p_t[:, g * GROUP * tq:(g * GROUP + 1) * tq]
        for h in range(1, GROUP):
            pg = pg + p_t[:, (g * GROUP + h) * tq:(g * GROUP + h + 1) * tq]
        hi = pg.astype(jnp.bfloat16)
        lo = (pg - hi.astype(jnp.float32)).astype(jnp.bfloat16)
        imp_t = _dot(ovt_ref[...], hi) + _dot(ovt_ref[...], lo)
        work = jnp.where(candidate, imp_t[0:n_blk, :], NEG_INF)
        picked = jnp.zeros((n_blk, tq), jnp.int32)
        for _ in range(n_free):
            top = jnp.max(work, axis=0, keepdims=True)
            first = jnp.min(jnp.where(work == top, j_blk, n_blk), axis=0, keepdims=True)
            hit = j_blk == first
            picked = jnp.where(hit, 1, picked)
            work = jnp.where(hit, NEG_INF, work)
        keep = causal & (take_all | forced | ((picked > 0) & candidate))
        rows.append(jnp.where(keep, 0.0, NEG_INF))
    rows.append(jnp.zeros((LANES - N_KV * n_blk, tq), jnp.float32))
    bias_t = jnp.concatenate(rows, axis=0)
    bias_ref[0] = bias_t.T.astype(jnp.bfloat16)


def _cmp_attn(qa, kc, vct, ovt, n_cmp, tq=256):
    bsz, seq, _ = qa.shape
    n_blk = seq // SEL_BLOCK
    kern = functools.partial(_cmp_kernel, tq=tq, n_cmp=n_cmp, n_blk=n_blk)
    kv_spec = pl.BlockSpec((1, LANES, LANES), lambda b, i: (b, 0, 0))
    return pl.pallas_call(
        kern,
        out_shape=[jax.ShapeDtypeStruct((bsz, seq, Q_WIDTH), jnp.bfloat16),
                   jax.ShapeDtypeStruct((bsz, seq, LANES), jnp.bfloat16)],
        grid=(bsz, seq // tq),
        in_specs=[pl.BlockSpec((1, tq, Q_WIDTH), lambda b, i: (b, i, 0)),
                  kv_spec, kv_spec,
                  pl.BlockSpec((LANES, LANES), lambda b, i: (0, 0))],
        out_specs=[pl.BlockSpec((1, tq, Q_WIDTH), lambda b, i: (b, i, 0)),
                   pl.BlockSpec((1, tq, LANES), lambda b, i: (b, i, 0))],
        compiler_params=_cparams(("arbitrary", "arbitrary")),
        name="cmp_attn",
    )(qa, kc, vct, ovt)


def _sel_kernel(q_ref, bias_ref, k_ref, vt_ref, blk_ref, o_ref, kaug_ref, qaug_ref, s0_ref, s1_ref, acc_ref, m_ref,
                *, tq):
    qi = pl.program_id(1)
    tk = tq
    nq = tq // LANES
    per_g = GROUP * nq

    @pl.when(qi == 0)
    def _():
        kaug_ref[:, 0:LANES] = k_ref[0]
        kaug_ref[:, LANES:2 * LANES] = blk_ref[...]

    lane = lax.broadcasted_iota(jnp.int32, (1, LANES), 1)
    n_blk = blk_ref.shape[0] // SEL_BLOCK
    bt = bias_ref[0]
    zero = jnp.zeros((), bt.dtype)
    b0 = jnp.where(lane < n_blk, bt, zero)
    b1 = jnp.where((lane >= n_blk) & (lane < 2 * n_blk), bt, zero)
    qaug_ref[:, 0:LANES] = _stack_heads(q_ref[0])
    qaug_ref[:, LANES:2 * LANES] = jnp.concatenate([b0] * GROUP + [b1] * GROUP, axis=0)

    acc_ref[...] = jnp.zeros_like(acc_ref)
    m_ref[...] = jnp.full_like(m_ref, NEG_INF)
    row = lax.broadcasted_iota(jnp.int32, (tk, tq), 0)
    col = lax.broadcasted_iota(jnp.int32, (tk, tq), 1)
    causal_bias = jnp.where(row <= col, 0.0, NEG_INF)

    def scores(c, s_ref):
        k0 = pl.multiple_of(c * tk, tk)
        s_ref[...] = _dot_nt(kaug_ref[pl.ds(k0, tk), :], qaug_ref[...])

    def accumulate(c, s_ref, diagonal):
        k0 = pl.multiple_of(c * tk, tk)
        for g in range(N_KV):
            ps, alphas = [], []
            for i in range(per_g):
                cb = g * per_g + i
                cols = slice(cb * LANES, (cb + 1) * LANES)
                s = s_ref[:, cols]
                if diagonal:
                    s = s + causal_bias[:, (cb % nq) * LANES:(cb % nq + 1) * LANES]
                m_old = m_ref[0:1, cols]
                m_new = jnp.maximum(m_old, jnp.max(s, axis=0, keepdims=True))
                m_ref[0:1, cols] = m_new
                alphas.append(jnp.exp2(m_old - m_new))
                ps.append(jnp.exp2(s - m_new).astype(jnp.bfloat16))
            vt_g = vt_ref[0, g * G_ROWS:(g + 1) * G_ROWS, pl.ds(k0, tk)]
            pv = _dot(vt_g, jnp.concatenate(ps, axis=1))
            acc_ref[g] = jnp.concatenate(alphas, axis=1) * acc_ref[g] + pv

    scores(0, s0_ref)

    def body(i, carry):
        c = 2 * i
        scores(c + 1, s1_ref)
        accumulate(c, s0_ref, False)
        scores(c + 2, s0_ref)
        accumulate(c + 1, s1_ref, False)
        return carry

    lax.fori_loop(0, qi // 2, body, 0)

    @pl.when(qi % 2 == 1)
    def _():
        scores(qi, s1_ref)
        accumulate(qi - 1, s0_ref, False)
        accumulate(qi, s1_ref, True)

    @pl.when(qi % 2 == 0)
    def _():
        accumulate(qi, s0_ref, True)

    o_g = [acc_ref[g] for g in range(N_KV)]
    inv_l = [1.0 / o[HEAD_DIM:HEAD_DIM + 1, :] for o in o_g]
    _store_heads(o_ref, slice(None), o_g, inv_l, tq)


def _sel_attn(qa, bias, ks, vst, blk_onehot, tq=256):
    bsz, seq, _ = qa.shape
    kern = functools.partial(_sel_kernel, tq=tq)
    return pl.pallas_call(
        kern,
        out_shape=jax.ShapeDtypeStruct((bsz, seq, Q_WIDTH), jnp.bfloat16),
        grid=(bsz, seq // tq),
        in_specs=[pl.BlockSpec((1, tq, Q_WIDTH), lambda b, i: (b, i, 0)),
                  pl.BlockSpec((1, tq, LANES), lambda b, i: (b, i, 0)),
                  pl.BlockSpec((1, seq, LANES), lambda b, i: (b, 0, 0)),
                  pl.BlockSpec((1, V_ROWS, seq), lambda b, i: (b, 0, 0)),
                  pl.BlockSpec((seq, LANES), lambda b, i: (0, 0))],
        out_specs=pl.BlockSpec((1, tq, Q_WIDTH), lambda b, i: (b, i, 0)),
        scratch_shapes=[pltpu.VMEM((seq, 2 * LANES), jnp.bfloat16),
                        pltpu.VMEM((N_HEADS * tq, 2 * LANES), jnp.bfloat16),
                        pltpu.VMEM((tq, N_HEADS * tq), jnp.float32),
                        pltpu.VMEM((tq, N_HEADS * tq), jnp.float32),
                        pltpu.VMEM((N_KV, G_ROWS, GROUP * tq), jnp.float32),
                        pltpu.VMEM((8, N_HEADS * tq), jnp.float32)],
        compiler_params=_cparams(("arbitrary", "arbitrary")),
        name="sel_attn",
    )(qa, bias, ks, vst, blk_onehot)


def _band_kernel(*refs, tq, n_sub, span, window, use_sink):
    if use_sink:
        q_ref, k_ref, vt_ref, sink_ref, o_ref = refs
    else:
        q_ref, k_ref, vt_ref, o_ref = refs
    step = pl.program_id(1)
    seq = k_ref.shape[1]
    nq = tq // LANES
    row = lax.broadcasted_iota(jnp.int32, (span, tq), 0)
    col = lax.broadcasted_iota(jnp.int32, (span, tq), 1)
    diff = col - row
    def scores(sub):
        q0 = (step * n_sub + sub) * tq
        start = pl.multiple_of(jnp.clip(q0 + tq - span, 0, seq - span), LANES)
        rel = diff + (q0 - start)
        bias = jnp.where((rel >= 0) & (rel < window), 0.0, NEG_INF)
        qs = _stack_heads(q_ref[0, sub * tq:(sub + 1) * tq, :])
        return start, bias, _dot_nt(k_ref[0, pl.ds(start, span), :], qs)

    def finish(sub, start, bias, s_t):
        ps, extra = [], []
        for cb in range(N_HEADS * nq):
            s = s_t[:, cb * LANES:(cb + 1) * LANES] + bias[:, (cb % nq) * LANES:(cb % nq + 1) * LANES]
            m = jnp.max(s, axis=0, keepdims=True)
            if use_sink:
                sink = sink_ref[cb // nq] * LOG2E
                m = jnp.maximum(m, sink)
                extra.append(jnp.exp2(sink - m))
            ps.append(jnp.exp2(s - m).astype(jnp.bfloat16))
        o_g, inv_l = [], []
        for g in range(N_KV):
            per_g = GROUP * nq
            vt_g = vt_ref[0, g * G_ROWS:(g + 1) * G_ROWS, pl.ds(start, span)]
            o = _dot(vt_g, jnp.concatenate(ps[g * per_g:(g + 1) * per_g], axis=1))
            l = o[HEAD_DIM:HEAD_DIM + 1, :]
            if use_sink:
                l = l + jnp.concatenate(extra[g * per_g:(g + 1) * per_g], axis=1)
            o_g.append(o)
            inv_l.append(1.0 / l)
        _store_heads(o_ref, slice(sub * tq, (sub + 1) * tq), o_g, inv_l, tq)

    nxt = scores(0)
    for sub in range(n_sub):
        cur = nxt
        if sub + 1 < n_sub:
            nxt = scores(sub + 1)
        finish(sub, *cur)


def _band_attn(q, k, vt, window, sinks=None, tq=128, n_sub=4):
    bsz, seq, _ = q.shape
    span = (-(-(window - 1) // tq)) * tq + tq
    use_sink = sinks is not None
    kern = functools.partial(_band_kernel, tq=tq, n_sub=n_sub, span=span, window=window, use_sink=use_sink)
    ts = tq * n_sub
    in_specs = [pl.BlockSpec((1, ts, Q_WIDTH), lambda b, i: (b, i, 0)),
                pl.BlockSpec((1, seq, LANES), lambda b, i: (b, 0, 0)),
                pl.BlockSpec((1, V_ROWS, seq), lambda b, i: (b, 0, 0))]
    args = [q, k, vt]
    if use_sink:
        in_specs.append(pl.BlockSpec(memory_space=pltpu.SMEM))
        args.append(sinks)
    return pl.pallas_call(
        kern,
        out_shape=jax.ShapeDtypeStruct((bsz, seq, Q_WIDTH), jnp.bfloat16),
        grid=(bsz, seq // ts),
        in_specs=in_specs,
        out_specs=pl.BlockSpec((1, ts, Q_WIDTH), lambda b, i: (b, i, 0)),
        compiler_params=_cparams(("arbitrary", "arbitrary")),
        name="band_attn_sink" if use_sink else "band_attn",
    )(*args)


GATE_WIDTH = LANES + 2 * Q_WIDTH + 2 * D_MODEL


def _merge_kernel(x_ref, mod_ref, wg_ref, gexp_ref, ocmp_ref, osel_ref, owin_ref, ob_ref,
                  wua_ref, wub_ref, wo_ref, lng_ref, lnb_ref, o_ref, *, ts, n_sub):
    shift = mod_ref[0, 0:1, :]
    scale = mod_ref[0, 1:2, :]
    gate = mod_ref[0, 2:3, :]

    def project(r):
        u = (_layer_norm(x_ref[0, r * ts:(r + 1) * ts, :]) * (1.0 + scale) + shift).astype(jnp.bfloat16)
        return _dot(u, wg_ref[...])

    def finish(r, g_all):
        rows = slice(r * ts, (r + 1) * ts)
        sig = _sigmoid(g_all[:, 0:LANES])
        hi = sig.astype(jnp.bfloat16)
        lo = (sig - hi.astype(jnp.float32)).astype(jnp.bfloat16)
        gx = _dot(jnp.concatenate([hi, lo], axis=1), gexp_ref[...])
        o_a = (gx[:, 0:Q_WIDTH] * ocmp_ref[0, rows, :].astype(jnp.float32)
               + gx[:, Q_WIDTH:2 * Q_WIDTH] * osel_ref[0, rows, :].astype(jnp.float32)
               + gx[:, 2 * Q_WIDTH:3 * Q_WIDTH] * owin_ref[0, rows, :].astype(jnp.float32))
        c0 = LANES
        z_a = g_all[:, c0:c0 + Q_WIDTH]
        y_a = (o_a * (z_a * _sigmoid(z_a))).astype(jnp.bfloat16)
        c0 += Q_WIDTH
        z_b = g_all[:, c0:c0 + Q_WIDTH]
        y_b = (ob_ref[0, rows, :].astype(jnp.float32) * (z_b * _sigmoid(z_b))).astype(jnp.bfloat16)
        c0 += Q_WIDTH
        merged = _sigmoid(g_all[:, c0:c0 + D_MODEL]) * _dot(y_a, wua_ref[...])
        c0 += D_MODEL
        merged = merged + _sigmoid(g_all[:, c0:c0 + D_MODEL]) * _dot(y_b, wub_ref[...])
        out = _dot(merged.astype(jnp.bfloat16), wo_ref[...])
        y = DEEPNORM_ALPHA * x_ref[0, rows, :] + gate * out
        o_ref[0, rows, :] = _layer_norm(y) * lng_ref[...] + lnb_ref[...]

    nxt = project(0)
    for r in range(n_sub):
        cur = nxt
        if r + 1 < n_sub:
            nxt = project(r + 1)
        finish(r, cur)


def _merge_out(x, mod3, wg, gexp, o_cmp, o_sel, o_win, o_b, wua, wub, wo, ln_g, ln_b, ts=256, n_sub=4):
    bsz, seq, _ = x.shape
    tm = ts * n_sub
    tok = lambda w: pl.BlockSpec((1, tm, w), lambda b, i: (b, i, 0))
    full = lambda a: pl.BlockSpec(a.shape, lambda b, i: (0,) * a.ndim)
    return pl.pallas_call(
        functools.partial(_merge_kernel, ts=ts, n_sub=n_sub),
        out_shape=jax.ShapeDtypeStruct(x.shape, jnp.float32),
        grid=(bsz, seq // tm),
        in_specs=[tok(D_MODEL), pl.BlockSpec((1, 3, D_MODEL), lambda b, i: (b, 0, 0)),
                  full(wg), full(gexp), tok(Q_WIDTH), tok(Q_WIDTH), tok(Q_WIDTH), tok(Q_WIDTH),
                  full(wua), full(wub), full(wo), full(ln_g), full(ln_b)],
        out_specs=tok(D_MODEL),
        compiler_params=_cparams(("arbitrary", "arbitrary")),
        name="merge_out",
    )(x, mod3, wg, gexp, o_cmp, o_sel, o_win, o_b, wua, wub, wo, ln_g, ln_b)


def _pair_q_cols(w):
    k = w.shape[0]
    return w.reshape(k, N_KV, GROUP, 2, HALF).transpose(0, 2, 3, 1, 4).reshape(k, Q_WIDTH)


def _pair_k_cols(w):
    k = w.shape[0]
    return w.reshape(k, N_KV, 2, HALF).transpose(0, 2, 1, 3).reshape(k, KV_WIDTH)


def _pair_o_cols(w):
    k = w.shape[0]
    return w.reshape(k, N_KV, GROUP, HEAD_DIM).transpose(0, 2, 1, 3).reshape(k, Q_WIDTH)


def _pair_o_rows(w):
    n = w.shape[1]
    return w.reshape(N_KV, GROUP, HEAD_DIM, n).transpose(1, 0, 2, 3).reshape(Q_WIDTH, n)


def _prep_qkv_weight(w_in):
    q_scale = HEAD_DIM ** -0.5
    sec = lambda off, width: w_in[:, off:off + width]
    cols = [_pair_q_cols(sec(OFF_QA, Q_WIDTH)) * q_scale,
            _pair_k_cols(sec(OFF_KVC, KV_WIDTH)), _pair_k_cols(sec(OFF_KVS, KV_WIDTH)),
            _pair_k_cols(sec(OFF_KVW, KV_WIDTH)),
            _pair_q_cols(sec(OFF_QB, Q_WIDTH)) * q_scale,
            _pair_k_cols(sec(OFF_KVB, KV_WIDTH)),
            sec(OFF_KVC + KV_WIDTH, KV_WIDTH)]
    v_cols = [sec(OFF_KVS + KV_WIDTH, KV_WIDTH), sec(OFF_KVW + KV_WIDTH, KV_WIDTH),
              sec(OFF_KVB + KV_WIDTH, KV_WIDTH)]
    return (jnp.concatenate(cols, axis=1).astype(jnp.bfloat16),
            jnp.concatenate(v_cols, axis=1).T.astype(jnp.bfloat16))


def _prep_gate_weight(w_in):
    pad = jnp.zeros((D_MODEL, LANES - 3 * N_HEADS), w_in.dtype)
    cols = [w_in[:, OFF_GN:OFF_GN + 3 * N_HEADS], pad,
            _pair_o_cols(w_in[:, OFF_ZA:OFF_ZA + Q_WIDTH]), _pair_o_cols(w_in[:, OFF_ZB:OFF_ZB + Q_WIDTH]),
            w_in[:, OFF_GM:OFF_GM + 2 * D_MODEL]]
    return jnp.concatenate(cols, axis=1).astype(jnp.bfloat16)


def _gate_expand_matrix():
    e = np.zeros((LANES, 3 * Q_WIDTH), np.float32)
    for br in range(3):
        for c in range(Q_WIDTH):
            h = int(O_PERM[c]) // HEAD_DIM
            e[h * 3 + br, br * Q_WIDTH + c] = 1.0
    return jnp.asarray(np.concatenate([e, e], axis=0), jnp.bfloat16)


def _prep_compress(pos, w1, w2, paired):
    half_len = CMP_LEN // 2
    eye = jnp.eye(N_KV, dtype=w1.dtype)
    if paired:
        w1x = jnp.einsum("lhif,gk->lhgikf", w1.reshape(CMP_LEN, 2, HALF, CMP_HIDDEN), eye)
        w2e = jnp.einsum("fhi,kg->kfhgi", w2.reshape(CMP_HIDDEN, 2, HALF), eye)
        pos_l = jnp.broadcast_to(pos.reshape(CMP_LEN, 2, 1, HALF), (CMP_LEN, 2, N_KV, HALF))
    else:
        w1x = jnp.einsum("ldf,gk->lgdkf", w1.reshape(CMP_LEN, HEAD_DIM, CMP_HIDDEN), eye)
        w2e = jnp.einsum("fd,kg->kfgd", w2, eye)
        pos_l = jnp.broadcast_to(pos.reshape(CMP_LEN, 1, HEAD_DIM), (CMP_LEN, N_KV, HEAD_DIM))
    w1x = w1x.reshape(CMP_LEN, LANES, N_KV * CMP_HIDDEN)
    first = w1x[:half_len].reshape(half_len * LANES, N_KV * CMP_HIDDEN)
    second = w1x[half_len:].reshape(half_len * LANES, N_KV * CMP_HIDDEN)
    w1e = jnp.concatenate([first, second], axis=1).astype(jnp.bfloat16)
    pos_l = pos_l.reshape(CMP_LEN, LANES)
    pos_rows = jnp.concatenate([pos_l[:half_len].reshape(1, -1), pos_l[half_len:].reshape(1, -1),
                                jnp.zeros((6, half_len * LANES), pos.dtype)], axis=0).astype(jnp.bfloat16)
    return w1e, pos_rows, w2e.reshape(N_KV * CMP_HIDDEN, LANES).astype(jnp.bfloat16)


def _overlap_t(n_cmp, n_blk):
    c0 = np.arange(n_cmp) * CMP_STRIDE
    j0 = np.arange(n_blk) * SEL_BLOCK
    overlap = (c0[:, None] < j0[None, :] + SEL_BLOCK) & (c0[:, None] + CMP_LEN > j0[None, :])
    ovt = np.zeros((LANES, LANES), np.float32)
    ovt[:n_blk, :n_cmp] = overlap.T
    return jnp.asarray(ovt, jnp.bfloat16)


def _block_onehot(seq):
    n_blk = seq // SEL_BLOCK
    e = np.zeros((seq, LANES), np.float32)
    pos = np.arange(seq)
    e[pos, pos // SEL_BLOCK] = 1.0
    e[pos, n_blk + pos // SEL_BLOCK] = 1.0
    return jnp.asarray(e, jnp.bfloat16)


def kernel(x, c, positions, w_ada, b_ada, w_in, cmp_pos_k, cmp_w1_k, cmp_w2_k,
           cmp_pos_v, cmp_w1_v, cmp_w2_v, sinks, w_up_a, w_up_b, w_out, ln_g, ln_b):
    bsz, seq, _ = x.shape
    n_cmp = (seq - CMP_LEN) // CMP_STRIDE + 1
    n_blk = seq // SEL_BLOCK
    assert seq // CMP_STRIDE == LANES and 2 * n_blk <= LANES
    cos_t, sin_t = _rope_tables(positions)
    for l in range(DEPTH):
        mod3 = _adaln_mod(c, w_ada[l], b_ada[l]).reshape(bsz, 3, D_MODEL)
        w_qk, w_vt = _prep_qkv_weight(w_in[l])
        qa, k_c, k_s, k_w, qb, k_b, v_c, vt_s, vt_w, vt_b = _qkv_proj(x, mod3, w_qk, w_vt, cos_t, sin_t)

        w1k, posk, w2k = _prep_compress(cmp_pos_k[l], cmp_w1_k[l], cmp_w2_k[l], paired=True)
        w1v, posv, w2v = _prep_compress(cmp_pos_v[l], cmp_w1_v[l], cmp_w2_v[l], paired=False)
        chunk = lambda t: t.reshape(bsz, seq // CMP_STRIDE, CMP_STRIDE * KV_WIDTH)
        kc, vct = _compress(chunk(k_c), chunk(v_c), w1k, w1v, posk, posv, w2k, w2v.T)

        o_cmp, sel_bias = _cmp_attn(qa, kc, vct, _overlap_t(n_cmp, n_blk), n_cmp)
        o_sel = _sel_attn(qa, sel_bias, k_s, vt_s, _block_onehot(seq))
        o_win = _band_attn(qa, k_w, vt_w, NSA_WINDOW)
        o_b = _band_attn(qb, k_b, vt_b, SWA_WINDOW, sinks[l].astype(jnp.float32))

        x = _merge_out(x, mod3, _prep_gate_weight(w_in[l]), _gate_expand_matrix(),
                       o_cmp, o_sel, o_win, o_b,
                       _pair_o_rows(w_up_a[l]).astype(jnp.bfloat16), _pair_o_rows(w_up_b[l]).astype(jnp.bfloat16),
                       w_out[l].astype(jnp.bfloat16), ln_g[l].reshape(1, D_MODEL), ln_b[l].reshape(1, D_MODEL))
    return x
```

```python
import functools

import numpy as np
import jax
import jax.numpy as jnp
from jax import lax
from jax.experimental import pallas as pl
from jax.experimental.pallas import tpu as pltpu

D_MODEL = 1024
HEAD_DIM = 64
HALF = HEAD_DIM // 2
N_HEADS = 8
N_KV = 2
GROUP = N_HEADS // N_KV
Q_WIDTH = N_HEADS * HEAD_DIM
KV_WIDTH = N_KV * HEAD_DIM
CMP_LEN = 32
CMP_STRIDE = 16
CMP_HIDDEN = 256
SEL_BLOCK = 64
SEL_TOPN = 8
NSA_WINDOW = 512
SWA_WINDOW = 128
ROPE_THETA = 10000.0
LN_EPS = 1e-5
NEG_INF = -1e30
FORCE_SCORE = 1e9
DEPTH = 1
DEEPNORM_ALPHA = (2 * DEPTH) ** 0.25
LANES = 128
SUBLANES = 8
BF16_SUBLANES = 16
ROPE_PACK = LANES // HALF
QKV_ROWS = 256
LOG2E = 1.4426950408889634
G_ROWS = HEAD_DIM + BF16_SUBLANES
V_ROWS = N_KV * G_ROWS
VMEM_LIMIT = 56 * 1024 * 1024

_SPLITS = [Q_WIDTH, 2 * KV_WIDTH, 2 * KV_WIDTH, 2 * KV_WIDTH, 3 * N_HEADS, Q_WIDTH,
           Q_WIDTH, 2 * KV_WIDTH, Q_WIDTH, 2 * D_MODEL]
_OFFS = [int(o) for o in np.cumsum([0] + _SPLITS)]
(OFF_QA, OFF_KVC, OFF_KVS, OFF_KVW, OFF_GN, OFF_ZA, OFF_QB, OFF_KVB, OFF_ZB, OFF_GM) = _OFFS[:10]


def _q_pair_perm():
    cols = []
    for j in range(GROUP):
        for half in range(2):
            for grp in range(N_KV):
                h = j + GROUP * grp
                cols += [h * HEAD_DIM + half * HALF + i for i in range(HALF)]
    return np.asarray(cols, np.int32)


def _k_pair_perm():
    cols = []
    for half in range(2):
        for grp in range(N_KV):
            cols += [grp * HEAD_DIM + half * HALF + i for i in range(HALF)]
    return np.asarray(cols, np.int32)


def _o_pair_perm():
    cols = []
    for j in range(GROUP):
        for grp in range(N_KV):
            h = j + GROUP * grp
            cols += [h * HEAD_DIM + d for d in range(HEAD_DIM)]
    return np.asarray(cols, np.int32)


Q_PERM = _q_pair_perm()
K_PERM = _k_pair_perm()
O_PERM = _o_pair_perm()


def _cparams(sem):
    return pltpu.CompilerParams(dimension_semantics=sem, vmem_limit_bytes=VMEM_LIMIT)


def _dot(a, b):
    return jnp.dot(a, b, preferred_element_type=jnp.float32)


def _dot_nt(a, b):
    return lax.dot_general(a, b, (((1,), (1,)), ((), ())), preferred_element_type=jnp.float32)


def _layer_norm(x):
    mu = jnp.mean(x, axis=-1, keepdims=True)
    xc = x - mu
    var = jnp.mean(xc * xc, axis=-1, keepdims=True)
    return xc * lax.rsqrt(var + LN_EPS)


def _sigmoid(x):
    return 1.0 / (1.0 + jnp.exp(-x))


def _adaln_kernel(c_ref, w_ref, b_ref, o_ref):
    c = c_ref[...]
    a = c * _sigmoid(c)
    o_ref[...] = jnp.dot(a, w_ref[...], preferred_element_type=jnp.float32,
                         precision=lax.Precision.HIGHEST) + b_ref[...]


def _adaln_mod(c, w_ada, b_ada):
    bsz = c.shape[0]
    n = w_ada.shape[1]
    tn = 512
    return pl.pallas_call(
        _adaln_kernel,
        out_shape=jax.ShapeDtypeStruct((bsz, n), jnp.float32),
        grid=(n // tn,),
        in_specs=[pl.BlockSpec((bsz, D_MODEL), lambda j: (0, 0)),
                  pl.BlockSpec((D_MODEL, tn), lambda j: (0, j)),
                  pl.BlockSpec((1, tn), lambda j: (0, j))],
        out_specs=pl.BlockSpec((bsz, tn), lambda j: (0, j)),
        compiler_params=_cparams(("arbitrary",)),
        name="adaln_mod",
    )(c, w_ada, b_ada.reshape(1, n))


def _rope_table_kernel(pos_ref, invf_ref, cos_ref, sin_ref):
    ang = pos_ref[...] * invf_ref[...]
    cos_ref[...] = jnp.cos(ang)
    sin_ref[...] = jnp.sin(ang)


def _rope_tables(positions, ts):
    bsz, seq = positions.shape
    sub = ts // ROPE_PACK
    rows = bsz * seq // ROPE_PACK
    pos = positions.astype(jnp.float32).reshape(bsz, seq // ts, ROPE_PACK, sub).transpose(0, 1, 3, 2)
    pos = jnp.broadcast_to(pos[..., None], pos.shape + (HALF,)).reshape(rows, LANES)
    inv_freq = ROPE_THETA ** (-jnp.arange(HALF, dtype=jnp.float32) / HALF)
    invf = jnp.broadcast_to(inv_freq[None, :], (ROPE_PACK, HALF)).reshape(1, LANES)
    tr = seq // ROPE_PACK
    cos, sin = pl.pallas_call(
        _rope_table_kernel,
        out_shape=[jax.ShapeDtypeStruct((rows, LANES), jnp.float32)] * 2,
        grid=(rows // tr,),
        in_specs=[pl.BlockSpec((tr, LANES), lambda i: (i, 0)),
                  pl.BlockSpec((1, LANES), lambda i: (0, 0))],
        out_specs=[pl.BlockSpec((tr, LANES), lambda i: (i, 0))] * 2,
        compiler_params=_cparams(("arbitrary",)),
        name="rope_table",
    )(pos, invf)
    return cos.reshape(bsz, tr, LANES), sin.reshape(bsz, tr, LANES)


N_QK_SLABS = 13
N_VT = 3


def _qkv_kernel(x_ref, mod_ref, w_ref, wvt_ref, cos_ref, sin_ref,
                qa_ref, kc_ref, ks_ref, kw_ref, qb_ref, kb_ref, vc_ref,
                vst_ref, vwt_ref, vbt_ref, kstage_ref, vstage_ref, *, ts, n_sub):
    shift = mod_ref[0, 0:1, :]
    scale = mod_ref[0, 1:2, :]

    def project(r):
        u = (_layer_norm(x_ref[0, r * ts:(r + 1) * ts, :]) * (1.0 + scale) + shift).astype(jnp.bfloat16)
        return _dot(u, w_ref[...]), _dot_nt(wvt_ref[...], u)

    def finish(r, acc, vt):
        rows = slice(r * ts, (r + 1) * ts)
        sub = ts // ROPE_PACK
        for k in range(ROPE_PACK):
            c32 = cos_ref[0, r * sub:(r + 1) * sub, k * HALF:(k + 1) * HALF]
            s32 = sin_ref[0, r * sub:(r + 1) * sub, k * HALF:(k + 1) * HALF]
            cos = jnp.concatenate([c32, c32, c32, c32], axis=1)
            sin = jnp.concatenate([-s32, -s32, s32, s32], axis=1)
            tok = slice(r * ts + k * sub, r * ts + (k + 1) * sub)

            def rope(slab):
                t = acc[k * sub:(k + 1) * sub, slab * LANES:(slab + 1) * LANES]
                return t * cos + pltpu.roll(t, LANES // 2, 1) * sin

            for j in range(4):
                qa_ref[0, tok, j * LANES:(j + 1) * LANES] = (rope(j) * LOG2E).astype(jnp.bfloat16)
                qb_ref[0, tok, j * LANES:(j + 1) * LANES] = (rope(7 + j) * LOG2E).astype(jnp.bfloat16)
            for slab, ref in ((5, ks_ref), (6, kw_ref), (11, kb_ref)):
                ref[0, tok, :] = rope(slab).astype(jnp.bfloat16)
            kstage_ref[k * sub:(k + 1) * sub, :] = rope(4)
        vstage_ref[...] = acc[:, 12 * LANES:13 * LANES]
        n_ch = ts // CMP_STRIDE
        for l in range(CMP_STRIDE):
            for stage, ref in ((kstage_ref, kc_ref), (vstage_ref, vc_ref)):
                ref[0, r * n_ch:(r + 1) * n_ch, l * LANES:(l + 1) * LANES] = (
                    stage[pl.ds(l, n_ch, stride=CMP_STRIDE), :].astype(jnp.bfloat16))
        ones = jnp.ones((G_ROWS - HEAD_DIM, ts), jnp.bfloat16)
        for j, ref in enumerate((vst_ref, vwt_ref, vbt_ref)):
            for g in range(N_KV):
                lo = j * KV_WIDTH + g * HEAD_DIM
                ref[0, g * G_ROWS:g * G_ROWS + HEAD_DIM, rows] = vt[lo:lo + HEAD_DIM].astype(jnp.bfloat16)
                ref[0, g * G_ROWS + HEAD_DIM:(g + 1) * G_ROWS, rows] = ones

    nxt = project(0)
    for r in range(n_sub):
        cur = nxt
        if r + 1 < n_sub:
            nxt = project(r + 1)
        finish(r, *cur)


def _qkv_proj(x, mod3, w_qk, w_vt, cos_t, sin_t, ts=QKV_ROWS, n_sub=4):
    bsz, seq, _ = x.shape
    tm = ts * n_sub
    chunk_w = CMP_STRIDE * KV_WIDTH
    q_shape = jax.ShapeDtypeStruct((bsz, seq, Q_WIDTH), jnp.bfloat16)
    k_shape = jax.ShapeDtypeStruct((bsz, seq, KV_WIDTH), jnp.bfloat16)
    c_shape = jax.ShapeDtypeStruct((bsz, seq // CMP_STRIDE, chunk_w), jnp.bfloat16)
    vt_shape = jax.ShapeDtypeStruct((bsz, V_ROWS, seq), jnp.bfloat16)
    q_spec = pl.BlockSpec((1, tm, Q_WIDTH), lambda b, i: (b, i, 0))
    k_spec = pl.BlockSpec((1, tm, KV_WIDTH), lambda b, i: (b, i, 0))
    c_spec = pl.BlockSpec((1, tm // CMP_STRIDE, chunk_w), lambda b, i: (b, i, 0))
    t_spec = pl.BlockSpec((1, tm // ROPE_PACK, LANES), lambda b, i: (b, i, 0))
    vt_spec = pl.BlockSpec((1, V_ROWS, tm), lambda b, i: (b, 0, i))
    return pl.pallas_call(
        functools.partial(_qkv_kernel, ts=ts, n_sub=n_sub),
        out_shape=[q_shape, c_shape, k_shape, k_shape, q_shape, k_shape, c_shape,
                   vt_shape, vt_shape, vt_shape],
        grid=(bsz, seq // tm),
        in_specs=[pl.BlockSpec((1, tm, D_MODEL), lambda b, i: (b, i, 0)),
                  pl.BlockSpec((1, 3, D_MODEL), lambda b, i: (b, 0, 0)),
                  pl.BlockSpec((D_MODEL, N_QK_SLABS * LANES), lambda b, i: (0, 0)),
                  pl.BlockSpec((N_VT * KV_WIDTH, D_MODEL), lambda b, i: (0, 0)),
                  t_spec, t_spec],
        out_specs=[q_spec, c_spec, k_spec, k_spec, q_spec, k_spec, c_spec,
                   vt_spec, vt_spec, vt_spec],
        scratch_shapes=[pltpu.VMEM((ts, KV_WIDTH), jnp.float32), pltpu.VMEM((ts, KV_WIDTH), jnp.float32)],
        compiler_params=_cparams(("arbitrary", "arbitrary")),
        name="qkv_proj",
    )(x, mod3, w_qk, w_vt, cos_t, sin_t)


def _compress_kernel(xk_ref, xv_ref, w1k_ref, w1v_ref, posk_ref, posv_ref, w2k_ref, w2vt_ref,
                     kc_ref, vct_ref):
    bb, n_chunks, cw = xk_ref.shape

    def hidden(x_ref, w1_ref, pos_ref):
        hw = w1_ref.shape[1] // 2
        p = _dot(x_ref[...].reshape(bb * n_chunks, cw), w1_ref[...])
        pt = _dot(pos_ref[...], w1_ref[...])
        pos_term = pt[0:1, 0:hw] + pt[1:2, hw:]
        first = p[:, 0:hw]
        second = pltpu.roll(p[:, hw:], bb * n_chunks - 1, 0)
        hid = first + second + pos_term
        return (hid * _sigmoid(hid)).astype(jnp.bfloat16)

    kc = _dot(hidden(xk_ref, w1k_ref, posk_ref), w2k_ref[...]).astype(jnp.bfloat16)
    vct = _dot_nt(w2vt_ref[...], hidden(xv_ref, w1v_ref, posv_ref)).astype(jnp.bfloat16)
    for i in range(bb):
        kc_ref[i] = kc[i * n_chunks:(i + 1) * n_chunks]
        vct_ref[i] = vct[:, i * n_chunks:(i + 1) * n_chunks]


def _compress(xk, xv, w1k, w1v, posk, posv, w2k, w2vt, bb=4):
    bsz, n_chunks, cw = xk.shape
    hw2 = w1k.shape[1]
    out = jax.ShapeDtypeStruct((bsz, n_chunks, LANES), jnp.bfloat16)
    x_spec = pl.BlockSpec((bb, n_chunks, cw), lambda b: (b, 0, 0))
    w1_spec = pl.BlockSpec((cw, hw2), lambda b: (0, 0))
    pos_spec = pl.BlockSpec((8, cw), lambda b: (0, 0))
    o_spec = pl.BlockSpec((bb, n_chunks, LANES), lambda b: (b, 0, 0))
    return pl.pallas_call(
        _compress_kernel,
        out_shape=[out, out],
        grid=(bsz // bb,),
        in_specs=[x_spec, x_spec, w1_spec, w1_spec, pos_spec, pos_spec,
                  pl.BlockSpec(w2k.shape, lambda b: (0, 0)), pl.BlockSpec(w2vt.shape, lambda b: (0, 0))],
        out_specs=[o_spec, o_spec],
        compiler_params=_cparams(("arbitrary",)),
        name="compress",
    )(xk, xv, w1k, w1v, posk, posv, w2k, w2vt)


def _stack_heads(q):
    lane = lax.broadcasted_iota(jnp.int32, (1, LANES), 1)
    is_a = (lane & (HEAD_DIM - 1)) < HALF
    zero = jnp.zeros((), q.dtype)
    tiles = [q[:, j * LANES:(j + 1) * LANES] for j in range(GROUP)]
    parts = [jnp.where(is_a, t, zero) for t in tiles] + [jnp.where(is_a, zero, t) for t in tiles]
    return jnp.concatenate(parts, axis=0)


def _store_heads(o_ref, rows, o_g, inv_l, tq):
    for j in range(GROUP):
        cols = slice(j * tq, (j + 1) * tq)
        halves = [o_g[g][0:HEAD_DIM, cols] for g in range(N_KV)]
        if inv_l is not None:
            halves = [h * inv_l[g][:, cols] for g, h in enumerate(halves)]
        tile_t = jnp.concatenate(halves, axis=0)
        o_ref[0, rows, j * LANES:(j + 1) * LANES] = tile_t.T.astype(jnp.bfloat16)


def _group_cols(g, tq):
    return slice(g * GROUP * tq, (g + 1) * GROUP * tq)


def _cmp_kernel(q_ref, kc_ref, vct_ref, ovt_ref, o_ref, bias_ref, *, tq, n_cmp, n_blk):
    qi = pl.program_id(1)
    nq = tq // LANES
    qs = _stack_heads(q_ref[0])
    c_row = lax.broadcasted_iota(jnp.int32, (LANES, tq), 0)
    t_col = qi * tq + lax.broadcasted_iota(jnp.int32, (LANES, tq), 1)
    vis = (c_row * CMP_STRIDE + (CMP_LEN - 1) <= t_col) & (c_row < n_cmp)
    any_vis = (t_col[0:1, :] >= CMP_LEN - 1).astype(jnp.float32)
    s_t = _dot_nt(kc_ref[0], qs)
    p_parts = []
    for cb in range(N_HEADS * nq):
        qq = slice((cb % nq) * LANES, (cb % nq + 1) * LANES)
        s = jnp.where(vis[:, qq], s_t[:, cb * LANES:(cb + 1) * LANES], NEG_INF)
        e = jnp.exp2(s - jnp.max(s, axis=0, keepdims=True))
        p_parts.append(e * (any_vis[:, qq] / jnp.sum(e, axis=0, keepdims=True)))
    p_t = jnp.concatenate(p_parts, axis=1)
    o_g = [_dot(vct_ref[0, g * HEAD_DIM:(g + 1) * HEAD_DIM, :], p_t[:, _group_cols(g, tq)].astype(jnp.bfloat16))
           for g in range(N_KV)]
    _store_heads(o_ref, slice(None), o_g, None, tq)

    j_blk = lax.broadcasted_iota(jnp.int32, (n_blk, tq), 0)
    t_q = qi * tq + lax.broadcasted_iota(jnp.int32, (n_blk, tq), 1)
    cur = jnp.right_shift(t_q, SEL_BLOCK.bit_length() - 1)
    forced = (j_blk == 0) | (j_blk == cur) | (j_blk == cur - 1)
    causal = j_blk <= cur
    n_free = SEL_TOPN - 3
    take_all = cur < SEL_TOPN
    candidate = causal & jnp.logical_not(forced)
    rows = []
    for g in range(N_KV):
        pg = p_t[:, g * GROUP * tq:(g * GROUP + 1) * tq]
        for h in range(1, GROUP):
            pg = pg + p_t[:, (g * GROUP + h) * tq:(g * GROUP + h + 1) * tq]
        hi = pg.astype(jnp.bfloat16)
        lo = (pg - hi.astype(jnp.float32)).astype(jnp.bfloat16)
        imp_t = _dot(ovt_ref[...], hi) + _dot(ovt_ref[...], lo)
        work = jnp.where(candidate, imp_t[0:n_blk, :], NEG_INF)
        picked = jnp.zeros((n_blk, tq), jnp.int32)
        for _ in range(n_free):
            top = jnp.max(work, axis=0, keepdims=True)
            first = jnp.min(jnp.where(work == top, j_blk, n_blk), axis=0, keepdims=True)
            hit = j_blk == first
            picked = jnp.where(hit, 1, picked)
            work = jnp.where(hit, NEG_INF, work)
        keep = causal & (take_all | forced | ((picked > 0) & candidate))
        rows.append(jnp.where(keep, 0.0, NEG_INF))
    rows.append(jnp.zeros((LANES - N_KV * n_blk, tq), jnp.float32))
    bias_t = jnp.concatenate(rows, axis=0)
    bias_ref[0] = bias_t.T.astype(jnp.bfloat16)


def _cmp_attn(qa, kc, vct, ovt, n_cmp, tq=256):
    bsz, seq, _ = qa.shape
    n_blk = seq // SEL_BLOCK
    kern = functools.partial(_cmp_kernel, tq=tq, n_cmp=n_cmp, n_blk=n_blk)
    kv_spec = pl.BlockSpec((1, LANES, LANES), lambda b, i: (b, 0, 0))
    return pl.pallas_call(
        kern,
        out_shape=[jax.ShapeDtypeStruct((bsz, seq, Q_WIDTH), jnp.bfloat16),
                   jax.ShapeDtypeStruct((bsz, seq, LANES), jnp.bfloat16)],
        grid=(bsz, seq // tq),
        in_specs=[pl.BlockSpec((1, tq, Q_WIDTH), lambda b, i: (b, i, 0)),
                  kv_spec, kv_spec,
                  pl.BlockSpec((LANES, LANES), lambda b, i: (0, 0))],
        out_specs=[pl.BlockSpec((1, tq, Q_WIDTH), lambda b, i: (b, i, 0)),
                   pl.BlockSpec((1, tq, LANES), lambda b, i: (b, i, 0))],
        compiler_params=_cparams(("arbitrary", "arbitrary")),
        name="cmp_attn",
    )(qa, kc, vct, ovt)


def _sel_kernel(q_ref, bias_ref, k_ref, vt_ref, blk_ref, o_ref, kaug_ref, qaug_ref, s0_ref, s1_ref, acc_ref, m_ref,
                *, tq):
    qi = pl.program_id(1)
    tk = tq
    nq = tq // LANES
    per_g = GROUP * nq

    @pl.when(qi == 0)
    def _():
        kaug_ref[:, 0:LANES] = k_ref[0]
        kaug_ref[:, LANES:2 * LANES] = blk_ref[...]

    lane = lax.broadcasted_iota(jnp.int32, (1, LANES), 1)
    n_blk = blk_ref.shape[0] // SEL_BLOCK
    bt = bias_ref[0]
    zero = jnp.zeros((), bt.dtype)
    b0 = jnp.where(lane < n_blk, bt, zero)
    b1 = jnp.where((lane >= n_blk) & (lane < 2 * n_blk), bt, zero)
    qaug_ref[:, 0:LANES] = _stack_heads(q_ref[0])
    qaug_ref[:, LANES:2 * LANES] = jnp.concatenate([b0] * GROUP + [b1] * GROUP, axis=0)

    acc_ref[...] = jnp.zeros_like(acc_ref)
    m_ref[...] = jnp.full_like(m_ref, NEG_INF)
    row = lax.broadcasted_iota(jnp.int32, (tk, tq), 0)
    col = lax.broadcasted_iota(jnp.int32, (tk, tq), 1)
    causal_bias = jnp.where(row <= col, 0.0, NEG_INF)

    def scores(c, s_ref):
        k0 = pl.multiple_of(c * tk, tk)
        s_ref[...] = _dot_nt(kaug_ref[pl.ds(k0, tk), :], qaug_ref[...])

    def accumulate(c, s_ref, diagonal):
        k0 = pl.multiple_of(c * tk, tk)
        for g in range(N_KV):
            ps, alphas = [], []
            for i in range(per_g):
                cb = g * per_g + i
                cols = slice(cb * LANES, (cb + 1) * LANES)
                s = s_ref[:, cols]
                if diagonal:
                    s = s + causal_bias[:, (cb % nq) * LANES:(cb % nq + 1) * LANES]
                m_old = m_ref[0:1, cols]
                m_new = jnp.maximum(m_old, jnp.max(s, axis=0, keepdims=True))
                m_ref[0:1, cols] = m_new
                alphas.append(jnp.exp2(m_old - m_new))
                ps.append(jnp.exp2(s - m_new).astype(jnp.bfloat16))
            vt_g = vt_ref[0, g * G_ROWS:(g + 1) * G_ROWS, pl.ds(k0, tk)]
            pv = _dot(vt_g, jnp.concatenate(ps, axis=1))
            acc_ref[g] = jnp.concatenate(alphas, axis=1) * acc_ref[g] + pv

    scores(0, s0_ref)

    def body(i, carry):
        c = 2 * i
        scores(c + 1, s1_ref)
        accumulate(c, s0_ref, False)
        scores(c + 2, s0_ref)
        accumulate(c + 1, s1_ref, False)
        return carry

    lax.fori_loop(0, qi // 2, body, 0)

    @pl.when(qi % 2 == 1)
    def _():
        scores(qi, s1_ref)
        accumulate(qi - 1, s0_ref, False)
        accumulate(qi, s1_ref, True)

    @pl.when(qi % 2 == 0)
    def _():
        accumulate(qi, s0_ref, True)

    o_g = [acc_ref[g] for g in range(N_KV)]
    inv_l = [1.0 / o[HEAD_DIM:HEAD_DIM + 1, :] for o in o_g]
    _store_heads(o_ref, slice(None), o_g, inv_l, tq)


def _sel_attn(qa, bias, ks, vst, blk_onehot, tq=256):
    bsz, seq, _ = qa.shape
    kern = functools.partial(_sel_kernel, tq=tq)
    return pl.pallas_call(
        kern,
        out_shape=jax.ShapeDtypeStruct((bsz, seq, Q_WIDTH), jnp.bfloat16),
        grid=(bsz, seq // tq),
        in_specs=[pl.BlockSpec((1, tq, Q_WIDTH), lambda b, i: (b, i, 0)),
                  pl.BlockSpec((1, tq, LANES), lambda b, i: (b, i, 0)),
                  pl.BlockSpec((1, seq, LANES), lambda b, i: (b, 0, 0)),
                  pl.BlockSpec((1, V_ROWS, seq), lambda b, i: (b, 0, 0)),
                  pl.BlockSpec((seq, LANES), lambda b, i: (0, 0))],
        out_specs=pl.BlockSpec((1, tq, Q_WIDTH), lambda b, i: (b, i, 0)),
        scratch_shapes=[pltpu.VMEM((seq, 2 * LANES), jnp.bfloat16),
                        pltpu.VMEM((N_HEADS * tq, 2 * LANES), jnp.bfloat16),
                        pltpu.VMEM((tq, N_HEADS * tq), jnp.float32),
                        pltpu.VMEM((tq, N_HEADS * tq), jnp.float32),
                        pltpu.VMEM((N_KV, G_ROWS, GROUP * tq), jnp.float32),
                        pltpu.VMEM((8, N_HEADS * tq), jnp.float32)],
        compiler_params=_cparams(("arbitrary", "arbitrary")),
        name="sel_attn",
    )(qa, bias, ks, vst, blk_onehot)


def _band_kernel(*refs, tq, n_sub, span, window, use_sink):
    if use_sink:
        q_ref, k_ref, vt_ref, sink_ref, o_ref = refs
    else:
        q_ref, k_ref, vt_ref, o_ref = refs
    step = pl.program_id(1)
    seq = k_ref.shape[1]
    nq = tq // LANES
    row = lax.broadcasted_iota(jnp.int32, (span, tq), 0)
    col = lax.broadcasted_iota(jnp.int32, (span, tq), 1)
    diff = col - row

    def scores(sub):
        q0 = (step * n_sub + sub) * tq
        start = pl.multiple_of(jnp.clip(q0 + tq - span, 0, seq - span), LANES)
        rel = diff + (q0 - start)
        bias = jnp.where((rel >= 0) & (rel < window), 0.0, NEG_INF)
        qs = _stack_heads(q_ref[0, sub * tq:(sub + 1) * tq, :])
        return start, bias, _dot_nt(k_ref[0, pl.ds(start, span), :], qs)

    def finish(sub, start, bias, s_t):
        ps, extra = [], []
        for cb in range(N_HEADS * nq):
            s = s_t[:, cb * LANES:(cb + 1) * LANES] + bias[:, (cb % nq) * LANES:(cb % nq + 1) * LANES]
            m = jnp.max(s, axis=0, keepdims=True)
            if use_sink:
                sink = sink_ref[cb // nq] * LOG2E
                m = jnp.maximum(m, sink)
                extra.append(jnp.exp2(sink - m))
            ps.append(jnp.exp2(s - m).astype(jnp.bfloat16))
        o_g, inv_l = [], []
        for g in range(N_KV):
            per_g = GROUP * nq
            vt_g = vt_ref[0, g * G_ROWS:(g + 1) * G_ROWS, pl.ds(start, span)]
            o = _dot(vt_g, jnp.concatenate(ps[g * per_g:(g + 1) * per_g], axis=1))
            l = o[HEAD_DIM:HEAD_DIM + 1, :]
            if use_sink:
                l = l + jnp.concatenate(extra[g * per_g:(g + 1) * per_g], axis=1)
            o_g.append(o)
            inv_l.append(1.0 / l)
        _store_heads(o_ref, slice(sub * tq, (sub + 1) * tq), o_g, inv_l, tq)

    nxt = scores(0)
    for sub in range(n_sub):
        cur = nxt
        if sub + 1 < n_sub:
            nxt = scores(sub + 1)
        finish(sub, *cur)


def _band_attn(q, k, vt, window, sinks=None, tq=128, n_sub=8):
    bsz, seq, _ = q.shape
    span = (-(-(window - 1) // tq)) * tq + tq
    use_sink = sinks is not None
    kern = functools.partial(_band_kernel, tq=tq, n_sub=n_sub, span=span, window=window, use_sink=use_sink)
    ts = tq * n_sub
    in_specs = [pl.BlockSpec((1, ts, Q_WIDTH), lambda b, i: (b, i, 0)),
                pl.BlockSpec((1, seq, LANES), lambda b, i: (b, 0, 0)),
                pl.BlockSpec((1, V_ROWS, seq), lambda b, i: (b, 0, 0))]
    args = [q, k, vt]
    if use_sink:
        in_specs.append(pl.BlockSpec(memory_space=pltpu.SMEM))
        args.append(sinks)
    return pl.pallas_call(
        kern,
        out_shape=jax.ShapeDtypeStruct((bsz, seq, Q_WIDTH), jnp.bfloat16),
        grid=(bsz, seq // ts),
        in_specs=in_specs,
        out_specs=pl.BlockSpec((1, ts, Q_WIDTH), lambda b, i: (b, i, 0)),
        compiler_params=_cparams(("arbitrary", "arbitrary")),
        name="band_attn_sink" if use_sink else "band_attn",
    )(*args)


GATE_WIDTH = LANES + 2 * Q_WIDTH + 2 * D_MODEL


def _merge_kernel(x_ref, mod_ref, wg_ref, gexp_ref, ocmp_ref, osel_ref, owin_ref, ob_ref,
                  wua_ref, wub_ref, wo_ref, lng_ref, lnb_ref, o_ref, *, ts, n_sub):
    shift = mod_ref[0, 0:1, :]
    scale = mod_ref[0, 1:2, :]
    gate = mod_ref[0, 2:3, :]

    def project(r):
        u = (_layer_norm(x_ref[0, r * ts:(r + 1) * ts, :]) * (1.0 + scale) + shift).astype(jnp.bfloat16)
        return _dot(u, wg_ref[...])

    def finish(r, g_all):
        rows = slice(r * ts, (r + 1) * ts)
        sig = _sigmoid(g_all[:, 0:LANES])
        hi = sig.astype(jnp.bfloat16)
        lo = (sig - hi.astype(jnp.float32)).astype(jnp.bfloat16)
        gx = _dot(jnp.concatenate([hi, lo], axis=1), gexp_ref[...])
        o_a = (gx[:, 0:Q_WIDTH] * ocmp_ref[0, rows, :].astype(jnp.float32)
               + gx[:, Q_WIDTH:2 * Q_WIDTH] * osel_ref[0, rows, :].astype(jnp.float32)
               + gx[:, 2 * Q_WIDTH:3 * Q_WIDTH] * owin_ref[0, rows, :].astype(jnp.float32))
        c0 = LANES
        z_a = g_all[:, c0:c0 + Q_WIDTH]
        y_a = (o_a * (z_a * _sigmoid(z_a))).astype(jnp.bfloat16)
        c0 += Q_WIDTH
        z_b = g_all[:, c0:c0 + Q_WIDTH]
        y_b = (ob_ref[0, rows, :].astype(jnp.float32) * (z_b * _sigmoid(z_b))).astype(jnp.bfloat16)
        c0 += Q_WIDTH
        merged = _sigmoid(g_all[:, c0:c0 + D_MODEL]) * _dot(y_a, wua_ref[...])
        c0 += D_MODEL
        merged = merged + _sigmoid(g_all[:, c0:c0 + D_MODEL]) * _dot(y_b, wub_ref[...])
        out = _dot(merged.astype(jnp.bfloat16), wo_ref[...])
        y = DEEPNORM_ALPHA * x_ref[0, rows, :] + gate * out
        o_ref[0, rows, :] = _layer_norm(y) * lng_ref[...] + lnb_ref[...]

    nxt = project(0)
    for r in range(n_sub):
        cur = nxt
        if r + 1 < n_sub:
            nxt = project(r + 1)
        finish(r, cur)


def _merge_out(x, mod3, wg, gexp, o_cmp, o_sel, o_win, o_b, wua, wub, wo, ln_g, ln_b, ts=256, n_sub=4):
    bsz, seq, _ = x.shape
    tm = ts * n_sub
    tok = lambda w: pl.BlockSpec((1, tm, w), lambda b, i: (b, i, 0))
    full = lambda a: pl.BlockSpec(a.shape, lambda b, i: (0,) * a.ndim)
    return pl.pallas_call(
        functools.partial(_merge_kernel, ts=ts, n_sub=n_sub),
        out_shape=jax.ShapeDtypeStruct(x.shape, jnp.float32),
        grid=(bsz, seq // tm),
        in_specs=[tok(D_MODEL), pl.BlockSpec((1, 3, D_MODEL), lambda b, i: (b, 0, 0)),
                  full(wg), full(gexp), tok(Q_WIDTH), tok(Q_WIDTH), tok(Q_WIDTH), tok(Q_WIDTH),
                  full(wua), full(wub), full(wo), full(ln_g), full(ln_b)],
        out_specs=tok(D_MODEL),
        compiler_params=_cparams(("arbitrary", "arbitrary")),
        name="merge_out",
    )(x, mod3, wg, gexp, o_cmp, o_sel, o_win, o_b, wua, wub, wo, ln_g, ln_b)


def _pair_q_cols(w):
    k = w.shape[0]
    return w.reshape(k, N_KV, GROUP, 2, HALF).transpose(0, 2, 3, 1, 4).reshape(k, Q_WIDTH)


def _pair_k_cols(w):
    k = w.shape[0]
    return w.reshape(k, N_KV, 2, HALF).transpose(0, 2, 1, 3).reshape(k, KV_WIDTH)


def _pair_o_cols(w):
    k = w.shape[0]
    return w.reshape(k, N_KV, GROUP, HEAD_DIM).transpose(0, 2, 1, 3).reshape(k, Q_WIDTH)


def _pair_o_rows(w):
    n = w.shape[1]
    return w.reshape(N_KV, GROUP, HEAD_DIM, n).transpose(1, 0, 2, 3).reshape(Q_WIDTH, n)


def _prep_qkv_weight(w_in):
    q_scale = HEAD_DIM ** -0.5
    sec = lambda off, width: w_in[:, off:off + width]
    cols = [_pair_q_cols(sec(OFF_QA, Q_WIDTH)) * q_scale,
            _pair_k_cols(sec(OFF_KVC, KV_WIDTH)), _pair_k_cols(sec(OFF_KVS, KV_WIDTH)),
            _pair_k_cols(sec(OFF_KVW, KV_WIDTH)),
            _pair_q_cols(sec(OFF_QB, Q_WIDTH)) * q_scale,
            _pair_k_cols(sec(OFF_KVB, KV_WIDTH)),
            sec(OFF_KVC + KV_WIDTH, KV_WIDTH)]
    v_cols = [sec(OFF_KVS + KV_WIDTH, KV_WIDTH), sec(OFF_KVW + KV_WIDTH, KV_WIDTH),
              sec(OFF_KVB + KV_WIDTH, KV_WIDTH)]
    return (jnp.concatenate(cols, axis=1).astype(jnp.bfloat16),
            jnp.concatenate(v_cols, axis=1).T.astype(jnp.bfloat16))


def _prep_gate_weight(w_in):
    pad = jnp.zeros((D_MODEL, LANES - 3 * N_HEADS), w_in.dtype)
    cols = [w_in[:, OFF_GN:OFF_GN + 3 * N_HEADS], pad,
            _pair_o_cols(w_in[:, OFF_ZA:OFF_ZA + Q_WIDTH]), _pair_o_cols(w_in[:, OFF_ZB:OFF_ZB + Q_WIDTH]),
            w_in[:, OFF_GM:OFF_GM + 2 * D_MODEL]]
    return jnp.concatenate(cols, axis=1).astype(jnp.bfloat16)


def _gate_expand_matrix():
    e = np.zeros((LANES, 3 * Q_WIDTH), np.float32)
    for br in range(3):
        for c in range(Q_WIDTH):
            h = int(O_PERM[c]) // HEAD_DIM
            e[h * 3 + br, br * Q_WIDTH + c] = 1.0
    return jnp.asarray(np.concatenate([e, e], axis=0), jnp.bfloat16)


def _prep_compress(pos, w1, w2, paired):
    half_len = CMP_LEN // 2
    w = w1.reshape(CMP_LEN, HEAD_DIM, CMP_HIDDEN).astype(jnp.bfloat16)
    w2 = w2.astype(jnp.bfloat16)
    pieces = ([(g, slice(h * HALF, (h + 1) * HALF)) for h in range(2) for g in range(N_KV)] if paired
              else [(g, slice(0, HEAD_DIM)) for g in range(N_KV)])
    own = lambda g, blk: jnp.concatenate([blk if k == g else jnp.zeros_like(blk) for k in range(N_KV)], axis=-1)
    w1x = jnp.concatenate([own(g, w[:, d, :]) for g, d in pieces], axis=1)
    first = w1x[:half_len].reshape(half_len * LANES, N_KV * CMP_HIDDEN)
    second = w1x[half_len:].reshape(half_len * LANES, N_KV * CMP_HIDDEN)
    w1e = jnp.concatenate([first, second], axis=1)
    w2e = jnp.concatenate([jnp.concatenate([w2[:, d] if k == g else jnp.zeros_like(w2[:, d]) for g, d in pieces], axis=1)
                           for k in range(N_KV)], axis=0)
    pos_l = jnp.concatenate([pos[:, d] for _, d in pieces], axis=1)
    pos_rows = jnp.concatenate([pos_l[:half_len].reshape(1, -1), pos_l[half_len:].reshape(1, -1),
                                jnp.zeros((6, half_len * LANES), pos.dtype)], axis=0).astype(jnp.bfloat16)
    return w1e, pos_rows, w2e


def _overlap_t(n_cmp, n_blk):
    c0 = np.arange(n_cmp) * CMP_STRIDE
    j0 = np.arange(n_blk) * SEL_BLOCK
    overlap = (c0[:, None] < j0[None, :] + SEL_BLOCK) & (c0[:, None] + CMP_LEN > j0[None, :])
    ovt = np.zeros((LANES, LANES), np.float32)
    ovt[:n_blk, :n_cmp] = overlap.T
    return jnp.asarray(ovt, jnp.bfloat16)


def _block_onehot(seq):
    n_blk = seq // SEL_BLOCK
    e = np.zeros((seq, LANES), np.float32)
    pos = np.arange(seq)
    e[pos, pos // SEL_BLOCK] = 1.0
    e[pos, n_blk + pos // SEL_BLOCK] = 1.0
    return jnp.asarray(e, jnp.bfloat16)


def kernel(x, c, positions, w_ada, b_ada, w_in, cmp_pos_k, cmp_w1_k, cmp_w2_k,
           cmp_pos_v, cmp_w1_v, cmp_w2_v, sinks, w_up_a, w_up_b, w_out, ln_g, ln_b):
    bsz, seq, _ = x.shape
    n_cmp = (seq - CMP_LEN) // CMP_STRIDE + 1
    n_blk = seq // SEL_BLOCK
    assert seq // CMP_STRIDE == LANES and 2 * n_blk <= LANES
    cos_t, sin_t = _rope_tables(positions, QKV_ROWS)
    for l in range(DEPTH):
        mod3 = _adaln_mod(c, w_ada[l], b_ada[l]).reshape(bsz, 3, D_MODEL)
        w_qk, w_vt = _prep_qkv_weight(w_in[l])
        qa, k_c, k_s, k_w, qb, k_b, v_c, vt_s, vt_w, vt_b = _qkv_proj(x, mod3, w_qk, w_vt, cos_t, sin_t)

        w1k, posk, w2k = _prep_compress(cmp_pos_k[l], cmp_w1_k[l], cmp_w2_k[l], paired=True)
        w1v, posv, w2v = _prep_compress(cmp_pos_v[l], cmp_w1_v[l], cmp_w2_v[l], paired=False)
        kc, vct = _compress(k_c, v_c, w1k, w1v, posk, posv, w2k, w2v.T, bb=4 if bsz % 4 == 0 else 1)

        o_cmp, sel_bias = _cmp_attn(qa, kc, vct, _overlap_t(n_cmp, n_blk), n_cmp)
        o_sel = _sel_attn(qa, sel_bias, k_s, vt_s, _block_onehot(seq))
        o_win = _band_attn(qa, k_w, vt_w, NSA_WINDOW)
        o_b = _band_attn(qb, k_b, vt_b, SWA_WINDOW, sinks[l].astype(jnp.float32))

        x = _merge_out(x, mod3, _prep_gate_weight(w_in[l]), _gate_expand_matrix(),
                       o_cmp, o_sel, o_win, o_b,
                       _pair_o_rows(w_up_a[l]).astype(jnp.bfloat16), _pair_o_rows(w_up_b[l]).astype(jnp.bfloat16),
                       w_out[l].astype(jnp.bfloat16), ln_g[l].reshape(1, D_MODEL), ln_b[l].reshape(1, D_MODEL))
    return x
```

```python
import functools

import numpy as np
import jax
import jax.numpy as jnp
from jax import lax
from jax.experimental import pallas as pl
from jax.experimental.pallas import tpu as pltpu

D_MODEL = 1024
HEAD_DIM = 64
HALF = HEAD_DIM // 2
N_HEADS = 8
N_KV = 2
GROUP = N_HEADS // N_KV
Q_WIDTH = N_HEADS * HEAD_DIM
KV_WIDTH = N_KV * HEAD_DIM
CMP_LEN = 32
CMP_STRIDE = 16
CMP_HIDDEN = 256
SEL_BLOCK = 64
SEL_TOPN = 8
NSA_WINDOW = 512
SWA_WINDOW = 128
ROPE_THETA = 10000.0
LN_EPS = 1e-5
NEG_INF = -1e30
FORCE_SCORE = 1e9
DEPTH = 1
DEEPNORM_ALPHA = (2 * DEPTH) ** 0.25
LANES = 128
SUBLANES = 8
BF16_SUBLANES = 16
ROPE_PACK = LANES // HALF
QKV_ROWS = 256
LOG2E = 1.4426950408889634
G_ROWS = HEAD_DIM + BF16_SUBLANES
V_ROWS = N_KV * G_ROWS
VMEM_LIMIT = 56 * 1024 * 1024

_SPLITS = [Q_WIDTH, 2 * KV_WIDTH, 2 * KV_WIDTH, 2 * KV_WIDTH, 3 * N_HEADS, Q_WIDTH,
           Q_WIDTH, 2 * KV_WIDTH, Q_WIDTH, 2 * D_MODEL]
_OFFS = [int(o) for o in np.cumsum([0] + _SPLITS)]
(OFF_QA, OFF_KVC, OFF_KVS, OFF_KVW, OFF_GN, OFF_ZA, OFF_QB, OFF_KVB, OFF_ZB, OFF_GM) = _OFFS[:10]


def _q_pair_perm():
    cols = []
    for j in range(GROUP):
        for half in range(2):
            for grp in range(N_KV):
                h = j + GROUP * grp
                cols += [h * HEAD_DIM + half * HALF + i for i in range(HALF)]
    return np.asarray(cols, np.int32)


def _k_pair_perm():
    cols = []
    for half in range(2):
        for grp in range(N_KV):
            cols += [grp * HEAD_DIM + half * HALF + i for i in range(HALF)]
    return np.asarray(cols, np.int32)


def _o_pair_perm():
    cols = []
    for j in range(GROUP):
        for grp in range(N_KV):
            h = j + GROUP * grp
            cols += [h * HEAD_DIM + d for d in range(HEAD_DIM)]
    return np.asarray(cols, np.int32)


Q_PERM = _q_pair_perm()
K_PERM = _k_pair_perm()
O_PERM = _o_pair_perm()


def _cparams(sem):
    return pltpu.CompilerParams(dimension_semantics=sem, vmem_limit_bytes=VMEM_LIMIT)


def _dot(a, b):
    return jnp.dot(a, b, preferred_element_type=jnp.float32)


def _dot_nt(a, b):
    return lax.dot_general(a, b, (((1,), (1,)), ((), ())), preferred_element_type=jnp.float32)


def _layer_norm(x):
    mu = jnp.mean(x, axis=-1, keepdims=True)
    xc = x - mu
    var = jnp.mean(xc * xc, axis=-1, keepdims=True)
    return xc * lax.rsqrt(var + LN_EPS)


def _sigmoid(x):
    return 1.0 / (1.0 + jnp.exp(-x))


def _adaln_kernel(c_ref, w_ref, b_ref, o_ref):
    c = c_ref[...]
    a = c * _sigmoid(c)
    o_ref[...] = jnp.dot(a, w_ref[...], preferred_element_type=jnp.float32,
                         precision=lax.Precision.HIGHEST) + b_ref[...]


def _adaln_mod(c, w_ada, b_ada):
    bsz = c.shape[0]
    n = w_ada.shape[1]
    tn = 512
    return pl.pallas_call(
        _adaln_kernel,
        out_shape=jax.ShapeDtypeStruct((bsz, n), jnp.float32),
        grid=(n // tn,),
        in_specs=[pl.BlockSpec((bsz, D_MODEL), lambda j: (0, 0)),
                  pl.BlockSpec((D_MODEL, tn), lambda j: (0, j)),
                  pl.BlockSpec((1, tn), lambda j: (0, j))],
        out_specs=pl.BlockSpec((bsz, tn), lambda j: (0, j)),
        compiler_params=_cparams(("arbitrary",)),
        name="adaln_mod",
    )(c, w_ada, b_ada.reshape(1, n))


def _rope_table_kernel(pos_ref, invf_ref, cos_ref, sin_ref):
    ang = pos_ref[...] * invf_ref[...]
    cos_ref[...] = jnp.cos(ang)
    sin_ref[...] = jnp.sin(ang)


def _rope_tables(positions, ts):
    bsz, seq = positions.shape
    sub = ts // ROPE_PACK
    rows = bsz * seq // ROPE_PACK
    pos = positions.astype(jnp.float32).reshape(bsz, seq // ts, ROPE_PACK, sub).transpose(0, 1, 3, 2)
    pos = jnp.broadcast_to(pos[..., None], pos.shape + (HALF,)).reshape(rows, LANES)
    inv_freq = ROPE_THETA ** (-jnp.arange(HALF, dtype=jnp.float32) / HALF)
    invf = jnp.broadcast_to(inv_freq[None, :], (ROPE_PACK, HALF)).reshape(1, LANES)
    tr = seq // ROPE_PACK
    cos, sin = pl.pallas_call(
        _rope_table_kernel,
        out_shape=[jax.ShapeDtypeStruct((rows, LANES), jnp.float32)] * 2,
        grid=(rows // tr,),
        in_specs=[pl.BlockSpec((tr, LANES), lambda i: (i, 0)),
                  pl.BlockSpec((1, LANES), lambda i: (0, 0))],
        out_specs=[pl.BlockSpec((tr, LANES), lambda i: (i, 0))] * 2,
        compiler_params=_cparams(("arbitrary",)),
        name="rope_table",
    )(pos, invf)
    return cos.reshape(bsz, tr, LANES), sin.reshape(bsz, tr, LANES)


N_QK_SLABS = 13
N_VT = 3


def _qkv_kernel(x_ref, mod_ref, w_ref, wvt_ref, cos_ref, sin_ref,
                qa_ref, kc_ref, ks_ref, kw_ref, qb_ref, kb_ref, vc_ref,
                vst_ref, vwt_ref, vbt_ref, kstage_ref, vstage_ref, *, ts, n_sub):
    shift = mod_ref[0, 0:1, :]
    scale = mod_ref[0, 1:2, :]

    def project(r):
        u = (_layer_norm(x_ref[0, r * ts:(r + 1) * ts, :]) * (1.0 + scale) + shift).astype(jnp.bfloat16)
        return _dot(u, w_ref[...]), _dot_nt(wvt_ref[...], u)

    def finish(r, acc, vt):
        rows = slice(r * ts, (r + 1) * ts)
        sub = ts // ROPE_PACK
        for k in range(ROPE_PACK):
            c32 = cos_ref[0, r * sub:(r + 1) * sub, k * HALF:(k + 1) * HALF]
            s32 = sin_ref[0, r * sub:(r + 1) * sub, k * HALF:(k + 1) * HALF]
            cos = jnp.concatenate([c32, c32, c32, c32], axis=1)
            sin = jnp.concatenate([-s32, -s32, s32, s32], axis=1)
            tok = slice(r * ts + k * sub, r * ts + (k + 1) * sub)

            def rope(slab):
                t = acc[k * sub:(k + 1) * sub, slab * LANES:(slab + 1) * LANES]
                return t * cos + pltpu.roll(t, LANES // 2, 1) * sin

            for j in range(4):
                qa_ref[0, tok, j * LANES:(j + 1) * LANES] = (rope(j) * LOG2E).astype(jnp.bfloat16)
                qb_ref[0, tok, j * LANES:(j + 1) * LANES] = (rope(7 + j) * LOG2E).astype(jnp.bfloat16)
            for slab, ref in ((5, ks_ref), (6, kw_ref), (11, kb_ref)):
                ref[0, tok, :] = rope(slab).astype(jnp.bfloat16)
            kstage_ref[k * sub:(k + 1) * sub, :] = rope(4)
        vstage_ref[...] = acc[:, 12 * LANES:13 * LANES]
        n_ch = ts // CMP_STRIDE
        for l in range(CMP_STRIDE):
            for stage, ref in ((kstage_ref, kc_ref), (vstage_ref, vc_ref)):
                ref[0, r * n_ch:(r + 1) * n_ch, l * LANES:(l + 1) * LANES] = (
                    stage[pl.ds(l, n_ch, stride=CMP_STRIDE), :].astype(jnp.bfloat16))
        ones = jnp.ones((G_ROWS - HEAD_DIM, ts), jnp.bfloat16)
        for j, ref in enumerate((vst_ref, vwt_ref, vbt_ref)):
            for g in range(N_KV):
                lo = j * KV_WIDTH + g * HEAD_DIM
                ref[0, g * G_ROWS:g * G_ROWS + HEAD_DIM, rows] = vt[lo:lo + HEAD_DIM].astype(jnp.bfloat16)
                ref[0, g * G_ROWS + HEAD_DIM:(g + 1) * G_ROWS, rows] = ones

    nxt = project(0)
    for r in range(n_sub):
        cur = nxt
        if r + 1 < n_sub:
            nxt = project(r + 1)
        finish(r, *cur)


def _qkv_proj(x, mod3, w_qk, w_vt, cos_t, sin_t, ts=QKV_ROWS, n_sub=4):
    bsz, seq, _ = x.shape
    tm = ts * n_sub
    chunk_w = CMP_STRIDE * KV_WIDTH
    q_shape = jax.ShapeDtypeStruct((bsz, seq, Q_WIDTH), jnp.bfloat16)
    k_shape = jax.ShapeDtypeStruct((bsz, seq, KV_WIDTH), jnp.bfloat16)
    c_shape = jax.ShapeDtypeStruct((bsz, seq // CMP_STRIDE, chunk_w), jnp.bfloat16)
    vt_shape = jax.ShapeDtypeStruct((bsz, V_ROWS, seq), jnp.bfloat16)
    q_spec = pl.BlockSpec((1, tm, Q_WIDTH), lambda b, i: (b, i, 0))
    k_spec = pl.BlockSpec((1, tm, KV_WIDTH), lambda b, i: (b, i, 0))
    c_spec = pl.BlockSpec((1, tm // CMP_STRIDE, chunk_w), lambda b, i: (b, i, 0))
    t_spec = pl.BlockSpec((1, tm // ROPE_PACK, LANES), lambda b, i: (b, i, 0))
    vt_spec = pl.BlockSpec((1, V_ROWS, tm), lambda b, i: (b, 0, i))
    return pl.pallas_call(
        functools.partial(_qkv_kernel, ts=ts, n_sub=n_sub),
        out_shape=[q_shape, c_shape, k_shape, k_shape, q_shape, k_shape, c_shape,
                   vt_shape, vt_shape, vt_shape],
        grid=(bsz, seq // tm),
        in_specs=[pl.BlockSpec((1, tm, D_MODEL), lambda b, i: (b, i, 0)),
                  pl.BlockSpec((1, 3, D_MODEL), lambda b, i: (b, 0, 0)),
                  pl.BlockSpec((D_MODEL, N_QK_SLABS * LANES), lambda b, i: (0, 0)),
                  pl.BlockSpec((N_VT * KV_WIDTH, D_MODEL), lambda b, i: (0, 0)),
                  t_spec, t_spec],
        out_specs=[q_spec, c_spec, k_spec, k_spec, q_spec, k_spec, c_spec,
                   vt_spec, vt_spec, vt_spec],
        scratch_shapes=[pltpu.VMEM((ts, KV_WIDTH), jnp.float32), pltpu.VMEM((ts, KV_WIDTH), jnp.float32)],
        compiler_params=_cparams(("arbitrary", "arbitrary")),
        name="qkv_proj",
    )(x, mod3, w_qk, w_vt, cos_t, sin_t)


def _compress_kernel(xk_ref, xv_ref, w1k_ref, w1v_ref, posk_ref, posv_ref, w2k_ref, w2vt_ref,
                     kc_ref, vct_ref):
    bb, n_chunks, cw = xk_ref.shape

    def hidden(x_ref, w1_ref, pos_ref):
        hw = w1_ref.shape[1] // 2
        p = _dot(x_ref[...].reshape(bb * n_chunks, cw), w1_ref[...])
        pt = _dot(pos_ref[...], w1_ref[...])
        pos_term = pt[0:1, 0:hw] + pt[1:2, hw:]
        first = p[:, 0:hw]
        second = pltpu.roll(p[:, hw:], bb * n_chunks - 1, 0)
        hid = first + second + pos_term
        return (hid * _sigmoid(hid)).astype(jnp.bfloat16)

    kc = _dot(hidden(xk_ref, w1k_ref, posk_ref), w2k_ref[...]).astype(jnp.bfloat16)
    vct = _dot_nt(w2vt_ref[...], hidden(xv_ref, w1v_ref, posv_ref)).astype(jnp.bfloat16)
    for i in range(bb):
        kc_ref[i] = kc[i * n_chunks:(i + 1) * n_chunks]
        vct_ref[i] = vct[:, i * n_chunks:(i + 1) * n_chunks]


def _compress(xk, xv, w1k, w1v, posk, posv, w2k, w2vt, bb=4):
    bsz, n_chunks, cw = xk.shape
    hw2 = w1k.shape[1]
    out = jax.ShapeDtypeStruct((bsz, n_chunks, LANES), jnp.bfloat16)
    x_spec = pl.BlockSpec((bb, n_chunks, cw), lambda b: (b, 0, 0))
    w1_spec = pl.BlockSpec((cw, hw2), lambda b: (0, 0))
    pos_spec = pl.BlockSpec((8, cw), lambda b: (0, 0))
    o_spec = pl.BlockSpec((bb, n_chunks, LANES), lambda b: (b, 0, 0))
    return pl.pallas_call(
        _compress_kernel,
        out_shape=[out, out],
        grid=(bsz // bb,),
        in_specs=[x_spec, x_spec, w1_spec, w1_spec, pos_spec, pos_spec,
                  pl.BlockSpec(w2k.shape, lambda b: (0, 0)), pl.BlockSpec(w2vt.shape, lambda b: (0, 0))],
        out_specs=[o_spec, o_spec],
        compiler_params=_cparams(("arbitrary",)),
        name="compress",
    )(xk, xv, w1k, w1v, posk, posv, w2k, w2vt)


def _stack_heads(q):
    lane = lax.broadcasted_iota(jnp.int32, (1, LANES), 1)
    is_a = (lane & (HEAD_DIM - 1)) < HALF
    zero = jnp.zeros((), q.dtype)
    tiles = [q[:, j * LANES:(j + 1) * LANES] for j in range(GROUP)]
    parts = [jnp.where(is_a, t, zero) for t in tiles] + [jnp.where(is_a, zero, t) for t in tiles]
    return jnp.concatenate(parts, axis=0)


def _store_heads(o_ref, rows, o_g, inv_l, tq):
    for j in range(GROUP):
        cols = slice(j * tq, (j + 1) * tq)
        halves = [o_g[g][0:HEAD_DIM, cols] for g in range(N_KV)]
        if inv_l is not None:
            halves = [h * inv_l[g][:, cols] for g, h in enumerate(halves)]
        tile_t = jnp.concatenate(halves, axis=0)
        o_ref[0, rows, j * LANES:(j + 1) * LANES] = tile_t.T.astype(jnp.bfloat16)


def _group_cols(g, tq):
    return slice(g * GROUP * tq, (g + 1) * GROUP * tq)


def _cmp_kernel(q_ref, kc_ref, vct_ref, ovt_ref, o_ref, bias_ref, *, tq, n_cmp, n_blk):
    qi = pl.program_id(1)
    nq = tq // LANES
    qs = _stack_heads(q_ref[0])
    c_row = lax.broadcasted_iota(jnp.int32, (LANES, tq), 0)
    t_col = qi * tq + lax.broadcasted_iota(jnp.int32, (LANES, tq), 1)
    vis = (c_row * CMP_STRIDE + (CMP_LEN - 1) <= t_col) & (c_row < n_cmp)
    any_vis = (t_col[0:1, :] >= CMP_LEN - 1).astype(jnp.float32)
    s_t = _dot_nt(kc_ref[0], qs)
    p_parts = []
    for cb in range(N_HEADS * nq):
        qq = slice((cb % nq) * LANES, (cb % nq + 1) * LANES)
        s = jnp.where(vis[:, qq], s_t[:, cb * LANES:(cb + 1) * LANES], NEG_INF)
        e = jnp.exp2(s - jnp.max(s, axis=0, keepdims=True))
        p_parts.append(e * (any_vis[:, qq] / jnp.sum(e, axis=0, keepdims=True)))
    p_t = jnp.concatenate(p_parts, axis=1)
    o_g = [_dot(vct_ref[0, g * HEAD_DIM:(g + 1) * HEAD_DIM, :], p_t[:, _group_cols(g, tq)].astype(jnp.bfloat16))
           for g in range(N_KV)]
    _store_heads(o_ref, slice(None), o_g, None, tq)

    j_blk = lax.broadcasted_iota(jnp.int32, (n_blk, tq), 0)
    t_q = qi * tq + lax.broadcasted_iota(jnp.int32, (n_blk, tq), 1)
    cur = jnp.right_shift(t_q, SEL_BLOCK.bit_length() - 1)
    forced = (j_blk == 0) | (j_blk == cur) | (j_blk == cur - 1)
    causal = j_blk <= cur
    n_free = SEL_TOPN - 3
    take_all = cur < SEL_TOPN
    candidate = causal & jnp.logical_not(forced)
    rows = []
    for g in range(N_KV):
        pg = p_t[:, g * GROUP * tq:(g * GROUP + 1) * tq]
        for h in range(1, GROUP):
            pg = pg + p_t[:, (g * GROUP + h) * tq:(g * GROUP + h + 1) * tq]
        hi = pg.astype(jnp.bfloat16)
        lo = (pg - hi.astype(jnp.float32)).astype(jnp.bfloat16)
        imp_t = _dot(ovt_ref[...], hi) + _dot(ovt_ref[...], lo)
        work = jnp.where(candidate, imp_t[0:n_blk, :], NEG_INF)
        picked = jnp.zeros((n_blk, tq), jnp.int32)
        for _ in range(n_free):
            top = jnp.max(work, axis=0, keepdims=True)
            first = jnp.min(jnp.where(work == top, j_blk, n_blk), axis=0, keepdims=True)
            hit = j_blk == first
            picked = jnp.where(hit, 1, picked)
            work = jnp.where(hit, NEG_INF, work)
        keep = causal & (take_all | forced | ((picked > 0) & candidate))
        rows.append(jnp.where(keep, 0.0, NEG_INF))
    rows.append(jnp.zeros((LANES - N_KV * n_blk, tq), jnp.float32))
    bias_t = jnp.concatenate(rows, axis=0)
    bias_ref[0] = bias_t.T.astype(jnp.bfloat16)


def _cmp_attn(qa, kc, vct, ovt, n_cmp, tq=1024):
    bsz, seq, _ = qa.shape
    n_blk = seq // SEL_BLOCK
    kern = functools.partial(_cmp_kernel, tq=tq, n_cmp=n_cmp, n_blk=n_blk)
    kv_spec = pl.BlockSpec((1, LANES, LANES), lambda b, i: (b, 0, 0))
    return pl.pallas_call(
        kern,
        out_shape=[jax.ShapeDtypeStruct((bsz, seq, Q_WIDTH), jnp.bfloat16),
                   jax.ShapeDtypeStruct((bsz, seq, LANES), jnp.bfloat16)],
        grid=(bsz, seq // tq),
        in_specs=[pl.BlockSpec((1, tq, Q_WIDTH), lambda b, i: (b, i, 0)),
                  kv_spec, kv_spec,
                  pl.BlockSpec((LANES, LANES), lambda b, i: (0, 0))],
        out_specs=[pl.BlockSpec((1, tq, Q_WIDTH), lambda b, i: (b, i, 0)),
                   pl.BlockSpec((1, tq, LANES), lambda b, i: (b, i, 0))],
        compiler_params=_cparams(("arbitrary", "arbitrary")),
        name="cmp_attn",
    )(qa, kc, vct, ovt)


def _sel_kernel(q_ref, bias_ref, k_ref, vt_ref, blk_ref, o_ref, kaug_ref, qaug_ref, s0_ref, s1_ref, acc_ref, m_ref,
                *, tq):
    qi = pl.program_id(1)
    tk = tq
    nq = tq // LANES
    per_g = GROUP * nq

    @pl.when(qi == 0)
    def _():
        kaug_ref[:, 0:LANES] = k_ref[0]
        kaug_ref[:, LANES:2 * LANES] = blk_ref[...]

    lane = lax.broadcasted_iota(jnp.int32, (1, LANES), 1)
    n_blk = blk_ref.shape[0] // SEL_BLOCK
    bt = bias_ref[0]
    zero = jnp.zeros((), bt.dtype)
    b0 = jnp.where(lane < n_blk, bt, zero)
    b1 = jnp.where((lane >= n_blk) & (lane < 2 * n_blk), bt, zero)
    qaug_ref[:, 0:LANES] = _stack_heads(q_ref[0])
    qaug_ref[:, LANES:2 * LANES] = jnp.concatenate([b0] * GROUP + [b1] * GROUP, axis=0)

    acc_ref[...] = jnp.zeros_like(acc_ref)
    m_ref[...] = jnp.full_like(m_ref, NEG_INF)
    row = lax.broadcasted_iota(jnp.int32, (tk, tq), 0)
    col = lax.broadcasted_iota(jnp.int32, (tk, tq), 1)
    causal_bias = jnp.where(row <= col, 0.0, NEG_INF)

    def scores(c, s_ref):
        k0 = pl.multiple_of(c * tk, tk)
        s_ref[...] = _dot_nt(kaug_ref[pl.ds(k0, tk), :], qaug_ref[...])

    def accumulate(c, s_ref, diagonal):
        k0 = pl.multiple_of(c * tk, tk)
        for g in range(N_KV):
            ps, alphas = [], []
            for i in range(per_g):
                cb = g * per_g + i
                cols = slice(cb * LANES, (cb + 1) * LANES)
                s = s_ref[:, cols]
                if diagonal:
                    s = s + causal_bias[:, (cb % nq) * LANES:(cb % nq + 1) * LANES]
                m_old = m_ref[0:1, cols]
                m_new = jnp.maximum(m_old, jnp.max(s, axis=0, keepdims=True))
                m_ref[0:1, cols] = m_new
                alphas.append(jnp.exp2(m_old - m_new))
                ps.append(jnp.exp2(s - m_new).astype(jnp.bfloat16))
            vt_g = vt_ref[0, g * G_ROWS:(g + 1) * G_ROWS, pl.ds(k0, tk)]
            pv = _dot(vt_g, jnp.concatenate(ps, axis=1))
            acc_ref[g] = jnp.concatenate(alphas, axis=1) * acc_ref[g] + pv

    scores(0, s0_ref)

    def body(i, carry):
        c = 2 * i
        scores(c + 1, s1_ref)
        accumulate(c, s0_ref, False)
        scores(c + 2, s0_ref)
        accumulate(c + 1, s1_ref, False)
        return carry

    lax.fori_loop(0, qi // 2, body, 0)

    @pl.when(qi % 2 == 1)
    def _():
        scores(qi, s1_ref)
        accumulate(qi - 1, s0_ref, False)
        accumulate(qi, s1_ref, True)

    @pl.when(qi % 2 == 0)
    def _():
        accumulate(qi, s0_ref, True)

    o_g = [acc_ref[g] for g in range(N_KV)]
    inv_l = [1.0 / o[HEAD_DIM:HEAD_DIM + 1, :] for o in o_g]
    _store_heads(o_ref, slice(None), o_g, inv_l, tq)


def _sel_attn(qa, bias, ks, vst, blk_onehot, tq=256):
    bsz, seq, _ = qa.shape
    kern = functools.partial(_sel_kernel, tq=tq)
    return pl.pallas_call(
        kern,
        out_shape=jax.ShapeDtypeStruct((bsz, seq, Q_WIDTH), jnp.bfloat16),
        grid=(bsz, seq // tq),
        in_specs=[pl.BlockSpec((1, tq, Q_WIDTH), lambda b, i: (b, i, 0)),
                  pl.BlockSpec((1, tq, LANES), lambda b, i: (b, i, 0)),
                  pl.BlockSpec((1, seq, LANES), lambda b, i: (b, 0, 0)),
                  pl.BlockSpec((1, V_ROWS, seq), lambda b, i: (b, 0, 0)),
                  pl.BlockSpec((seq, LANES), lambda b, i: (0, 0))],
        out_specs=pl.BlockSpec((1, tq, Q_WIDTH), lambda b, i: (b, i, 0)),
        scratch_shapes=[pltpu.VMEM((seq, 2 * LANES), jnp.bfloat16),
                        pltpu.VMEM((N_HEADS * tq, 2 * LANES), jnp.bfloat16),
                        pltpu.VMEM((tq, N_HEADS * tq), jnp.float32),
                        pltpu.VMEM((tq, N_HEADS * tq), jnp.float32),
                        pltpu.VMEM((N_KV, G_ROWS, GROUP * tq), jnp.float32),
                        pltpu.VMEM((8, N_HEADS * tq), jnp.float32)],
        compiler_params=_cparams(("arbitrary", "arbitrary")),
        name="sel_attn",
    )(qa, bias, ks, vst, blk_onehot)


def _band_kernel(*refs, tq, n_sub, span, window, use_sink):
    if use_sink:
        q_ref, k_ref, vt_ref, sink_ref, o_ref = refs
    else:
        q_ref, k_ref, vt_ref, o_ref = refs
    step = pl.program_id(1)
    seq = k_ref.shape[1]
    nq = tq // LANES
    row = lax.broadcasted_iota(jnp.int32, (span, tq), 0)
    col = lax.broadcasted_iota(jnp.int32, (span, tq), 1)
    diff = col - row

    def scores(sub):
        q0 = (step * n_sub + sub) * tq
        start = pl.multiple_of(jnp.clip(q0 + tq - span, 0, seq - span), LANES)
        rel = diff + (q0 - start)
        bias = jnp.where((rel >= 0) & (rel < window), 0.0, NEG_INF)
        qs = _stack_heads(q_ref[0, sub * tq:(sub + 1) * tq, :])
        return start, bias, _dot_nt(k_ref[0, pl.ds(start, span), :], qs)

    def finish(sub, start, bias, s_t):
        ps, extra = [], []
        for cb in range(N_HEADS * nq):
            s = s_t[:, cb * LANES:(cb + 1) * LANES] + bias[:, (cb % nq) * LANES:(cb % nq + 1) * LANES]
            m = jnp.max(s, axis=0, keepdims=True)
            if use_sink:
                sink = sink_ref[cb // nq] * LOG2E
                m = jnp.maximum(m, sink)
                extra.append(jnp.exp2(sink - m))
            ps.append(jnp.exp2(s - m).astype(jnp.bfloat16))
        o_g, inv_l = [], []
        for g in range(N_KV):
            per_g = GROUP * nq
            vt_g = vt_ref[0, g * G_ROWS:(g + 1) * G_ROWS, pl.ds(start, span)]
            o = _dot(vt_g, jnp.concatenate(ps[g * per_g:(g + 1) * per_g], axis=1))
            l = o[HEAD_DIM:HEAD_DIM + 1, :]
            if use_sink:
                l = l + jnp.concatenate(extra[g * per_g:(g + 1) * per_g], axis=1)
            o_g.append(o)
            inv_l.append(1.0 / l)
        _store_heads(o_ref, slice(sub * tq, (sub + 1) * tq), o_g, inv_l, tq)

    nxt = scores(0)
    for sub in range(n_sub):
        cur = nxt
        if sub + 1 < n_sub:
            nxt = scores(sub + 1)
        finish(sub, *cur)


def _band_attn(q, k, vt, window, sinks=None, tq=128, n_sub=16):
    bsz, seq, _ = q.shape
    span = (-(-(window - 1) // tq)) * tq + tq
    use_sink = sinks is not None
    kern = functools.partial(_band_kernel, tq=tq, n_sub=n_sub, span=span, window=window, use_sink=use_sink)
    ts = tq * n_sub
    in_specs = [pl.BlockSpec((1, ts, Q_WIDTH), lambda b, i: (b, i, 0)),
                pl.BlockSpec((1, seq, LANES), lambda b, i: (b, 0, 0)),
                pl.BlockSpec((1, V_ROWS, seq), lambda b, i: (b, 0, 0))]
    args = [q, k, vt]
    if use_sink:
        in_specs.append(pl.BlockSpec(memory_space=pltpu.SMEM))
        args.append(sinks)
    return pl.pallas_call(
        kern,
        out_shape=jax.ShapeDtypeStruct((bsz, seq, Q_WIDTH), jnp.bfloat16),
        grid=(bsz, seq // ts),
        in_specs=in_specs,
        out_specs=pl.BlockSpec((1, ts, Q_WIDTH), lambda b, i: (b, i, 0)),
        compiler_params=_cparams(("arbitrary", "arbitrary")),
        name="band_attn_sink" if use_sink else "band_attn",
    )(*args)


GATE_WIDTH = LANES + 2 * Q_WIDTH + 2 * D_MODEL


def _merge_kernel(x_ref, mod_ref, wg_ref, gexp_ref, ocmp_ref, osel_ref, owin_ref, ob_ref,
                  wua_ref, wub_ref, wo_ref, lng_ref, lnb_ref, o_ref, *, ts, n_sub):
    shift = mod_ref[0, 0:1, :]
    scale = mod_ref[0, 1:2, :]
    gate = mod_ref[0, 2:3, :]

    def project(r):
        u = (_layer_norm(x_ref[0, r * ts:(r + 1) * ts, :]) * (1.0 + scale) + shift).astype(jnp.bfloat16)
        return _dot(u, wg_ref[...])

    def finish(r, g_all):
        rows = slice(r * ts, (r + 1) * ts)
        sig = _sigmoid(g_all[:, 0:LANES])
        hi = sig.astype(jnp.bfloat16)
        lo = (sig - hi.astype(jnp.float32)).astype(jnp.bfloat16)
        gx = _dot(jnp.concatenate([hi, lo], axis=1), gexp_ref[...])
        o_a = (gx[:, 0:Q_WIDTH] * ocmp_ref[0, rows, :].astype(jnp.float32)
               + gx[:, Q_WIDTH:2 * Q_WIDTH] * osel_ref[0, rows, :].astype(jnp.float32)
               + gx[:, 2 * Q_WIDTH:3 * Q_WIDTH] * owin_ref[0, rows, :].astype(jnp.float32))
        c0 = LANES
        z_a = g_all[:, c0:c0 + Q_WIDTH]
        y_a = (o_a * (z_a * _sigmoid(z_a))).astype(jnp.bfloat16)
        c0 += Q_WIDTH
        z_b = g_all[:, c0:c0 + Q_WIDTH]
        y_b = (ob_ref[0, rows, :].astype(jnp.float32) * (z_b * _sigmoid(z_b))).astype(jnp.bfloat16)
        c0 += Q_WIDTH
        merged = _sigmoid(g_all[:, c0:c0 + D_MODEL]) * _dot(y_a, wua_ref[...])
        c0 += D_MODEL
        merged = merged + _sigmoid(g_all[:, c0:c0 + D_MODEL]) * _dot(y_b, wub_ref[...])
        out = _dot(merged.astype(jnp.bfloat16), wo_ref[...])
        y = DEEPNORM_ALPHA * x_ref[0, rows, :] + gate * out
        o_ref[0, rows, :] = _layer_norm(y) * lng_ref[...] + lnb_ref[...]

    nxt = project(0)
    for r in range(n_sub):
        cur = nxt
        if r + 1 < n_sub:
            nxt = project(r + 1)
        finish(r, cur)


def _merge_out(x, mod3, wg, gexp, o_cmp, o_sel, o_win, o_b, wua, wub, wo, ln_g, ln_b, ts=256, n_sub=4):
    bsz, seq, _ = x.shape
    tm = ts * n_sub
    tok = lambda w: pl.BlockSpec((1, tm, w), lambda b, i: (b, i, 0))
    full = lambda a: pl.BlockSpec(a.shape, lambda b, i: (0,) * a.ndim)
    return pl.pallas_call(
        functools.partial(_merge_kernel, ts=ts, n_sub=n_sub),
        out_shape=jax.ShapeDtypeStruct(x.shape, jnp.float32),
        grid=(bsz, seq // tm),
        in_specs=[tok(D_MODEL), pl.BlockSpec((1, 3, D_MODEL), lambda b, i: (b, 0, 0)),
                  full(wg), full(gexp), tok(Q_WIDTH), tok(Q_WIDTH), tok(Q_WIDTH), tok(Q_WIDTH),
                  full(wua), full(wub), full(wo), full(ln_g), full(ln_b)],
        out_specs=tok(D_MODEL),
        compiler_params=_cparams(("arbitrary", "arbitrary")),
        name="merge_out",
    )(x, mod3, wg, gexp, o_cmp, o_sel, o_win, o_b, wua, wub, wo, ln_g, ln_b)


def _pair_q_cols(w):
    k = w.shape[0]
    return w.reshape(k, N_KV, GROUP, 2, HALF).transpose(0, 2, 3, 1, 4).reshape(k, Q_WIDTH)


def _pair_k_cols(w):
    k = w.shape[0]
    return w.reshape(k, N_KV, 2, HALF).transpose(0, 2, 1, 3).reshape(k, KV_WIDTH)


def _pair_o_cols(w):
    k = w.shape[0]
    return w.reshape(k, N_KV, GROUP, HEAD_DIM).transpose(0, 2, 1, 3).reshape(k, Q_WIDTH)


def _pair_o_rows(w):
    n = w.shape[1]
    return w.reshape(N_KV, GROUP, HEAD_DIM, n).transpose(1, 0, 2, 3).reshape(Q_WIDTH, n)


def _prep_qkv_weight(w_in):
    q_scale = HEAD_DIM ** -0.5
    sec = lambda off, width: w_in[:, off:off + width]
    cols = [_pair_q_cols(sec(OFF_QA, Q_WIDTH)) * q_scale,
            _pair_k_cols(sec(OFF_KVC, KV_WIDTH)), _pair_k_cols(sec(OFF_KVS, KV_WIDTH)),
            _pair_k_cols(sec(OFF_KVW, KV_WIDTH)),
            _pair_q_cols(sec(OFF_QB, Q_WIDTH)) * q_scale,
            _pair_k_cols(sec(OFF_KVB, KV_WIDTH)),
            sec(OFF_KVC + KV_WIDTH, KV_WIDTH)]
    v_cols = [sec(OFF_KVS + KV_WIDTH, KV_WIDTH), sec(OFF_KVW + KV_WIDTH, KV_WIDTH),
              sec(OFF_KVB + KV_WIDTH, KV_WIDTH)]
    return (jnp.concatenate(cols, axis=1).astype(jnp.bfloat16),
            jnp.concatenate(v_cols, axis=1).T.astype(jnp.bfloat16))


def _prep_gate_weight(w_in):
    pad = jnp.zeros((D_MODEL, LANES - 3 * N_HEADS), w_in.dtype)
    cols = [w_in[:, OFF_GN:OFF_GN + 3 * N_HEADS], pad,
            _pair_o_cols(w_in[:, OFF_ZA:OFF_ZA + Q_WIDTH]), _pair_o_cols(w_in[:, OFF_ZB:OFF_ZB + Q_WIDTH]),
            w_in[:, OFF_GM:OFF_GM + 2 * D_MODEL]]
    return jnp.concatenate(cols, axis=1).astype(jnp.bfloat16)


def _gate_expand_matrix():
    e = np.zeros((LANES, 3 * Q_WIDTH), np.float32)
    for br in range(3):
        for c in range(Q_WIDTH):
            h = int(O_PERM[c]) // HEAD_DIM
            e[h * 3 + br, br * Q_WIDTH + c] = 1.0
    return jnp.asarray(np.concatenate([e, e], axis=0), jnp.bfloat16)


def _prep_compress(pos, w1, w2, paired):
    half_len = CMP_LEN // 2
    w = w1.reshape(CMP_LEN, HEAD_DIM, CMP_HIDDEN).astype(jnp.bfloat16)
    w2 = w2.astype(jnp.bfloat16)
    pieces = ([(g, slice(h * HALF, (h + 1) * HALF)) for h in range(2) for g in range(N_KV)] if paired
              else [(g, slice(0, HEAD_DIM)) for g in range(N_KV)])
    own = lambda g, blk: jnp.concatenate([blk if k == g else jnp.zeros_like(blk) for k in range(N_KV)], axis=-1)
    w1x = jnp.concatenate([own(g, w[:, d, :]) for g, d in pieces], axis=1)
    first = w1x[:half_len].reshape(half_len * LANES, N_KV * CMP_HIDDEN)
    second = w1x[half_len:].reshape(half_len * LANES, N_KV * CMP_HIDDEN)
    w1e = jnp.concatenate([first, second], axis=1)
    w2e = jnp.concatenate([jnp.concatenate([w2[:, d] if k == g else jnp.zeros_like(w2[:, d]) for g, d in pieces], axis=1)
                           for k in range(N_KV)], axis=0)
    pos_l = jnp.concatenate([pos[:, d] for _, d in pieces], axis=1)
    pos_rows = jnp.concatenate([pos_l[:half_len].reshape(1, -1), pos_l[half_len:].reshape(1, -1),
                                jnp.zeros((6, half_len * LANES), pos.dtype)], axis=0).astype(jnp.bfloat16)
    return w1e, pos_rows, w2e


def _overlap_t(n_cmp, n_blk):
    c0 = np.arange(n_cmp) * CMP_STRIDE
    j0 = np.arange(n_blk) * SEL_BLOCK
    overlap = (c0[:, None] < j0[None, :] + SEL_BLOCK) & (c0[:, None] + CMP_LEN > j0[None, :])
    ovt = np.zeros((LANES, LANES), np.float32)
    ovt[:n_blk, :n_cmp] = overlap.T
    return jnp.asarray(ovt, jnp.bfloat16)


def _block_onehot(seq):
    n_blk = seq // SEL_BLOCK
    e = np.zeros((seq, LANES), np.float32)
    pos = np.arange(seq)
    e[pos, pos // SEL_BLOCK] = 1.0
    e[pos, n_blk + pos // SEL_BLOCK] = 1.0
    return jnp.asarray(e, jnp.bfloat16)


def kernel(x, c, positions, w_ada, b_ada, w_in, cmp_pos_k, cmp_w1_k, cmp_w2_k,
           cmp_pos_v, cmp_w1_v, cmp_w2_v, sinks, w_up_a, w_up_b, w_out, ln_g, ln_b):
    bsz, seq, _ = x.shape
    n_cmp = (seq - CMP_LEN) // CMP_STRIDE + 1
    n_blk = seq // SEL_BLOCK
    assert seq // CMP_STRIDE == LANES and 2 * n_blk <= LANES
    cos_t, sin_t = _rope_tables(positions, QKV_ROWS)
    for l in range(DEPTH):
        mod3 = _adaln_mod(c, w_ada[l], b_ada[l]).reshape(bsz, 3, D_MODEL)
        w_qk, w_vt = _prep_qkv_weight(w_in[l])
        qa, k_c, k_s, k_w, qb, k_b, v_c, vt_s, vt_w, vt_b = _qkv_proj(x, mod3, w_qk, w_vt, cos_t, sin_t)

        w1k, posk, w2k = _prep_compress(cmp_pos_k[l], cmp_w1_k[l], cmp_w2_k[l], paired=True)
        w1v, posv, w2v = _prep_compress(cmp_pos_v[l], cmp_w1_v[l], cmp_w2_v[l], paired=False)
        kc, vct = _compress(k_c, v_c, w1k, w1v, posk, posv, w2k, w2v.T, bb=4 if bsz % 4 == 0 else 1)

        o_cmp, sel_bias = _cmp_attn(qa, kc, vct, _overlap_t(n_cmp, n_blk), n_cmp)
        o_sel = _sel_attn(qa, sel_bias, k_s, vt_s, _block_onehot(seq))
        o_win = _band_attn(qa, k_w, vt_w, NSA_WINDOW)
        o_b = _band_attn(qb, k_b, vt_b, SWA_WINDOW, sinks[l].astype(jnp.float32))

        x = _merge_out(x, mod3, _prep_gate_weight(w_in[l]), _gate_expand_matrix(),
                       o_cmp, o_sel, o_win, o_b,
                       _pair_o_rows(w_up_a[l]).astype(jnp.bfloat16), _pair_o_rows(w_up_b[l]).astype(jnp.bfloat16),
                       w_out[l].astype(jnp.bfloat16), ln_g[l].reshape(1, D_MODEL), ln_b[l].reshape(1, D_MODEL))
    return x
```

```python
import functools

import numpy as np
import jax
import jax.numpy as jnp
from jax import lax
from jax.experimental import pallas as pl
from jax.experimental.pallas import tpu as pltpu

D_MODEL = 1024
HEAD_DIM = 64
HALF = HEAD_DIM // 2
N_HEADS = 8
N_KV = 2
GROUP = N_HEADS // N_KV
Q_WIDTH = N_HEADS * HEAD_DIM
KV_WIDTH = N_KV * HEAD_DIM
CMP_LEN = 32
CMP_STRIDE = 16
CMP_HIDDEN = 256
SEL_BLOCK = 64
SEL_TOPN = 8
NSA_WINDOW = 512
SWA_WINDOW = 128
ROPE_THETA = 10000.0
LN_EPS = 1e-5
NEG_INF = -1e30
FORCE_SCORE = 1e9
DEPTH = 1
DEEPNORM_ALPHA = (2 * DEPTH) ** 0.25
LANES = 128
SUBLANES = 8
BF16_SUBLANES = 16
ROPE_PACK = LANES // HALF
QKV_ROWS = 256
LOG2E = 1.4426950408889634
G_ROWS = HEAD_DIM + BF16_SUBLANES
V_ROWS = N_KV * G_ROWS
VMEM_LIMIT = 56 * 1024 * 1024

_SPLITS = [Q_WIDTH, 2 * KV_WIDTH, 2 * KV_WIDTH, 2 * KV_WIDTH, 3 * N_HEADS, Q_WIDTH,
           Q_WIDTH, 2 * KV_WIDTH, Q_WIDTH, 2 * D_MODEL]
_OFFS = [int(o) for o in np.cumsum([0] + _SPLITS)]
(OFF_QA, OFF_KVC, OFF_KVS, OFF_KVW, OFF_GN, OFF_ZA, OFF_QB, OFF_KVB, OFF_ZB, OFF_GM) = _OFFS[:10]


def _q_pair_perm():
    cols = []
    for j in range(GROUP):
        for half in range(2):
            for grp in range(N_KV):
                h = j + GROUP * grp
                cols += [h * HEAD_DIM + half * HALF + i for i in range(HALF)]
    return np.asarray(cols, np.int32)


def _k_pair_perm():
    cols = []
    for half in range(2):
        for grp in range(N_KV):
            cols += [grp * HEAD_DIM + half * HALF + i for i in range(HALF)]
    return np.asarray(cols, np.int32)


def _o_pair_perm():
    cols = []
    for j in range(GROUP):
        for grp in range(N_KV):
            h = j + GROUP * grp
            cols += [h * HEAD_DIM + d for d in range(HEAD_DIM)]
    return np.asarray(cols, np.int32)


Q_PERM = _q_pair_perm()
K_PERM = _k_pair_perm()
O_PERM = _o_pair_perm()


def _cparams(sem):
    return pltpu.CompilerParams(dimension_semantics=sem, vmem_limit_bytes=VMEM_LIMIT)


def _dot(a, b):
    return jnp.dot(a, b, preferred_element_type=jnp.float32)


def _dot_nt(a, b):
    return lax.dot_general(a, b, (((1,), (1,)), ((), ())), preferred_element_type=jnp.float32)


def _layer_norm(x):
    mu = jnp.mean(x, axis=-1, keepdims=True)
    xc = x - mu
    var = jnp.mean(xc * xc, axis=-1, keepdims=True)
    return xc * lax.rsqrt(var + LN_EPS)


def _sigmoid(x):
    return 1.0 / (1.0 + jnp.exp(-x))


def _adaln_kernel(c_ref, w_ref, b_ref, o_ref):
    c = c_ref[...]
    a = c * _sigmoid(c)
    o_ref[...] = jnp.dot(a, w_ref[...], preferred_element_type=jnp.float32,
                         precision=lax.Precision.HIGHEST) + b_ref[...]


def _adaln_mod(c, w_ada, b_ada):
    bsz = c.shape[0]
    n = w_ada.shape[1]
    tn = 512
    return pl.pallas_call(
        _adaln_kernel,
        out_shape=jax.ShapeDtypeStruct((bsz, n), jnp.float32),
        grid=(n // tn,),
        in_specs=[pl.BlockSpec((bsz, D_MODEL), lambda j: (0, 0)),
                  pl.BlockSpec((D_MODEL, tn), lambda j: (0, j)),
                  pl.BlockSpec((1, tn), lambda j: (0, j))],
        out_specs=pl.BlockSpec((bsz, tn), lambda j: (0, j)),
        compiler_params=_cparams(("arbitrary",)),
        name="adaln_mod",
    )(c, w_ada, b_ada.reshape(1, n))


def _rope_table_kernel(pos_ref, invf_ref, cos_ref, sin_ref):
    ang = pos_ref[...] * invf_ref[...]
    cos_ref[...] = jnp.cos(ang)
    sin_ref[...] = jnp.sin(ang)


def _rope_tables(positions, ts):
    bsz, seq = positions.shape
    sub = ts // ROPE_PACK
    rows = bsz * seq // ROPE_PACK
    pos = positions.astype(jnp.float32).reshape(bsz, seq // ts, ROPE_PACK, sub).transpose(0, 1, 3, 2)
    pos = jnp.broadcast_to(pos[..., None], pos.shape + (HALF,)).reshape(rows, LANES)
    inv_freq = ROPE_THETA ** (-jnp.arange(HALF, dtype=jnp.float32) / HALF)
    invf = jnp.broadcast_to(inv_freq[None, :], (ROPE_PACK, HALF)).reshape(1, LANES)
    tr = seq // ROPE_PACK
    cos, sin = pl.pallas_call(
        _rope_table_kernel,
        out_shape=[jax.ShapeDtypeStruct((rows, LANES), jnp.float32)] * 2,
        grid=(rows // tr,),
        in_specs=[pl.BlockSpec((tr, LANES), lambda i: (i, 0)),
                  pl.BlockSpec((1, LANES), lambda i: (0, 0))],
        out_specs=[pl.BlockSpec((tr, LANES), lambda i: (i, 0))] * 2,
        compiler_params=_cparams(("arbitrary",)),
        name="rope_table",
    )(pos, invf)
    return cos.reshape(bsz, tr, LANES), sin.reshape(bsz, tr, LANES)


N_QK_SLABS = 13
N_VT = 3


def _qkv_kernel(x_ref, mod_ref, w_ref, wvt_ref, cos_ref, sin_ref,
                qa_ref, kc_ref, ks_ref, kw_ref, qb_ref, kb_ref, vc_ref,
                vst_ref, vwt_ref, vbt_ref, kstage_ref, vstage_ref, *, ts, n_sub):
    shift = mod_ref[0, 0:1, :]
    scale = mod_ref[0, 1:2, :]

    def project(r):
        u = (_layer_norm(x_ref[0, r * ts:(r + 1) * ts, :]) * (1.0 + scale) + shift).astype(jnp.bfloat16)
        return _dot(u, w_ref[...]), _dot_nt(wvt_ref[...], u)

    def finish(r, acc, vt):
        rows = slice(r * ts, (r + 1) * ts)
        sub = ts // ROPE_PACK
        for k in range(ROPE_PACK):
            c32 = cos_ref[0, r * sub:(r + 1) * sub, k * HALF:(k + 1) * HALF]
            s32 = sin_ref[0, r * sub:(r + 1) * sub, k * HALF:(k + 1) * HALF]
            cos = jnp.concatenate([c32, c32, c32, c32], axis=1)
            sin = jnp.concatenate([-s32, -s32, s32, s32], axis=1)
            tok = slice(r * ts + k * sub, r * ts + (k + 1) * sub)

            def rope(slab):
                t = acc[k * sub:(k + 1) * sub, slab * LANES:(slab + 1) * LANES]
                return t * cos + pltpu.roll(t, LANES // 2, 1) * sin

            for j in range(4):
                qa_ref[0, tok, j * LANES:(j + 1) * LANES] = (rope(j) * LOG2E).astype(jnp.bfloat16)
                qb_ref[0, tok, j * LANES:(j + 1) * LANES] = (rope(7 + j) * LOG2E).astype(jnp.bfloat16)
            for slab, ref in ((5, ks_ref), (6, kw_ref), (11, kb_ref)):
                ref[0, tok, :] = rope(slab).astype(jnp.bfloat16)
            kstage_ref[k * sub:(k + 1) * sub, :] = rope(4)
        vstage_ref[...] = acc[:, 12 * LANES:13 * LANES]
        n_ch = ts // CMP_STRIDE
        for l in range(CMP_STRIDE):
            for stage, ref in ((kstage_ref, kc_ref), (vstage_ref, vc_ref)):
                ref[0, r * n_ch:(r + 1) * n_ch, l * LANES:(l + 1) * LANES] = (
                    stage[pl.ds(l, n_ch, stride=CMP_STRIDE), :].astype(jnp.bfloat16))
        ones = jnp.ones((G_ROWS - HEAD_DIM, ts), jnp.bfloat16)
        for j, ref in enumerate((vst_ref, vwt_ref, vbt_ref)):
            for g in range(N_KV):
                lo = j * KV_WIDTH + g * HEAD_DIM
                ref[0, g * G_ROWS:g * G_ROWS + HEAD_DIM, rows] = vt[lo:lo + HEAD_DIM].astype(jnp.bfloat16)
                ref[0, g * G_ROWS + HEAD_DIM:(g + 1) * G_ROWS, rows] = ones

    nxt = project(0)
    for r in range(n_sub):
        cur = nxt
        if r + 1 < n_sub:
            nxt = project(r + 1)
        finish(r, *cur)


def _qkv_proj(x, mod3, w_qk, w_vt, cos_t, sin_t, ts=QKV_ROWS, n_sub=4):
    bsz, seq, _ = x.shape
    tm = ts * n_sub
    chunk_w = CMP_STRIDE * KV_WIDTH
    q_shape = jax.ShapeDtypeStruct((bsz, seq, Q_WIDTH), jnp.bfloat16)
    k_shape = jax.ShapeDtypeStruct((bsz, seq, KV_WIDTH), jnp.bfloat16)
    c_shape = jax.ShapeDtypeStruct((bsz, seq // CMP_STRIDE, chunk_w), jnp.bfloat16)
    vt_shape = jax.ShapeDtypeStruct((bsz, V_ROWS, seq), jnp.bfloat16)
    q_spec = pl.BlockSpec((1, tm, Q_WIDTH), lambda b, i: (b, i, 0))
    k_spec = pl.BlockSpec((1, tm, KV_WIDTH), lambda b, i: (b, i, 0))
    c_spec = pl.BlockSpec((1, tm // CMP_STRIDE, chunk_w), lambda b, i: (b, i, 0))
    t_spec = pl.BlockSpec((1, tm // ROPE_PACK, LANES), lambda b, i: (b, i, 0))
    vt_spec = pl.BlockSpec((1, V_ROWS, tm), lambda b, i: (b, 0, i))
    return pl.pallas_call(
        functools.partial(_qkv_kernel, ts=ts, n_sub=n_sub),
        out_shape=[q_shape, c_shape, k_shape, k_shape, q_shape, k_shape, c_shape,
                   vt_shape, vt_shape, vt_shape],
        grid=(bsz, seq // tm),
        in_specs=[pl.BlockSpec((1, tm, D_MODEL), lambda b, i: (b, i, 0)),
                  pl.BlockSpec((1, 3, D_MODEL), lambda b, i: (b, 0, 0)),
                  pl.BlockSpec((D_MODEL, N_QK_SLABS * LANES), lambda b, i: (0, 0)),
                  pl.BlockSpec((N_VT * KV_WIDTH, D_MODEL), lambda b, i: (0, 0)),
                  t_spec, t_spec],
        out_specs=[q_spec, c_spec, k_spec, k_spec, q_spec, k_spec, c_spec,
                   vt_spec, vt_spec, vt_spec],
        scratch_shapes=[pltpu.VMEM((ts, KV_WIDTH), jnp.float32), pltpu.VMEM((ts, KV_WIDTH), jnp.float32)],
        compiler_params=_cparams(("arbitrary", "arbitrary")),
        name="qkv_proj",
    )(x, mod3, w_qk, w_vt, cos_t, sin_t)


def _compress_kernel(xk_ref, xv_ref, w1k_ref, w1v_ref, posk_ref, posv_ref, w2k_ref, w2vt_ref,
                     kc_ref, vct_ref):
    bb, n_chunks, cw = xk_ref.shape

    def hidden(x_ref, w1_ref, pos_ref):
        hw = w1_ref.shape[1] // 2
        p = _dot(x_ref[...].reshape(bb * n_chunks, cw), w1_ref[...])
        pt = _dot(pos_ref[...], w1_ref[...])
        pos_term = pt[0:1, 0:hw] + pt[1:2, hw:]
        first = p[:, 0:hw]
        second = pltpu.roll(p[:, hw:], bb * n_chunks - 1, 0)
        hid = first + second + pos_term
        return (hid * _sigmoid(hid)).astype(jnp.bfloat16)

    kc = _dot(hidden(xk_ref, w1k_ref, posk_ref), w2k_ref[...]).astype(jnp.bfloat16)
    vct = _dot_nt(w2vt_ref[...], hidden(xv_ref, w1v_ref, posv_ref)).astype(jnp.bfloat16)
    for i in range(bb):
        kc_ref[i] = kc[i * n_chunks:(i + 1) * n_chunks]
        vct_ref[i] = vct[:, i * n_chunks:(i + 1) * n_chunks]


def _compress(xk, xv, w1k, w1v, posk, posv, w2k, w2vt, bb=4):
    bsz, n_chunks, cw = xk.shape
    hw2 = w1k.shape[1]
    out = jax.ShapeDtypeStruct((bsz, n_chunks, LANES), jnp.bfloat16)
    x_spec = pl.BlockSpec((bb, n_chunks, cw), lambda b: (b, 0, 0))
    w1_spec = pl.BlockSpec((cw, hw2), lambda b: (0, 0))
    pos_spec = pl.BlockSpec((8, cw), lambda b: (0, 0))
    o_spec = pl.BlockSpec((bb, n_chunks, LANES), lambda b: (b, 0, 0))
    return pl.pallas_call(
        _compress_kernel,
        out_shape=[out, out],
        grid=(bsz // bb,),
        in_specs=[x_spec, x_spec, w1_spec, w1_spec, pos_spec, pos_spec,
                  pl.BlockSpec(w2k.shape, lambda b: (0, 0)), pl.BlockSpec(w2vt.shape, lambda b: (0, 0))],
        out_specs=[o_spec, o_spec],
        compiler_params=_cparams(("arbitrary",)),
        name="compress",
    )(xk, xv, w1k, w1v, posk, posv, w2k, w2vt)


def _stack_heads(q):
    lane = lax.broadcasted_iota(jnp.int32, (1, LANES), 1)
    is_a = (lane & (HEAD_DIM - 1)) < HALF
    zero = jnp.zeros((), q.dtype)
    tiles = [q[:, j * LANES:(j + 1) * LANES] for j in range(GROUP)]
    parts = [jnp.where(is_a, t, zero) for t in tiles] + [jnp.where(is_a, zero, t) for t in tiles]
    return jnp.concatenate(parts, axis=0)


def _store_heads(o_ref, rows, o_g, inv_l, tq):
    for j in range(GROUP):
        cols = slice(j * tq, (j + 1) * tq)
        halves = [o_g[g][0:HEAD_DIM, cols] for g in range(N_KV)]
        if inv_l is not None:
            halves = [h * inv_l[g][:, cols] for g, h in enumerate(halves)]
        tile_t = jnp.concatenate(halves, axis=0)
        o_ref[0, rows, j * LANES:(j + 1) * LANES] = tile_t.T.astype(jnp.bfloat16)


def _group_cols(g, tq):
    return slice(g * GROUP * tq, (g + 1) * GROUP * tq)


def _cmp_kernel(q_ref, kc_ref, vct_ref, ovt_ref, o_ref, bias_ref, *, tq, n_cmp, n_blk):
    qi = pl.program_id(1)
    nq = tq // LANES
    qs = _stack_heads(q_ref[0])
    c_row = lax.broadcasted_iota(jnp.int32, (LANES, tq), 0)
    t_col = qi * tq + lax.broadcasted_iota(jnp.int32, (LANES, tq), 1)
    vis = (c_row * CMP_STRIDE + (CMP_LEN - 1) <= t_col) & (c_row < n_cmp)
    any_vis = (t_col[0:1, :] >= CMP_LEN - 1).astype(jnp.float32)
    s_t = _dot_nt(kc_ref[0], qs)
    p_parts = []
    for cb in range(N_HEADS * nq):
        qq = slice((cb % nq) * LANES, (cb % nq + 1) * LANES)
        s = jnp.where(vis[:, qq], s_t[:, cb * LANES:(cb + 1) * LANES], NEG_INF)
        e = jnp.exp2(s - jnp.max(s, axis=0, keepdims=True))
        p_parts.append(e * (any_vis[:, qq] / jnp.sum(e, axis=0, keepdims=True)))
    p_t = jnp.concatenate(p_parts, axis=1)
    o_g = [_dot(vct_ref[0, g * HEAD_DIM:(g + 1) * HEAD_DIM, :], p_t[:, _group_cols(g, tq)].astype(jnp.bfloat16))
           for g in range(N_KV)]
    _store_heads(o_ref, slice(None), o_g, None, tq)

    j_blk = lax.broadcasted_iota(jnp.int32, (n_blk, tq), 0)
    t_q = qi * tq + lax.broadcasted_iota(jnp.int32, (n_blk, tq), 1)
    cur = jnp.right_shift(t_q, SEL_BLOCK.bit_length() - 1)
    forced = (j_blk == 0) | (j_blk == cur) | (j_blk == cur - 1)
    causal = j_blk <= cur
    n_free = SEL_TOPN - 3
    take_all = cur < SEL_TOPN
    candidate = causal & jnp.logical_not(forced)
    rows = []
    for g in range(N_KV):
        pg = p_t[:, g * GROUP * tq:(g * GROUP + 1) * tq]
        for h in range(1, GROUP):
            pg = pg + p_t[:, (g * GROUP + h) * tq:(g * GROUP + h + 1) * tq]
        hi = pg.astype(jnp.bfloat16)
        lo = (pg - hi.astype(jnp.float32)).astype(jnp.bfloat16)
        imp_t = _dot(ovt_ref[...], hi) + _dot(ovt_ref[...], lo)
        work = jnp.where(candidate, imp_t[0:n_blk, :], NEG_INF)
        picked = jnp.zeros((n_blk, tq), jnp.int32)
        for _ in range(n_free):
            top = jnp.max(work, axis=0, keepdims=True)
            first = jnp.min(jnp.where(work == top, j_blk, n_blk), axis=0, keepdims=True)
            hit = j_blk == first
            picked = jnp.where(hit, 1, picked)
            work = jnp.where(hit, NEG_INF, work)
        keep = causal & (take_all | forced | ((picked > 0) & candidate))
        rows.append(jnp.where(keep, 0.0, NEG_INF))
    rows.append(jnp.zeros((LANES - N_KV * n_blk, tq), jnp.float32))
    bias_t = jnp.concatenate(rows, axis=0)
    bias_ref[0] = bias_t.T.astype(jnp.bfloat16)


def _cmp_attn(qa, kc, vct, ovt, n_cmp, tq=1024):
    bsz, seq, _ = qa.shape
    n_blk = seq // SEL_BLOCK
    kern = functools.partial(_cmp_kernel, tq=tq, n_cmp=n_cmp, n_blk=n_blk)
    kv_spec = pl.BlockSpec((1, LANES, LANES), lambda b, i: (b, 0, 0))
    return pl.pallas_call(
        kern,
        out_shape=[jax.ShapeDtypeStruct((bsz, seq, Q_WIDTH), jnp.bfloat16),
                   jax.ShapeDtypeStruct((bsz, seq, LANES), jnp.bfloat16)],
        grid=(bsz, seq // tq),
        in_specs=[pl.BlockSpec((1, tq, Q_WIDTH), lambda b, i: (b, i, 0)),
                  kv_spec, kv_spec,
                  pl.BlockSpec((LANES, LANES), lambda b, i: (0, 0))],
        out_specs=[pl.BlockSpec((1, tq, Q_WIDTH), lambda b, i: (b, i, 0)),
                   pl.BlockSpec((1, tq, LANES), lambda b, i: (b, i, 0))],
        compiler_params=_cparams(("arbitrary", "arbitrary")),
        name="cmp_attn",
    )(qa, kc, vct, ovt)


def _sel_kernel(q_ref, bias_ref, k_ref, vt_ref, blk_ref, o_ref, kaug_ref, qaug_ref, s0_ref, s1_ref, acc_ref, m_ref,
                *, tq, tk):
    step = pl.program_id(1)
    nq = tq // LANES
    nk = tk // LANES
    per_g = GROUP * nq

    @pl.when(step == 0)
    def _():
        kaug_ref[:, 0:LANES] = k_ref[0]
        kaug_ref[:, LANES:2 * LANES] = blk_ref[...]

    lane = lax.broadcasted_iota(jnp.int32, (1, LANES), 1)
    n_blk = blk_ref.shape[0] // SEL_BLOCK
    bt = bias_ref[0]
    zero = jnp.zeros((), bt.dtype)
    b0 = jnp.where(lane < n_blk, bt, zero)
    b1 = jnp.where((lane >= n_blk) & (lane < 2 * n_blk), bt, zero)
    qaug_ref[:, 0:LANES] = _stack_heads(q_ref[0])
    qaug_ref[:, LANES:2 * LANES] = jnp.concatenate([b0] * GROUP + [b1] * GROUP, axis=0)

    acc_ref[...] = jnp.zeros_like(acc_ref)
    m_ref[...] = jnp.full_like(m_ref, NEG_INF)
    row = lax.broadcasted_iota(jnp.int32, (tk, tk), 0)
    col = lax.broadcasted_iota(jnp.int32, (tk, tk), 1)
    causal_bias = jnp.where(row <= col, 0.0, NEG_INF)

    def scores(c, s_ref):
        k0 = pl.multiple_of(c * tk, tk)
        s_ref[...] = _dot_nt(kaug_ref[pl.ds(k0, tk), :], qaug_ref[...])

    def accumulate(c, s_ref, diag_half):
        k0 = pl.multiple_of(c * tk, tk)
        for g in range(N_KV):
            ps, alphas = [], []
            for i in range(per_g):
                cb = g * per_g + i
                half, within = divmod(cb % nq, nk)
                cols = slice(cb * LANES, (cb + 1) * LANES)
                if diag_half is not None and half < diag_half:
                    ps.append(jnp.zeros((tk, LANES), jnp.bfloat16))
                    alphas.append(jnp.ones((1, LANES), jnp.float32))
                    continue
                s = s_ref[:, cols]
                if diag_half is not None and half == diag_half:
                    s = s + causal_bias[:, within * LANES:(within + 1) * LANES]
                m_old = m_ref[0:1, cols]
                m_new = jnp.maximum(m_old, jnp.max(s, axis=0, keepdims=True))
                m_ref[0:1, cols] = m_new
                alphas.append(jnp.exp2(m_old - m_new))
                ps.append(jnp.exp2(s - m_new).astype(jnp.bfloat16))
            vt_g = vt_ref[0, g * G_ROWS:(g + 1) * G_ROWS, pl.ds(k0, tk)]
            pv = _dot(vt_g, jnp.concatenate(ps, axis=1))
            acc_ref[g] = jnp.concatenate(alphas, axis=1) * acc_ref[g] + pv

    scores(0, s0_ref)

    def body(i, carry):
        c = 2 * i
        scores(c + 1, s1_ref)
        accumulate(c, s0_ref, None)
        scores(c + 2, s0_ref)
        accumulate(c + 1, s1_ref, None)
        return carry

    lax.fori_loop(0, step, body, 0)
    c0 = 2 * step
    scores(c0 + 1, s1_ref)
    accumulate(c0, s0_ref, 0)
    accumulate(c0 + 1, s1_ref, 1)

    o_g = [acc_ref[g] for g in range(N_KV)]
    inv_l = [1.0 / o[HEAD_DIM:HEAD_DIM + 1, :] for o in o_g]
    _store_heads(o_ref, slice(None), o_g, inv_l, tq)


def _sel_attn(qa, bias, ks, vst, blk_onehot, tk=256):
    bsz, seq, _ = qa.shape
    tq = 2 * tk
    kern = functools.partial(_sel_kernel, tq=tq, tk=tk)
    return pl.pallas_call(
        kern,
        out_shape=jax.ShapeDtypeStruct((bsz, seq, Q_WIDTH), jnp.bfloat16),
        grid=(bsz, seq // tq),
        in_specs=[pl.BlockSpec((1, tq, Q_WIDTH), lambda b, i: (b, i, 0)),
                  pl.BlockSpec((1, tq, LANES), lambda b, i: (b, i, 0)),
                  pl.BlockSpec((1, seq, LANES), lambda b, i: (b, 0, 0)),
                  pl.BlockSpec((1, V_ROWS, seq), lambda b, i: (b, 0, 0)),
                  pl.BlockSpec((seq, LANES), lambda b, i: (0, 0))],
        out_specs=pl.BlockSpec((1, tq, Q_WIDTH), lambda b, i: (b, i, 0)),
        scratch_shapes=[pltpu.VMEM((seq, 2 * LANES), jnp.bfloat16),
                        pltpu.VMEM((N_HEADS * tq, 2 * LANES), jnp.bfloat16),
                        pltpu.VMEM((tk, N_HEADS * tq), jnp.float32),
                        pltpu.VMEM((tk, N_HEADS * tq), jnp.float32),
                        pltpu.VMEM((N_KV, G_ROWS, GROUP * tq), jnp.float32),
                        pltpu.VMEM((8, N_HEADS * tq), jnp.float32)],
        compiler_params=_cparams(("arbitrary", "arbitrary")),
        name="sel_attn",
    )(qa, bias, ks, vst, blk_onehot)


def _band_kernel(*refs, tq, n_sub, span, window, use_sink):
    if use_sink:
        q_ref, k_ref, vt_ref, sink_ref, o_ref = refs
    else:
        q_ref, k_ref, vt_ref, o_ref = refs
    step = pl.program_id(1)
    seq = k_ref.shape[1]
    nq = tq // LANES
    row = lax.broadcasted_iota(jnp.int32, (span, tq), 0)
    col = lax.broadcasted_iota(jnp.int32, (span, tq), 1)
    diff = col - row

    def scores(sub):
        q0 = (step * n_sub + sub) * tq
        start = pl.multiple_of(jnp.clip(q0 + tq - span, 0, seq - span), LANES)
        rel = diff + (q0 - start)
        bias = jnp.where((rel >= 0) & (rel < window), 0.0, NEG_INF)
        qs = _stack_heads(q_ref[0, sub * tq:(sub + 1) * tq, :])
        return start, bias, _dot_nt(k_ref[0, pl.ds(start, span), :], qs)

    def finish(sub, start, bias, s_t):
        ps, extra = [], []
        for cb in range(N_HEADS * nq):
            s = s_t[:, cb * LANES:(cb + 1) * LANES] + bias[:, (cb % nq) * LANES:(cb % nq + 1) * LANES]
            m = jnp.max(s, axis=0, keepdims=True)
            if use_sink:
                sink = sink_ref[cb // nq] * LOG2E
                m = jnp.maximum(m, sink)
                extra.append(jnp.exp2(sink - m))
            ps.append(jnp.exp2(s - m).astype(jnp.bfloat16))
        o_g, inv_l = [], []
        for g in range(N_KV):
            per_g = GROUP * nq
            vt_g = vt_ref[0, g * G_ROWS:(g + 1) * G_ROWS, pl.ds(start, span)]
            o = _dot(vt_g, jnp.concatenate(ps[g * per_g:(g + 1) * per_g], axis=1))
            l = o[HEAD_DIM:HEAD_DIM + 1, :]
            if use_sink:
                l = l + jnp.concatenate(extra[g * per_g:(g + 1) * per_g], axis=1)
            o_g.append(o)
            inv_l.append(1.0 / l)
        _store_heads(o_ref, slice(sub * tq, (sub + 1) * tq), o_g, inv_l, tq)

    nxt = scores(0)
    for sub in range(n_sub):
        cur = nxt
        if sub + 1 < n_sub:
            nxt = scores(sub + 1)
        finish(sub, *cur)


def _band_attn(q, k, vt, window, sinks=None, tq=128, n_sub=16):
    bsz, seq, _ = q.shape
    span = (-(-(window - 1) // tq)) * tq + tq
    use_sink = sinks is not None
    kern = functools.partial(_band_kernel, tq=tq, n_sub=n_sub, span=span, window=window, use_sink=use_sink)
    ts = tq * n_sub
    in_specs = [pl.BlockSpec((1, ts, Q_WIDTH), lambda b, i: (b, i, 0)),
                pl.BlockSpec((1, seq, LANES), lambda b, i: (b, 0, 0)),
                pl.BlockSpec((1, V_ROWS, seq), lambda b, i: (b, 0, 0))]
    args = [q, k, vt]
    if use_sink:
        in_specs.append(pl.BlockSpec(memory_space=pltpu.SMEM))
        args.append(sinks)
    return pl.pallas_call(
        kern,
        out_shape=jax.ShapeDtypeStruct((bsz, seq, Q_WIDTH), jnp.bfloat16),
        grid=(bsz, seq // ts),
        in_specs=in_specs,
        out_specs=pl.BlockSpec((1, ts, Q_WIDTH), lambda b, i: (b, i, 0)),
        compiler_params=_cparams(("arbitrary", "arbitrary")),
        name="band_attn_sink" if use_sink else "band_attn",
    )(*args)


GATE_WIDTH = LANES + 2 * Q_WIDTH + 2 * D_MODEL


def _merge_kernel(x_ref, mod_ref, wg_ref, gexp_ref, ocmp_ref, osel_ref, owin_ref, ob_ref,
                  wua_ref, wub_ref, wo_ref, lng_ref, lnb_ref, o_ref, *, ts, n_sub):
    shift = mod_ref[0, 0:1, :]
    scale = mod_ref[0, 1:2, :]
    gate = mod_ref[0, 2:3, :]

    def project(r):
        u = (_layer_norm(x_ref[0, r * ts:(r + 1) * ts, :]) * (1.0 + scale) + shift).astype(jnp.bfloat16)
        return _dot(u, wg_ref[...])

    def finish(r, g_all):
        rows = slice(r * ts, (r + 1) * ts)
        sig = _sigmoid(g_all[:, 0:LANES])
        hi = sig.astype(jnp.bfloat16)
        lo = (sig - hi.astype(jnp.float32)).astype(jnp.bfloat16)
        gx = _dot(jnp.concatenate([hi, lo], axis=1), gexp_ref[...])
        o_a = (gx[:, 0:Q_WIDTH] * ocmp_ref[0, rows, :].astype(jnp.float32)
               + gx[:, Q_WIDTH:2 * Q_WIDTH] * osel_ref[0, rows, :].astype(jnp.float32)
               + gx[:, 2 * Q_WIDTH:3 * Q_WIDTH] * owin_ref[0, rows, :].astype(jnp.float32))
        c0 = LANES
        z_a = g_all[:, c0:c0 + Q_WIDTH]
        y_a = (o_a * (z_a * _sigmoid(z_a))).astype(jnp.bfloat16)
        c0 += Q_WIDTH
        z_b = g_all[:, c0:c0 + Q_WIDTH]
        y_b = (ob_ref[0, rows, :].astype(jnp.float32) * (z_b * _sigmoid(z_b))).astype(jnp.bfloat16)
        c0 += Q_WIDTH
        merged = _sigmoid(g_all[:, c0:c0 + D_MODEL]) * _dot(y_a, wua_ref[...])
        c0 += D_MODEL
        merged = merged + _sigmoid(g_all[:, c0:c0 + D_MODEL]) * _dot(y_b, wub_ref[...])
        out = _dot(merged.astype(jnp.bfloat16), wo_ref[...])
        y = DEEPNORM_ALPHA * x_ref[0, rows, :] + gate * out
        o_ref[0, rows, :] = _layer_norm(y) * lng_ref[...] + lnb_ref[...]

    nxt = project(0)
    for r in range(n_sub):
        cur = nxt
        if r + 1 < n_sub:
            nxt = project(r + 1)
        finish(r, cur)


def _merge_out(x, mod3, wg, gexp, o_cmp, o_sel, o_win, o_b, wua, wub, wo, ln_g, ln_b, ts=256, n_sub=4):
    bsz, seq, _ = x.shape
    tm = ts * n_sub
    tok = lambda w: pl.BlockSpec((1, tm, w), lambda b, i: (b, i, 0))
    full = lambda a: pl.BlockSpec(a.shape, lambda b, i: (0,) * a.ndim)
    return pl.pallas_call(
        functools.partial(_merge_kernel, ts=ts, n_sub=n_sub),
        out_shape=jax.ShapeDtypeStruct(x.shape, jnp.float32),
        grid=(bsz, seq // tm),
        in_specs=[tok(D_MODEL), pl.BlockSpec((1, 3, D_MODEL), lambda b, i: (b, 0, 0)),
                  full(wg), full(gexp), tok(Q_WIDTH), tok(Q_WIDTH), tok(Q_WIDTH), tok(Q_WIDTH),
                  full(wua), full(wub), full(wo), full(ln_g), full(ln_b)],
        out_specs=tok(D_MODEL),
        compiler_params=_cparams(("arbitrary", "arbitrary")),
        name="merge_out",
    )(x, mod3, wg, gexp, o_cmp, o_sel, o_win, o_b, wua, wub, wo, ln_g, ln_b)


def _pair_q_cols(w):
    k = w.shape[0]
    return w.reshape(k, N_KV, GROUP, 2, HALF).transpose(0, 2, 3, 1, 4).reshape(k, Q_WIDTH)


def _pair_k_cols(w):
    k = w.shape[0]
    return w.reshape(k, N_KV, 2, HALF).transpose(0, 2, 1, 3).reshape(k, KV_WIDTH)


def _pair_o_cols(w):
    k = w.shape[0]
    return w.reshape(k, N_KV, GROUP, HEAD_DIM).transpose(0, 2, 1, 3).reshape(k, Q_WIDTH)


def _pair_o_rows(w):
    n = w.shape[1]
    return w.reshape(N_KV, GROUP, HEAD_DIM, n).transpose(1, 0, 2, 3).reshape(Q_WIDTH, n)


def _prep_qkv_weight(w_in):
    q_scale = HEAD_DIM ** -0.5
    sec = lambda off, width: w_in[:, off:off + width]
    cols = [_pair_q_cols(sec(OFF_QA, Q_WIDTH)) * q_scale,
            _pair_k_cols(sec(OFF_KVC, KV_WIDTH)), _pair_k_cols(sec(OFF_KVS, KV_WIDTH)),
            _pair_k_cols(sec(OFF_KVW, KV_WIDTH)),
            _pair_q_cols(sec(OFF_QB, Q_WIDTH)) * q_scale,
            _pair_k_cols(sec(OFF_KVB, KV_WIDTH)),
            sec(OFF_KVC + KV_WIDTH, KV_WIDTH)]
    v_cols = [sec(OFF_KVS + KV_WIDTH, KV_WIDTH), sec(OFF_KVW + KV_WIDTH, KV_WIDTH),
              sec(OFF_KVB + KV_WIDTH, KV_WIDTH)]
    return (jnp.concatenate(cols, axis=1).astype(jnp.bfloat16),
            jnp.concatenate(v_cols, axis=1).T.astype(jnp.bfloat16))


def _prep_gate_weight(w_in):
    pad = jnp.zeros((D_MODEL, LANES - 3 * N_HEADS), w_in.dtype)
    cols = [w_in[:, OFF_GN:OFF_GN + 3 * N_HEADS], pad,
            _pair_o_cols(w_in[:, OFF_ZA:OFF_ZA + Q_WIDTH]), _pair_o_cols(w_in[:, OFF_ZB:OFF_ZB + Q_WIDTH]),
            w_in[:, OFF_GM:OFF_GM + 2 * D_MODEL]]
    return jnp.concatenate(cols, axis=1).astype(jnp.bfloat16)


def _gate_expand_matrix():
    e = np.zeros((LANES, 3 * Q_WIDTH), np.float32)
    for br in range(3):
        for c in range(Q_WIDTH):
            h = int(O_PERM[c]) // HEAD_DIM
            e[h * 3 + br, br * Q_WIDTH + c] = 1.0
    return jnp.asarray(np.concatenate([e, e], axis=0), jnp.bfloat16)


def _prep_compress(pos, w1, w2, paired):
    half_len = CMP_LEN // 2
    w = w1.reshape(CMP_LEN, HEAD_DIM, CMP_HIDDEN).astype(jnp.bfloat16)
    w2 = w2.astype(jnp.bfloat16)
    pieces = ([(g, slice(h * HALF, (h + 1) * HALF)) for h in range(2) for g in range(N_KV)] if paired
              else [(g, slice(0, HEAD_DIM)) for g in range(N_KV)])
    own = lambda g, blk: jnp.concatenate([blk if k == g else jnp.zeros_like(blk) for k in range(N_KV)], axis=-1)
    w1x = jnp.concatenate([own(g, w[:, d, :]) for g, d in pieces], axis=1)
    first = w1x[:half_len].reshape(half_len * LANES, N_KV * CMP_HIDDEN)
    second = w1x[half_len:].reshape(half_len * LANES, N_KV * CMP_HIDDEN)
    w1e = jnp.concatenate([first, second], axis=1)
    w2e = jnp.concatenate([jnp.concatenate([w2[:, d] if k == g else jnp.zeros_like(w2[:, d]) for g, d in pieces], axis=1)
                           for k in range(N_KV)], axis=0)
    pos_l = jnp.concatenate([pos[:, d] for _, d in pieces], axis=1)
    pos_rows = jnp.concatenate([pos_l[:half_len].reshape(1, -1), pos_l[half_len:].reshape(1, -1),
                                jnp.zeros((6, half_len * LANES), pos.dtype)], axis=0).astype(jnp.bfloat16)
    return w1e, pos_rows, w2e


def _overlap_t(n_cmp, n_blk):
    c0 = np.arange(n_cmp) * CMP_STRIDE
    j0 = np.arange(n_blk) * SEL_BLOCK
    overlap = (c0[:, None] < j0[None, :] + SEL_BLOCK) & (c0[:, None] + CMP_LEN > j0[None, :])
    ovt = np.zeros((LANES, LANES), np.float32)
    ovt[:n_blk, :n_cmp] = overlap.T
    return jnp.asarray(ovt, jnp.bfloat16)


def _block_onehot(seq):
    n_blk = seq // SEL_BLOCK
    e = np.zeros((seq, LANES), np.float32)
    pos = np.arange(seq)
    e[pos, pos // SEL_BLOCK] = 1.0
    e[pos, n_blk + pos // SEL_BLOCK] = 1.0
    return jnp.asarray(e, jnp.bfloat16)


def kernel(x, c, positions, w_ada, b_ada, w_in, cmp_pos_k, cmp_w1_k, cmp_w2_k,
           cmp_pos_v, cmp_w1_v, cmp_w2_v, sinks, w_up_a, w_up_b, w_out, ln_g, ln_b):
    bsz, seq, _ = x.shape
    n_cmp = (seq - CMP_LEN) // CMP_STRIDE + 1
    n_blk = seq // SEL_BLOCK
    assert seq // CMP_STRIDE == LANES and 2 * n_blk <= LANES
    cos_t, sin_t = _rope_tables(positions, QKV_ROWS)
    for l in range(DEPTH):
        mod3 = _adaln_mod(c, w_ada[l], b_ada[l]).reshape(bsz, 3, D_MODEL)
        w_qk, w_vt = _prep_qkv_weight(w_in[l])
        qa, k_c, k_s, k_w, qb, k_b, v_c, vt_s, vt_w, vt_b = _qkv_proj(x, mod3, w_qk, w_vt, cos_t, sin_t)

        w1k, posk, w2k = _prep_compress(cmp_pos_k[l], cmp_w1_k[l], cmp_w2_k[l], paired=True)
        w1v, posv, w2v = _prep_compress(cmp_pos_v[l], cmp_w1_v[l], cmp_w2_v[l], paired=False)
        kc, vct = _compress(k_c, v_c, w1k, w1v, posk, posv, w2k, w2v.T, bb=4 if bsz % 4 == 0 else 1)

        o_cmp, sel_bias = _cmp_attn(qa, kc, vct, _overlap_t(n_cmp, n_blk), n_cmp)
        o_sel = _sel_attn(qa, sel_bias, k_s, vt_s, _block_onehot(seq))
        o_win = _band_attn(qa, k_w, vt_w, NSA_WINDOW)
        o_b = _band_attn(qb, k_b, vt_b, SWA_WINDOW, sinks[l].astype(jnp.float32))

        x = _merge_out(x, mod3, _prep_gate_weight(w_in[l]), _gate_expand_matrix(),
                       o_cmp, o_sel, o_win, o_b,
                       _pair_o_rows(w_up_a[l]).astype(jnp.bfloat16), _pair_o_rows(w_up_b[l]).astype(jnp.bfloat16),
                       w_out[l].astype(jnp.bfloat16), ln_g[l].reshape(1, D_MODEL), ln_b[l].reshape(1, D_MODEL))
    return x
```

```python
import functools

import numpy as np
import jax
import jax.numpy as jnp
from jax import lax
from jax.experimental import pallas as pl
from jax.experimental.pallas import tpu as pltpu

D_MODEL = 1024
HEAD_DIM = 64
HALF = HEAD_DIM // 2
N_HEADS = 8
N_KV = 2
GROUP = N_HEADS // N_KV
Q_WIDTH = N_HEADS * HEAD_DIM
KV_WIDTH = N_KV * HEAD_DIM
CMP_LEN = 32
CMP_STRIDE = 16
CMP_HIDDEN = 256
SEL_BLOCK = 64
SEL_TOPN = 8
NSA_WINDOW = 512
SWA_WINDOW = 128
ROPE_THETA = 10000.0
LN_EPS = 1e-5
NEG_INF = -1e30
FORCE_SCORE = 1e9
DEPTH = 1
DEEPNORM_ALPHA = (2 * DEPTH) ** 0.25
LANES = 128
SUBLANES = 8
BF16_SUBLANES = 16
ROPE_PACK = LANES // HALF
QKV_ROWS = 256
LOG2E = 1.4426950408889634
G_ROWS = HEAD_DIM + BF16_SUBLANES
V_ROWS = N_KV * G_ROWS
VMEM_LIMIT = 56 * 1024 * 1024

_SPLITS = [Q_WIDTH, 2 * KV_WIDTH, 2 * KV_WIDTH, 2 * KV_WIDTH, 3 * N_HEADS, Q_WIDTH,
           Q_WIDTH, 2 * KV_WIDTH, Q_WIDTH, 2 * D_MODEL]
_OFFS = [int(o) for o in np.cumsum([0] + _SPLITS)]
(OFF_QA, OFF_KVC, OFF_KVS, OFF_KVW, OFF_GN, OFF_ZA, OFF_QB, OFF_KVB, OFF_ZB, OFF_GM) = _OFFS[:10]


def _q_pair_perm():
    cols = []
    for j in range(GROUP):
        for half in range(2):
            for grp in range(N_KV):
                h = j + GROUP * grp
                cols += [h * HEAD_DIM + half * HALF + i for i in range(HALF)]
    return np.asarray(cols, np.int32)


def _k_pair_perm():
    cols = []
    for half in range(2):
        for grp in range(N_KV):
            cols += [grp * HEAD_DIM + half * HALF + i for i in range(HALF)]
    return np.asarray(cols, np.int32)


def _o_pair_perm():
    cols = []
    for j in range(GROUP):
        for grp in range(N_KV):
            h = j + GROUP * grp
            cols += [h * HEAD_DIM + d for d in range(HEAD_DIM)]
    return np.asarray(cols, np.int32)


Q_PERM = _q_pair_perm()
K_PERM = _k_pair_perm()
O_PERM = _o_pair_perm()


def _cparams(sem):
    return pltpu.CompilerParams(dimension_semantics=sem, vmem_limit_bytes=VMEM_LIMIT)


def _dot(a, b):
    return jnp.dot(a, b, preferred_element_type=jnp.float32)


def _dot_nt(a, b):
    return lax.dot_general(a, b, (((1,), (1,)), ((), ())), preferred_element_type=jnp.float32)


def _layer_norm(x):
    mu = jnp.mean(x, axis=-1, keepdims=True)
    xc = x - mu
    var = jnp.mean(xc * xc, axis=-1, keepdims=True)
    return xc * lax.rsqrt(var + LN_EPS)


def _sigmoid(x):
    return 1.0 / (1.0 + jnp.exp(-x))


def _adaln_kernel(c_ref, w_ref, b_ref, o_ref):
    c = c_ref[...]
    a = c * _sigmoid(c)
    o_ref[...] = jnp.dot(a, w_ref[...], preferred_element_type=jnp.float32,
                         precision=lax.Precision.HIGHEST) + b_ref[...]


def _adaln_mod(c, w_ada, b_ada):
    bsz = c.shape[0]
    n = w_ada.shape[1]
    tn = 512
    return pl.pallas_call(
        _adaln_kernel,
        out_shape=jax.ShapeDtypeStruct((bsz, n), jnp.float32),
        grid=(n // tn,),
        in_specs=[pl.BlockSpec((bsz, D_MODEL), lambda j: (0, 0)),
                  pl.BlockSpec((D_MODEL, tn), lambda j: (0, j)),
                  pl.BlockSpec((1, tn), lambda j: (0, j))],
        out_specs=pl.BlockSpec((bsz, tn), lambda j: (0, j)),
        compiler_params=_cparams(("arbitrary",)),
        name="adaln_mod",
    )(c, w_ada, b_ada.reshape(1, n))


def _rope_table_kernel(pos_ref, invf_ref, cos_ref, sin_ref):
    pos = pos_ref[...]
    rows = pos.shape[0]
    ang = jnp.concatenate([jnp.broadcast_to(pos[:, k:k + 1], (rows, HALF)) for k in range(ROPE_PACK)],
                          axis=1) * invf_ref[...]
    cos_ref[...] = jnp.cos(ang)
    sin_ref[...] = jnp.sin(ang)


def _rope_tables(positions, ts):
    bsz, seq = positions.shape
    sub = ts // ROPE_PACK
    rows = bsz * seq // ROPE_PACK
    pos = positions.astype(jnp.float32).reshape(bsz, seq // ts, ROPE_PACK, sub).transpose(0, 1, 3, 2)
    pos = pos.reshape(rows, ROPE_PACK)
    inv_freq = ROPE_THETA ** (-jnp.arange(HALF, dtype=jnp.float32) / HALF)
    invf = jnp.broadcast_to(inv_freq[None, :], (ROPE_PACK, HALF)).reshape(1, LANES)
    tr = seq // ROPE_PACK
    cos, sin = pl.pallas_call(
        _rope_table_kernel,
        out_shape=[jax.ShapeDtypeStruct((rows, LANES), jnp.float32)] * 2,
        grid=(rows // tr,),
        in_specs=[pl.BlockSpec((tr, ROPE_PACK), lambda i: (i, 0)),
                  pl.BlockSpec((1, LANES), lambda i: (0, 0))],
        out_specs=[pl.BlockSpec((tr, LANES), lambda i: (i, 0))] * 2,
        compiler_params=_cparams(("arbitrary",)),
        name="rope_table",
    )(pos, invf)
    return cos.reshape(bsz, tr, LANES), sin.reshape(bsz, tr, LANES)


N_QK_SLABS = 13
N_VT = 3


def _qkv_kernel(x_ref, mod_ref, w_ref, wvt_ref, cos_ref, sin_ref,
                qa_ref, kc_ref, ks_ref, kw_ref, qb_ref, kb_ref, vc_ref,
                vst_ref, vwt_ref, vbt_ref, kstage_ref, vstage_ref, *, ts, n_sub):
    shift = mod_ref[0, 0:1, :]
    scale = mod_ref[0, 1:2, :]

    def project(r):
        u = (_layer_norm(x_ref[0, r * ts:(r + 1) * ts, :]) * (1.0 + scale) + shift).astype(jnp.bfloat16)
        return _dot(u, w_ref[...]), _dot_nt(wvt_ref[...], u)

    def finish(r, acc, vt):
        rows = slice(r * ts, (r + 1) * ts)
        sub = ts // ROPE_PACK
        for k in range(ROPE_PACK):
            c32 = cos_ref[0, r * sub:(r + 1) * sub, k * HALF:(k + 1) * HALF]
            s32 = sin_ref[0, r * sub:(r + 1) * sub, k * HALF:(k + 1) * HALF]
            cos = jnp.concatenate([c32, c32, c32, c32], axis=1)
            sin = jnp.concatenate([-s32, -s32, s32, s32], axis=1)
            tok = slice(r * ts + k * sub, r * ts + (k + 1) * sub)

            def rope(slab):
                t = acc[k * sub:(k + 1) * sub, slab * LANES:(slab + 1) * LANES]
                return t * cos + pltpu.roll(t, LANES // 2, 1) * sin

            for j in range(4):
                qa_ref[0, tok, j * LANES:(j + 1) * LANES] = (rope(j) * LOG2E).astype(jnp.bfloat16)
                qb_ref[0, tok, j * LANES:(j + 1) * LANES] = (rope(7 + j) * LOG2E).astype(jnp.bfloat16)
            for slab, ref in ((5, ks_ref), (6, kw_ref), (11, kb_ref)):
                ref[0, tok, :] = rope(slab).astype(jnp.bfloat16)
            kstage_ref[k * sub:(k + 1) * sub, :] = rope(4)
        vstage_ref[...] = acc[:, 12 * LANES:13 * LANES]
        n_ch = ts // CMP_STRIDE
        for l in range(CMP_STRIDE):
            for stage, ref in ((kstage_ref, kc_ref), (vstage_ref, vc_ref)):
                ref[0, r * n_ch:(r + 1) * n_ch, l * LANES:(l + 1) * LANES] = (
                    stage[pl.ds(l, n_ch, stride=CMP_STRIDE), :].astype(jnp.bfloat16))
        ones = jnp.ones((G_ROWS - HEAD_DIM, ts), jnp.bfloat16)
        for j, ref in enumerate((vst_ref, vwt_ref, vbt_ref)):
            for g in range(N_KV):
                lo = j * KV_WIDTH + g * HEAD_DIM
                ref[0, g * G_ROWS:g * G_ROWS + HEAD_DIM, rows] = vt[lo:lo + HEAD_DIM].astype(jnp.bfloat16)
                ref[0, g * G_ROWS + HEAD_DIM:(g + 1) * G_ROWS, rows] = ones

    nxt = project(0)
    for r in range(n_sub):
        cur = nxt
        if r + 1 < n_sub:
            nxt = project(r + 1)
        finish(r, *cur)


def _qkv_proj(x, mod3, w_qk, w_vt, cos_t, sin_t, ts=QKV_ROWS, n_sub=8):
    bsz, seq, _ = x.shape
    tm = ts * n_sub
    chunk_w = CMP_STRIDE * KV_WIDTH
    q_shape = jax.ShapeDtypeStruct((bsz, seq, Q_WIDTH), jnp.bfloat16)
    k_shape = jax.ShapeDtypeStruct((bsz, seq, KV_WIDTH), jnp.bfloat16)
    c_shape = jax.ShapeDtypeStruct((bsz, seq // CMP_STRIDE, chunk_w), jnp.bfloat16)
    vt_shape = jax.ShapeDtypeStruct((bsz, V_ROWS, seq), jnp.bfloat16)
    q_spec = pl.BlockSpec((1, tm, Q_WIDTH), lambda b, i: (b, i, 0))
    k_spec = pl.BlockSpec((1, tm, KV_WIDTH), lambda b, i: (b, i, 0))
    c_spec = pl.BlockSpec((1, tm // CMP_STRIDE, chunk_w), lambda b, i: (b, i, 0))
    t_spec = pl.BlockSpec((1, tm // ROPE_PACK, LANES), lambda b, i: (b, i, 0))
    vt_spec = pl.BlockSpec((1, V_ROWS, tm), lambda b, i: (b, 0, i))
    return pl.pallas_call(
        functools.partial(_qkv_kernel, ts=ts, n_sub=n_sub),
        out_shape=[q_shape, c_shape, k_shape, k_shape, q_shape, k_shape, c_shape,
                   vt_shape, vt_shape, vt_shape],
        grid=(bsz, seq // tm),
        in_specs=[pl.BlockSpec((1, tm, D_MODEL), lambda b, i: (b, i, 0)),
                  pl.BlockSpec((1, 3, D_MODEL), lambda b, i: (b, 0, 0)),
                  pl.BlockSpec((D_MODEL, N_QK_SLABS * LANES), lambda b, i: (0, 0)),
                  pl.BlockSpec((N_VT * KV_WIDTH, D_MODEL), lambda b, i: (0, 0)),
                  t_spec, t_spec],
        out_specs=[q_spec, c_spec, k_spec, k_spec, q_spec, k_spec, c_spec,
                   vt_spec, vt_spec, vt_spec],
        scratch_shapes=[pltpu.VMEM((ts, KV_WIDTH), jnp.float32), pltpu.VMEM((ts, KV_WIDTH), jnp.float32)],
        compiler_params=_cparams(("arbitrary", "arbitrary")),
        name="qkv_proj",
    )(x, mod3, w_qk, w_vt, cos_t, sin_t)


def _compress_kernel(xk_ref, xv_ref, w1k_ref, w1v_ref, posk_ref, posv_ref, w2k_ref, w2vt_ref,
                     kc_ref, vct_ref):
    bb, n_chunks, cw = xk_ref.shape

    def hidden(x_ref, w1_ref, pos_ref):
        hw = w1_ref.shape[1] // 2
        p = _dot(x_ref[...].reshape(bb * n_chunks, cw), w1_ref[...])
        pt = _dot(pos_ref[...], w1_ref[...])
        pos_term = pt[0:1, 0:hw] + pt[1:2, hw:]
        first = p[:, 0:hw]
        second = pltpu.roll(p[:, hw:], bb * n_chunks - 1, 0)
        hid = first + second + pos_term
        return (hid * _sigmoid(hid)).astype(jnp.bfloat16)

    kc = _dot(hidden(xk_ref, w1k_ref, posk_ref), w2k_ref[...]).astype(jnp.bfloat16)
    vct = _dot_nt(w2vt_ref[...], hidden(xv_ref, w1v_ref, posv_ref)).astype(jnp.bfloat16)
    for i in range(bb):
        kc_ref[i] = kc[i * n_chunks:(i + 1) * n_chunks]
        vct_ref[i] = vct[:, i * n_chunks:(i + 1) * n_chunks]


def _compress(xk, xv, w1k, w1v, posk, posv, w2k, w2vt, bb=4):
    bsz, n_chunks, cw = xk.shape
    hw2 = w1k.shape[1]
    out = jax.ShapeDtypeStruct((bsz, n_chunks, LANES), jnp.bfloat16)
    x_spec = pl.BlockSpec((bb, n_chunks, cw), lambda b: (b, 0, 0))
    w1_spec = pl.BlockSpec((cw, hw2), lambda b: (0, 0))
    pos_spec = pl.BlockSpec((8, cw), lambda b: (0, 0))
    o_spec = pl.BlockSpec((bb, n_chunks, LANES), lambda b: (b, 0, 0))
    return pl.pallas_call(
        _compress_kernel,
        out_shape=[out, out],
        grid=(bsz // bb,),
        in_specs=[x_spec, x_spec, w1_spec, w1_spec, pos_spec, pos_spec,
                  pl.BlockSpec(w2k.shape, lambda b: (0, 0)), pl.BlockSpec(w2vt.shape, lambda b: (0, 0))],
        out_specs=[o_spec, o_spec],
        compiler_params=_cparams(("arbitrary",)),
        name="compress",
    )(xk, xv, w1k, w1v, posk, posv, w2k, w2vt)


def _stack_heads(q):
    lane = lax.broadcasted_iota(jnp.int32, (1, LANES), 1)
    is_a = (lane & (HEAD_DIM - 1)) < HALF
    zero = jnp.zeros((), q.dtype)
    tiles = [q[:, j * LANES:(j + 1) * LANES] for j in range(GROUP)]
    parts = [jnp.where(is_a, t, zero) for t in tiles] + [jnp.where(is_a, zero, t) for t in tiles]
    return jnp.concatenate(parts, axis=0)


def _store_heads(o_ref, rows, o_g, inv_l, tq):
    for j in range(GROUP):
        cols = slice(j * tq, (j + 1) * tq)
        halves = [o_g[g][0:HEAD_DIM, cols] for g in range(N_KV)]
        if inv_l is not None:
            halves = [h * inv_l[g][:, cols] for g, h in enumerate(halves)]
        tile_t = jnp.concatenate(halves, axis=0)
        o_ref[0, rows, j * LANES:(j + 1) * LANES] = tile_t.T.astype(jnp.bfloat16)


def _group_cols(g, tq):
    return slice(g * GROUP * tq, (g + 1) * GROUP * tq)


def _cmp_kernel(q_ref, kc_ref, vct_ref, ovt_ref, o_ref, bias_ref, *, tq, n_cmp, n_blk):
    qi = pl.program_id(1)
    nq = tq // LANES
    qs = _stack_heads(q_ref[0])
    c_row = lax.broadcasted_iota(jnp.int32, (LANES, tq), 0)
    t_col = qi * tq + lax.broadcasted_iota(jnp.int32, (LANES, tq), 1)
    vis = (c_row * CMP_STRIDE + (CMP_LEN - 1) <= t_col) & (c_row < n_cmp)
    any_vis = (t_col[0:1, :] >= CMP_LEN - 1).astype(jnp.float32)
    s_t = _dot_nt(kc_ref[0], qs)
    p_parts = []
    for cb in range(N_HEADS * nq):
        qq = slice((cb % nq) * LANES, (cb % nq + 1) * LANES)
        s = jnp.where(vis[:, qq], s_t[:, cb * LANES:(cb + 1) * LANES], NEG_INF)
        e = jnp.exp2(s - jnp.max(s, axis=0, keepdims=True))
        p_parts.append(e * (any_vis[:, qq] / jnp.sum(e, axis=0, keepdims=True)))
    p_t = jnp.concatenate(p_parts, axis=1)
    o_g = [_dot(vct_ref[0, g * HEAD_DIM:(g + 1) * HEAD_DIM, :], p_t[:, _group_cols(g, tq)].astype(jnp.bfloat16))
           for g in range(N_KV)]
    _store_heads(o_ref, slice(None), o_g, None, tq)

    j_blk = lax.broadcasted_iota(jnp.int32, (n_blk, tq), 0)
    t_q = qi * tq + lax.broadcasted_iota(jnp.int32, (n_blk, tq), 1)
    cur = jnp.right_shift(t_q, SEL_BLOCK.bit_length() - 1)
    forced = (j_blk == 0) | (j_blk == cur) | (j_blk == cur - 1)
    causal = j_blk <= cur
    n_free = SEL_TOPN - 3
    take_all = cur < SEL_TOPN
    candidate = causal & jnp.logical_not(forced)
    rows = []
    for g in range(N_KV):
        pg = p_t[:, g * GROUP * tq:(g * GROUP + 1) * tq]
        for h in range(1, GROUP):
            pg = pg + p_t[:, (g * GROUP + h) * tq:(g * GROUP + h + 1) * tq]
        hi = pg.astype(jnp.bfloat16)
        lo = (pg - hi.astype(jnp.float32)).astype(jnp.bfloat16)
        imp_t = _dot(ovt_ref[...], hi) + _dot(ovt_ref[...], lo)
        work = jnp.where(candidate, imp_t[0:n_blk, :], NEG_INF)
        picked = jnp.zeros((n_blk, tq), jnp.int32)
        for _ in range(n_free):
            top = jnp.max(work, axis=0, keepdims=True)
            first = jnp.min(jnp.where(work == top, j_blk, n_blk), axis=0, keepdims=True)
            hit = j_blk == first
            picked = jnp.where(hit, 1, picked)
            work = jnp.where(hit, NEG_INF, work)
        keep = causal & (take_all | forced | ((picked > 0) & candidate))
        rows.append(jnp.where(keep, 0.0, NEG_INF))
    rows.append(jnp.zeros((LANES - N_KV * n_blk, tq), jnp.float32))
    bias_t = jnp.concatenate(rows, axis=0)
    bias_ref[0] = bias_t.T.astype(jnp.bfloat16)


def _cmp_attn(qa, kc, vct, ovt, n_cmp, tq=2048):
    bsz, seq, _ = qa.shape
    n_blk = seq // SEL_BLOCK
    kern = functools.partial(_cmp_kernel, tq=tq, n_cmp=n_cmp, n_blk=n_blk)
    kv_spec = pl.BlockSpec((1, LANES, LANES), lambda b, i: (b, 0, 0))
    return pl.pallas_call(
        kern,
        out_shape=[jax.ShapeDtypeStruct((bsz, seq, Q_WIDTH), jnp.bfloat16),
                   jax.ShapeDtypeStruct((bsz, seq, LANES), jnp.bfloat16)],
        grid=(bsz, seq // tq),
        in_specs=[pl.BlockSpec((1, tq, Q_WIDTH), lambda b, i: (b, i, 0)),
                  kv_spec, kv_spec,
                  pl.BlockSpec((LANES, LANES), lambda b, i: (0, 0))],
        out_specs=[pl.BlockSpec((1, tq, Q_WIDTH), lambda b, i: (b, i, 0)),
                   pl.BlockSpec((1, tq, LANES), lambda b, i: (b, i, 0))],
        compiler_params=_cparams(("arbitrary", "arbitrary")),
        name="cmp_attn",
    )(qa, kc, vct, ovt)


def _sel_kernel(q_ref, bias_ref, k_ref, vt_ref, blk_ref, o_ref, kaug_ref, qaug_ref, s0_ref, s1_ref, acc_ref, m_ref,
                *, tq, tk):
    step = pl.program_id(1)
    nq = tq // LANES
    nk = tk // LANES
    per_g = GROUP * nq

    @pl.when(step == 0)
    def _():
        kaug_ref[:, 0:LANES] = k_ref[0]
        kaug_ref[:, LANES:2 * LANES] = blk_ref[...]

    lane = lax.broadcasted_iota(jnp.int32, (1, LANES), 1)
    n_blk = blk_ref.shape[0] // SEL_BLOCK
    bt = bias_ref[0]
    zero = jnp.zeros((), bt.dtype)
    b0 = jnp.where(lane < n_blk, bt, zero)
    b1 = jnp.where((lane >= n_blk) & (lane < 2 * n_blk), bt, zero)
    qaug_ref[:, 0:LANES] = _stack_heads(q_ref[0])
    qaug_ref[:, LANES:2 * LANES] = jnp.concatenate([b0] * GROUP + [b1] * GROUP, axis=0)

    acc_ref[...] = jnp.zeros_like(acc_ref)
    m_ref[...] = jnp.full_like(m_ref, NEG_INF)
    row = lax.broadcasted_iota(jnp.int32, (tk, tk), 0)
    col = lax.broadcasted_iota(jnp.int32, (tk, tk), 1)
    causal_bias = jnp.where(row <= col, 0.0, NEG_INF)

    def scores(c, s_ref):
        k0 = pl.multiple_of(c * tk, tk)
        s_ref[...] = _dot_nt(kaug_ref[pl.ds(k0, tk), :], qaug_ref[...])

    def accumulate(c, s_ref, diag_half):
        k0 = pl.multiple_of(c * tk, tk)
        for g in range(N_KV):
            ps, alphas = [], []
            for i in range(per_g):
                cb = g * per_g + i
                half, within = divmod(cb % nq, nk)
                cols = slice(cb * LANES, (cb + 1) * LANES)
                if diag_half is not None and half < diag_half:
                    ps.append(jnp.zeros((tk, LANES), jnp.bfloat16))
                    alphas.append(jnp.ones((1, LANES), jnp.float32))
                    continue
                s = s_ref[:, cols]
                if diag_half is not None and half == diag_half:
                    s = s + causal_bias[:, within * LANES:(within + 1) * LANES]
                m_old = m_ref[0:1, cols]
                m_new = jnp.maximum(m_old, jnp.max(s, axis=0, keepdims=True))
                m_ref[0:1, cols] = m_new
                alphas.append(jnp.exp2(m_old - m_new))
                ps.append(jnp.exp2(s - m_new).astype(jnp.bfloat16))
            vt_g = vt_ref[0, g * G_ROWS:(g + 1) * G_ROWS, pl.ds(k0, tk)]
            pv = _dot(vt_g, jnp.concatenate(ps, axis=1))
            acc_ref[g] = jnp.concatenate(alphas, axis=1) * acc_ref[g] + pv

    scores(0, s0_ref)

    def body(i, carry):
        c = 2 * i
        scores(c + 1, s1_ref)
        accumulate(c, s0_ref, None)
        scores(c + 2, s0_ref)
        accumulate(c + 1, s1_ref, None)
        return carry

    lax.fori_loop(0, step, body, 0)
    c0 = 2 * step
    scores(c0 + 1, s1_ref)
    accumulate(c0, s0_ref, 0)
    accumulate(c0 + 1, s1_ref, 1)

    o_g = [acc_ref[g] for g in range(N_KV)]
    inv_l = [1.0 / o[HEAD_DIM:HEAD_DIM + 1, :] for o in o_g]
    _store_heads(o_ref, slice(None), o_g, inv_l, tq)


def _sel_attn(qa, bias, ks, vst, blk_onehot, tk=256):
    bsz, seq, _ = qa.shape
    tq = 2 * tk
    kern = functools.partial(_sel_kernel, tq=tq, tk=tk)
    return pl.pallas_call(
        kern,
        out_shape=jax.ShapeDtypeStruct((bsz, seq, Q_WIDTH), jnp.bfloat16),
        grid=(bsz, seq // tq),
        in_specs=[pl.BlockSpec((1, tq, Q_WIDTH), lambda b, i: (b, i, 0)),
                  pl.BlockSpec((1, tq, LANES), lambda b, i: (b, i, 0)),
                  pl.BlockSpec((1, seq, LANES), lambda b, i: (b, 0, 0)),
                  pl.BlockSpec((1, V_ROWS, seq), lambda b, i: (b, 0, 0)),
                  pl.BlockSpec((seq, LANES), lambda b, i: (0, 0))],
        out_specs=pl.BlockSpec((1, tq, Q_WIDTH), lambda b, i: (b, i, 0)),
        scratch_shapes=[pltpu.VMEM((seq, 2 * LANES), jnp.bfloat16),
                        pltpu.VMEM((N_HEADS * tq, 2 * LANES), jnp.bfloat16),
                        pltpu.VMEM((tk, N_HEADS * tq), jnp.float32),
                        pltpu.VMEM((tk, N_HEADS * tq), jnp.float32),
                        pltpu.VMEM((N_KV, G_ROWS, GROUP * tq), jnp.float32),
                        pltpu.VMEM((8, N_HEADS * tq), jnp.float32)],
        compiler_params=_cparams(("arbitrary", "arbitrary")),
        name="sel_attn",
    )(qa, bias, ks, vst, blk_onehot)


def _band_kernel(*refs, tq, n_sub, span, window, use_sink):
    if use_sink:
        q_ref, k_ref, vt_ref, sink_ref, o_ref = refs
    else:
        q_ref, k_ref, vt_ref, o_ref = refs
    step = pl.program_id(1)
    seq = k_ref.shape[1]
    nq = tq // LANES
    row = lax.broadcasted_iota(jnp.int32, (span, tq), 0)
    col = lax.broadcasted_iota(jnp.int32, (span, tq), 1)
    diff = col - row

    def scores(sub):
        q0 = (step * n_sub + sub) * tq
        start = pl.multiple_of(jnp.clip(q0 + tq - span, 0, seq - span), LANES)
        rel = diff + (q0 - start)
        bias = jnp.where((rel >= 0) & (rel < window), 0.0, NEG_INF)
        qs = _stack_heads(q_ref[0, sub * tq:(sub + 1) * tq, :])
        return start, bias, _dot_nt(k_ref[0, pl.ds(start, span), :], qs)

    def finish(sub, start, bias, s_t):
        ps, extra = [], []
        for cb in range(N_HEADS * nq):
            s = s_t[:, cb * LANES:(cb + 1) * LANES] + bias[:, (cb % nq) * LANES:(cb % nq + 1) * LANES]
            m = jnp.max(s, axis=0, keepdims=True)
            if use_sink:
                sink = sink_ref[cb // nq] * LOG2E
                m = jnp.maximum(m, sink)
                extra.append(jnp.exp2(sink - m))
            ps.append(jnp.exp2(s - m).astype(jnp.bfloat16))
        o_g, inv_l = [], []
        for g in range(N_KV):
            per_g = GROUP * nq
            vt_g = vt_ref[0, g * G_ROWS:(g + 1) * G_ROWS, pl.ds(start, span)]
            o = _dot(vt_g, jnp.concatenate(ps[g * per_g:(g + 1) * per_g], axis=1))
            l = o[HEAD_DIM:HEAD_DIM + 1, :]
            if use_sink:
                l = l + jnp.concatenate(extra[g * per_g:(g + 1) * per_g], axis=1)
            o_g.append(o)
            inv_l.append(1.0 / l)
        _store_heads(o_ref, slice(sub * tq, (sub + 1) * tq), o_g, inv_l, tq)

    nxt = scores(0)
    for sub in range(n_sub):
        cur = nxt
        if sub + 1 < n_sub:
            nxt = scores(sub + 1)
        finish(sub, *cur)


def _band_attn(q, k, vt, window, sinks=None, tq=128, n_sub=16):
    bsz, seq, _ = q.shape
    span = (-(-(window - 1) // tq)) * tq + tq
    use_sink = sinks is not None
    kern = functools.partial(_band_kernel, tq=tq, n_sub=n_sub, span=span, window=window, use_sink=use_sink)
    ts = tq * n_sub
    in_specs = [pl.BlockSpec((1, ts, Q_WIDTH), lambda b, i: (b, i, 0)),
                pl.BlockSpec((1, seq, LANES), lambda b, i: (b, 0, 0)),
                pl.BlockSpec((1, V_ROWS, seq), lambda b, i: (b, 0, 0))]
    args = [q, k, vt]
    if use_sink:
        in_specs.append(pl.BlockSpec(memory_space=pltpu.SMEM))
        args.append(sinks)
    return pl.pallas_call(
        kern,
        out_shape=jax.ShapeDtypeStruct((bsz, seq, Q_WIDTH), jnp.bfloat16),
        grid=(bsz, seq // ts),
        in_specs=in_specs,
        out_specs=pl.BlockSpec((1, ts, Q_WIDTH), lambda b, i: (b, i, 0)),
        compiler_params=_cparams(("arbitrary", "arbitrary")),
        name="band_attn_sink" if use_sink else "band_attn",
    )(*args)


def _merge_kernel(x_ref, mod_ref, wgn_ref, wza_ref, wzb_ref, wgm_ref, gexp_ref, ocmp_ref, osel_ref, owin_ref, ob_ref,
                  wua_ref, wub_ref, wo_ref, lng_ref, lnb_ref, o_ref, *, ts, n_sub):
    shift = mod_ref[0, 0:1, :]
    scale = mod_ref[0, 1:2, :]
    gate = mod_ref[0, 2:3, :]

    def project(r):
        u = (_layer_norm(x_ref[0, r * ts:(r + 1) * ts, :]) * (1.0 + scale) + shift).astype(jnp.bfloat16)
        return _dot(u, wgn_ref[...]), _dot(u, wza_ref[...]), _dot(u, wzb_ref[...]), _dot(u, wgm_ref[...])

    def finish(r, g_nsa, z_a, z_b, g_m):
        rows = slice(r * ts, (r + 1) * ts)
        sig = _sigmoid(g_nsa)
        hi = sig.astype(jnp.bfloat16)
        lo = (sig - hi.astype(jnp.float32)).astype(jnp.bfloat16)
        gx = _dot(jnp.concatenate([hi, lo], axis=1), gexp_ref[...])
        o_a = (gx[:, 0:Q_WIDTH] * ocmp_ref[0, rows, :].astype(jnp.float32)
               + gx[:, Q_WIDTH:2 * Q_WIDTH] * osel_ref[0, rows, :].astype(jnp.float32)
               + gx[:, 2 * Q_WIDTH:3 * Q_WIDTH] * owin_ref[0, rows, :].astype(jnp.float32))
        y_a = (o_a * (z_a * _sigmoid(z_a))).astype(jnp.bfloat16)
        y_b = (ob_ref[0, rows, :].astype(jnp.float32) * (z_b * _sigmoid(z_b))).astype(jnp.bfloat16)
        merged = _sigmoid(g_m[:, 0:D_MODEL]) * _dot(y_a, wua_ref[...])
        merged = merged + _sigmoid(g_m[:, D_MODEL:2 * D_MODEL]) * _dot(y_b, wub_ref[...])
        out = _dot(merged.astype(jnp.bfloat16), wo_ref[...])
        y = DEEPNORM_ALPHA * x_ref[0, rows, :] + gate * out
        o_ref[0, rows, :] = _layer_norm(y) * lng_ref[...] + lnb_ref[...]

    nxt = project(0)
    for r in range(n_sub):
        cur = nxt
        if r + 1 < n_sub:
            nxt = project(r + 1)
        finish(r, *cur)


def _merge_out(x, mod3, wgs, gexp, o_cmp, o_sel, o_win, o_b, wua, wub, wo, ln_g, ln_b, ts=256, n_sub=4):
    bsz, seq, _ = x.shape
    tm = ts * n_sub
    tok = lambda w: pl.BlockSpec((1, tm, w), lambda b, i: (b, i, 0))
    full = lambda a: pl.BlockSpec(a.shape, lambda b, i: (0,) * a.ndim)
    return pl.pallas_call(
        functools.partial(_merge_kernel, ts=ts, n_sub=n_sub),
        out_shape=jax.ShapeDtypeStruct(x.shape, jnp.float32),
        grid=(bsz, seq // tm),
        in_specs=[tok(D_MODEL), pl.BlockSpec((1, 3, D_MODEL), lambda b, i: (b, 0, 0)),
                  *[full(w) for w in wgs], full(gexp), tok(Q_WIDTH), tok(Q_WIDTH), tok(Q_WIDTH), tok(Q_WIDTH),
                  full(wua), full(wub), full(wo), full(ln_g), full(ln_b)],
        out_specs=tok(D_MODEL),
        compiler_params=_cparams(("arbitrary", "arbitrary")),
        name="merge_out",
    )(x, mod3, *wgs, gexp, o_cmp, o_sel, o_win, o_b, wua, wub, wo, ln_g, ln_b)


def _pair_q_cols(w):
    k = w.shape[0]
    return w.reshape(k, N_KV, GROUP, 2, HALF).transpose(0, 2, 3, 1, 4).reshape(k, Q_WIDTH)


def _pair_k_cols(w):
    k = w.shape[0]
    return w.reshape(k, N_KV, 2, HALF).transpose(0, 2, 1, 3).reshape(k, KV_WIDTH)


def _pair_o_cols(w):
    k = w.shape[0]
    return w.reshape(k, N_KV, GROUP, HEAD_DIM).transpose(0, 2, 1, 3).reshape(k, Q_WIDTH)


def _pair_o_rows(w):
    n = w.shape[1]
    return w.reshape(N_KV, GROUP, HEAD_DIM, n).transpose(1, 0, 2, 3).reshape(Q_WIDTH, n)


def _prep_qkv_weight(w_in):
    q_scale = HEAD_DIM ** -0.5
    sec = lambda off, width: w_in[:, off:off + width]
    cols = [_pair_q_cols(sec(OFF_QA, Q_WIDTH)) * q_scale,
            _pair_k_cols(sec(OFF_KVC, KV_WIDTH)), _pair_k_cols(sec(OFF_KVS, KV_WIDTH)),
            _pair_k_cols(sec(OFF_KVW, KV_WIDTH)),
            _pair_q_cols(sec(OFF_QB, Q_WIDTH)) * q_scale,
            _pair_k_cols(sec(OFF_KVB, KV_WIDTH)),
            sec(OFF_KVC + KV_WIDTH, KV_WIDTH)]
    v_cols = [sec(OFF_KVS + KV_WIDTH, KV_WIDTH), sec(OFF_KVW + KV_WIDTH, KV_WIDTH),
              sec(OFF_KVB + KV_WIDTH, KV_WIDTH)]
    return (jnp.concatenate(cols, axis=1).astype(jnp.bfloat16),
            jnp.concatenate(v_cols, axis=1).T.astype(jnp.bfloat16))


def _prep_gate_weights(w_in):
    pad = jnp.zeros((D_MODEL, LANES - 3 * N_HEADS), w_in.dtype)
    ws = [jnp.concatenate([w_in[:, OFF_GN:OFF_GN + 3 * N_HEADS], pad], axis=1),
          _pair_o_cols(w_in[:, OFF_ZA:OFF_ZA + Q_WIDTH]), _pair_o_cols(w_in[:, OFF_ZB:OFF_ZB + Q_WIDTH]),
          w_in[:, OFF_GM:OFF_GM + 2 * D_MODEL]]
    return [w.astype(jnp.bfloat16) for w in ws]


def _gate_expand_matrix():
    e = np.zeros((LANES, 3 * Q_WIDTH), np.float32)
    for br in range(3):
        for c in range(Q_WIDTH):
            h = int(O_PERM[c]) // HEAD_DIM
            e[h * 3 + br, br * Q_WIDTH + c] = 1.0
    return jnp.asarray(np.concatenate([e, e], axis=0), jnp.bfloat16)


def _prep_compress(pos, w1, w2, paired):
    half_len = CMP_LEN // 2
    w = w1.reshape(CMP_LEN, HEAD_DIM, CMP_HIDDEN).astype(jnp.bfloat16)
    w2 = w2.astype(jnp.bfloat16)
    pieces = ([(g, slice(h * HALF, (h + 1) * HALF)) for h in range(2) for g in range(N_KV)] if paired
              else [(g, slice(0, HEAD_DIM)) for g in range(N_KV)])
    own = lambda g, blk: jnp.concatenate([blk if k == g else jnp.zeros_like(blk) for k in range(N_KV)], axis=-1)
    w1x = jnp.concatenate([own(g, w[:, d, :]) for g, d in pieces], axis=1)
    first = w1x[:half_len].reshape(half_len * LANES, N_KV * CMP_HIDDEN)
    second = w1x[half_len:].reshape(half_len * LANES, N_KV * CMP_HIDDEN)
    w1e = jnp.concatenate([first, second], axis=1)
    w2e = jnp.concatenate([jnp.concatenate([w2[:, d] if k == g else jnp.zeros_like(w2[:, d]) for g, d in pieces], axis=1)
                           for k in range(N_KV)], axis=0)
    pos_l = jnp.concatenate([pos[:, d] for _, d in pieces], axis=1)
    pos_rows = jnp.concatenate([pos_l[:half_len].reshape(1, -1), pos_l[half_len:].reshape(1, -1),
                                jnp.zeros((6, half_len * LANES), pos.dtype)], axis=0).astype(jnp.bfloat16)
    return w1e, pos_rows, w2e


def _overlap_t(n_cmp, n_blk):
    c0 = np.arange(n_cmp) * CMP_STRIDE
    j0 = np.arange(n_blk) * SEL_BLOCK
    overlap = (c0[:, None] < j0[None, :] + SEL_BLOCK) & (c0[:, None] + CMP_LEN > j0[None, :])
    ovt = np.zeros((LANES, LANES), np.float32)
    ovt[:n_blk, :n_cmp] = overlap.T
    return jnp.asarray(ovt, jnp.bfloat16)


def _block_onehot(seq):
    n_blk = seq // SEL_BLOCK
    e = np.zeros((seq, LANES), np.float32)
    pos = np.arange(seq)
    e[pos, pos // SEL_BLOCK] = 1.0
    e[pos, n_blk + pos // SEL_BLOCK] = 1.0
    return jnp.asarray(e, jnp.bfloat16)


def kernel(x, c, positions, w_ada, b_ada, w_in, cmp_pos_k, cmp_w1_k, cmp_w2_k,
           cmp_pos_v, cmp_w1_v, cmp_w2_v, sinks, w_up_a, w_up_b, w_out, ln_g, ln_b):
    bsz, seq, _ = x.shape
    n_cmp = (seq - CMP_LEN) // CMP_STRIDE + 1
    n_blk = seq // SEL_BLOCK
    assert seq // CMP_STRIDE == LANES and 2 * n_blk <= LANES
    cos_t, sin_t = _rope_tables(positions, QKV_ROWS)
    for l in range(DEPTH):
        mod3 = _adaln_mod(c, w_ada[l], b_ada[l]).reshape(bsz, 3, D_MODEL)
        w_qk, w_vt = _prep_qkv_weight(w_in[l])
        qa, k_c, k_s, k_w, qb, k_b, v_c, vt_s, vt_w, vt_b = _qkv_proj(x, mod3, w_qk, w_vt, cos_t, sin_t)

        w1k, posk, w2k = _prep_compress(cmp_pos_k[l], cmp_w1_k[l], cmp_w2_k[l], paired=True)
        w1v, posv, w2v = _prep_compress(cmp_pos_v[l], cmp_w1_v[l], cmp_w2_v[l], paired=False)
        kc, vct = _compress(k_c, v_c, w1k, w1v, posk, posv, w2k, w2v.T, bb=4 if bsz % 4 == 0 else 1)

        o_cmp, sel_bias = _cmp_attn(qa, kc, vct, _overlap_t(n_cmp, n_blk), n_cmp)
        o_sel = _sel_attn(qa, sel_bias, k_s, vt_s, _block_onehot(seq))
        o_win = _band_attn(qa, k_w, vt_w, NSA_WINDOW)
        o_b = _band_attn(qb, k_b, vt_b, SWA_WINDOW, sinks[l].astype(jnp.float32))

        x = _merge_out(x, mod3, _prep_gate_weights(w_in[l]), _gate_expand_matrix(),
                       o_cmp, o_sel, o_win, o_b,
                       _pair_o_rows(w_up_a[l]).astype(jnp.bfloat16), _pair_o_rows(w_up_b[l]).astype(jnp.bfloat16),
                       w_out[l].astype(jnp.bfloat16), ln_g[l].reshape(1, D_MODEL), ln_b[l].reshape(1, D_MODEL))
    return x
```

```python
import functools

import numpy as np
import jax
import jax.numpy as jnp
from jax import lax
from jax.experimental import pallas as pl
from jax.experimental.pallas import tpu as pltpu

D_MODEL = 1024
HEAD_DIM = 64
HALF = HEAD_DIM // 2
N_HEADS = 8
N_KV = 2
GROUP = N_HEADS // N_KV
Q_WIDTH = N_HEADS * HEAD_DIM
KV_WIDTH = N_KV * HEAD_DIM
CMP_LEN = 32
CMP_STRIDE = 16
CMP_HIDDEN = 256
SEL_BLOCK = 64
SEL_TOPN = 8
NSA_WINDOW = 512
SWA_WINDOW = 128
ROPE_THETA = 10000.0
LN_EPS = 1e-5
NEG_INF = -1e30
FORCE_SCORE = 1e9
DEPTH = 1
DEEPNORM_ALPHA = (2 * DEPTH) ** 0.25
LANES = 128
SUBLANES = 8
BF16_SUBLANES = 16
ROPE_PACK = LANES // HALF
QKV_ROWS = 256
LOG2E = 1.4426950408889634
G_ROWS = HEAD_DIM + BF16_SUBLANES
V_ROWS = N_KV * G_ROWS
VMEM_LIMIT = 56 * 1024 * 1024

_SPLITS = [Q_WIDTH, 2 * KV_WIDTH, 2 * KV_WIDTH, 2 * KV_WIDTH, 3 * N_HEADS, Q_WIDTH,
           Q_WIDTH, 2 * KV_WIDTH, Q_WIDTH, 2 * D_MODEL]
_OFFS = [int(o) for o in np.cumsum([0] + _SPLITS)]
(OFF_QA, OFF_KVC, OFF_KVS, OFF_KVW, OFF_GN, OFF_ZA, OFF_QB, OFF_KVB, OFF_ZB, OFF_GM) = _OFFS[:10]


def _q_pair_perm():
    cols = []
    for j in range(GROUP):
        for half in range(2):
            for grp in range(N_KV):
                h = j + GROUP * grp
                cols += [h * HEAD_DIM + half * HALF + i for i in range(HALF)]
    return np.asarray(cols, np.int32)


def _k_pair_perm():
    cols = []
    for half in range(2):
        for grp in range(N_KV):
            cols += [grp * HEAD_DIM + half * HALF + i for i in range(HALF)]
    return np.asarray(cols, np.int32)


def _o_pair_perm():
    cols = []
    for j in range(GROUP):
        for grp in range(N_KV):
            h = j + GROUP * grp
            cols += [h * HEAD_DIM + d for d in range(HEAD_DIM)]
    return np.asarray(cols, np.int32)


Q_PERM = _q_pair_perm()
K_PERM = _k_pair_perm()
O_PERM = _o_pair_perm()


def _cparams(sem):
    return pltpu.CompilerParams(dimension_semantics=sem, vmem_limit_bytes=VMEM_LIMIT)


def _dot(a, b):
    return jnp.dot(a, b, preferred_element_type=jnp.float32)


def _dot_nt(a, b):
    return lax.dot_general(a, b, (((1,), (1,)), ((), ())), preferred_element_type=jnp.float32)


def _layer_norm(x):
    mu = jnp.mean(x, axis=-1, keepdims=True)
    xc = x - mu
    var = jnp.mean(xc * xc, axis=-1, keepdims=True)
    return xc * lax.rsqrt(var + LN_EPS)


def _sigmoid(x):
    return 1.0 / (1.0 + jnp.exp(-x))


def _adaln_kernel(c_ref, w_ref, b_ref, o_ref):
    c = c_ref[...]
    a = c * _sigmoid(c)
    o_ref[...] = jnp.dot(a, w_ref[...], preferred_element_type=jnp.float32,
                         precision=lax.Precision.HIGHEST) + b_ref[...]


def _adaln_mod(c, w_ada, b_ada):
    bsz = c.shape[0]
    n = w_ada.shape[1]
    tn = 512
    return pl.pallas_call(
        _adaln_kernel,
        out_shape=jax.ShapeDtypeStruct((bsz, n), jnp.float32),
        grid=(n // tn,),
        in_specs=[pl.BlockSpec((bsz, D_MODEL), lambda j: (0, 0)),
                  pl.BlockSpec((D_MODEL, tn), lambda j: (0, j)),
                  pl.BlockSpec((1, tn), lambda j: (0, j))],
        out_specs=pl.BlockSpec((bsz, tn), lambda j: (0, j)),
        compiler_params=_cparams(("arbitrary",)),
        name="adaln_mod",
    )(c, w_ada, b_ada.reshape(1, n))


def _rope_table_kernel(pos_ref, invf_ref, cos_ref, sin_ref):
    pos = pos_ref[...]
    rows = pos.shape[0]
    ang = jnp.concatenate([jnp.broadcast_to(pos[:, k:k + 1], (rows, HALF)) for k in range(ROPE_PACK)],
                          axis=1) * invf_ref[...]
    cos_ref[...] = jnp.cos(ang)
    sin_ref[...] = jnp.sin(ang)


def _rope_tables(positions, ts):
    bsz, seq = positions.shape
    sub = ts // ROPE_PACK
    rows = bsz * seq // ROPE_PACK
    pos = positions.astype(jnp.float32).reshape(bsz, seq // ts, ROPE_PACK, sub).transpose(0, 1, 3, 2)
    pos = pos.reshape(rows, ROPE_PACK)
    inv_freq = ROPE_THETA ** (-jnp.arange(HALF, dtype=jnp.float32) / HALF)
    invf = jnp.broadcast_to(inv_freq[None, :], (ROPE_PACK, HALF)).reshape(1, LANES)
    tr = seq // ROPE_PACK
    cos, sin = pl.pallas_call(
        _rope_table_kernel,
        out_shape=[jax.ShapeDtypeStruct((rows, LANES), jnp.float32)] * 2,
        grid=(rows // tr,),
        in_specs=[pl.BlockSpec((tr, ROPE_PACK), lambda i: (i, 0)),
                  pl.BlockSpec((1, LANES), lambda i: (0, 0))],
        out_specs=[pl.BlockSpec((tr, LANES), lambda i: (i, 0))] * 2,
        compiler_params=_cparams(("arbitrary",)),
        name="rope_table",
    )(pos, invf)
    return cos.reshape(bsz, tr, LANES), sin.reshape(bsz, tr, LANES)


N_QK_SLABS = 14
SLAB_QA, SLAB_KC, SLAB_KS, SLAB_KW, SLAB_QB, SLAB_KB, SLAB_VC, SLAB_GN = 0, 4, 5, 6, 7, 11, 12, 13
N_VT = 3


def _qkv_kernel(x_ref, mod_ref, w_ref, wvt_ref, cos_ref, sin_ref,
                qa_ref, kc_ref, ks_ref, kw_ref, qb_ref, kb_ref, vc_ref,
                vst_ref, vwt_ref, vbt_ref, gn_ref, kstage_ref, vstage_ref, *, ts, n_sub):
    shift = mod_ref[0, 0:1, :]
    scale = mod_ref[0, 1:2, :]

    def project(r):
        u = (_layer_norm(x_ref[0, r * ts:(r + 1) * ts, :]) * (1.0 + scale) + shift).astype(jnp.bfloat16)
        return _dot(u, w_ref[...]), _dot_nt(wvt_ref[...], u)

    def finish(r, acc, vt):
        rows = slice(r * ts, (r + 1) * ts)
        sub = ts // ROPE_PACK
        for k in range(ROPE_PACK):
            c32 = cos_ref[0, r * sub:(r + 1) * sub, k * HALF:(k + 1) * HALF]
            s32 = sin_ref[0, r * sub:(r + 1) * sub, k * HALF:(k + 1) * HALF]
            cos = jnp.concatenate([c32, c32, c32, c32], axis=1)
            sin = jnp.concatenate([-s32, -s32, s32, s32], axis=1)
            tok = slice(r * ts + k * sub, r * ts + (k + 1) * sub)

            def rope(slab):
                t = acc[k * sub:(k + 1) * sub, slab * LANES:(slab + 1) * LANES]
                return t * cos + pltpu.roll(t, LANES // 2, 1) * sin

            for j in range(GROUP):
                qa_ref[0, tok, j * LANES:(j + 1) * LANES] = (rope(SLAB_QA + j) * LOG2E).astype(jnp.bfloat16)
                qb_ref[0, tok, j * LANES:(j + 1) * LANES] = (rope(SLAB_QB + j) * LOG2E).astype(jnp.bfloat16)
            for slab, ref in ((SLAB_KS, ks_ref), (SLAB_KW, kw_ref), (SLAB_KB, kb_ref)):
                ref[0, tok, :] = rope(slab).astype(jnp.bfloat16)
            kstage_ref[k * sub:(k + 1) * sub, :] = rope(SLAB_KC)
        vstage_ref[...] = acc[:, SLAB_VC * LANES:(SLAB_VC + 1) * LANES]
        gn_ref[0, rows, :] = acc[:, SLAB_GN * LANES:(SLAB_GN + 1) * LANES]
        n_ch = ts // CMP_STRIDE
        for l in range(CMP_STRIDE):
            for stage, ref in ((kstage_ref, kc_ref), (vstage_ref, vc_ref)):
                ref[0, r * n_ch:(r + 1) * n_ch, l * LANES:(l + 1) * LANES] = (
                    stage[pl.ds(l, n_ch, stride=CMP_STRIDE), :].astype(jnp.bfloat16))
        ones = jnp.ones((G_ROWS - HEAD_DIM, ts), jnp.bfloat16)
        for j, ref in enumerate((vst_ref, vwt_ref, vbt_ref)):
            for g in range(N_KV):
                lo = j * KV_WIDTH + g * HEAD_DIM
                ref[0, g * G_ROWS:g * G_ROWS + HEAD_DIM, rows] = vt[lo:lo + HEAD_DIM].astype(jnp.bfloat16)
                ref[0, g * G_ROWS + HEAD_DIM:(g + 1) * G_ROWS, rows] = ones

    nxt = project(0)
    for r in range(n_sub):
        cur = nxt
        if r + 1 < n_sub:
            nxt = project(r + 1)
        finish(r, *cur)


def _qkv_proj(x, mod3, w_qk, w_vt, cos_t, sin_t, ts=QKV_ROWS, n_sub=8):
    bsz, seq, _ = x.shape
    tm = ts * n_sub
    chunk_w = CMP_STRIDE * KV_WIDTH
    q_shape = jax.ShapeDtypeStruct((bsz, seq, Q_WIDTH), jnp.bfloat16)
    k_shape = jax.ShapeDtypeStruct((bsz, seq, KV_WIDTH), jnp.bfloat16)
    c_shape = jax.ShapeDtypeStruct((bsz, seq // CMP_STRIDE, chunk_w), jnp.bfloat16)
    vt_shape = jax.ShapeDtypeStruct((bsz, V_ROWS, seq), jnp.bfloat16)
    q_spec = pl.BlockSpec((1, tm, Q_WIDTH), lambda b, i: (b, i, 0))
    k_spec = pl.BlockSpec((1, tm, KV_WIDTH), lambda b, i: (b, i, 0))
    c_spec = pl.BlockSpec((1, tm // CMP_STRIDE, chunk_w), lambda b, i: (b, i, 0))
    t_spec = pl.BlockSpec((1, tm // ROPE_PACK, LANES), lambda b, i: (b, i, 0))
    vt_spec = pl.BlockSpec((1, V_ROWS, tm), lambda b, i: (b, 0, i))
    return pl.pallas_call(
        functools.partial(_qkv_kernel, ts=ts, n_sub=n_sub),
        out_shape=[q_shape, c_shape, k_shape, k_shape, q_shape, k_shape, c_shape,
                   vt_shape, vt_shape, vt_shape, jax.ShapeDtypeStruct((bsz, seq, LANES), jnp.float32)],
        grid=(bsz, seq // tm),
        in_specs=[pl.BlockSpec((1, tm, D_MODEL), lambda b, i: (b, i, 0)),
                  pl.BlockSpec((1, 3, D_MODEL), lambda b, i: (b, 0, 0)),
                  pl.BlockSpec((D_MODEL, N_QK_SLABS * LANES), lambda b, i: (0, 0)),
                  pl.BlockSpec((N_VT * KV_WIDTH, D_MODEL), lambda b, i: (0, 0)),
                  t_spec, t_spec],
        out_specs=[q_spec, c_spec, k_spec, k_spec, q_spec, k_spec, c_spec,
                   vt_spec, vt_spec, vt_spec, k_spec],
        scratch_shapes=[pltpu.VMEM((ts, KV_WIDTH), jnp.float32), pltpu.VMEM((ts, KV_WIDTH), jnp.float32)],
        compiler_params=_cparams(("arbitrary", "arbitrary")),
        name="qkv_proj",
    )(x, mod3, w_qk, w_vt, cos_t, sin_t)


def _compress_kernel(xk_ref, xv_ref, w1k_ref, w1v_ref, posk_ref, posv_ref, w2k_ref, w2vt_ref,
                     kc_ref, vct_ref):
    bb, n_chunks, cw = xk_ref.shape

    def hidden(x_ref, w1_ref, pos_ref):
        hw = w1_ref.shape[1] // 2
        p = _dot(x_ref[...].reshape(bb * n_chunks, cw), w1_ref[...])
        pt = _dot(pos_ref[...], w1_ref[...])
        pos_term = pt[0:1, 0:hw] + pt[1:2, hw:]
        first = p[:, 0:hw]
        second = pltpu.roll(p[:, hw:], bb * n_chunks - 1, 0)
        hid = first + second + pos_term
        return (hid * _sigmoid(hid)).astype(jnp.bfloat16)

    kc = _dot(hidden(xk_ref, w1k_ref, posk_ref), w2k_ref[...]).astype(jnp.bfloat16)
    vct = _dot_nt(w2vt_ref[...], hidden(xv_ref, w1v_ref, posv_ref)).astype(jnp.bfloat16)
    for i in range(bb):
        kc_ref[i] = kc[i * n_chunks:(i + 1) * n_chunks]
        vct_ref[i] = vct[:, i * n_chunks:(i + 1) * n_chunks]


def _compress(xk, xv, w1k, w1v, posk, posv, w2k, w2vt, bb=4):
    bsz, n_chunks, cw = xk.shape
    hw2 = w1k.shape[1]
    out = jax.ShapeDtypeStruct((bsz, n_chunks, LANES), jnp.bfloat16)
    x_spec = pl.BlockSpec((bb, n_chunks, cw), lambda b: (b, 0, 0))
    w1_spec = pl.BlockSpec((cw, hw2), lambda b: (0, 0))
    pos_spec = pl.BlockSpec((SUBLANES, cw), lambda b: (0, 0))
    o_spec = pl.BlockSpec((bb, n_chunks, LANES), lambda b: (b, 0, 0))
    return pl.pallas_call(
        _compress_kernel,
        out_shape=[out, out],
        grid=(bsz // bb,),
        in_specs=[x_spec, x_spec, w1_spec, w1_spec, pos_spec, pos_spec,
                  pl.BlockSpec(w2k.shape, lambda b: (0, 0)), pl.BlockSpec(w2vt.shape, lambda b: (0, 0))],
        out_specs=[o_spec, o_spec],
        compiler_params=_cparams(("arbitrary",)),
        name="compress",
    )(xk, xv, w1k, w1v, posk, posv, w2k, w2vt)


def _stack_heads(q):
    lane = lax.broadcasted_iota(jnp.int32, (1, LANES), 1)
    is_a = (lane & (HEAD_DIM - 1)) < HALF
    zero = jnp.zeros((), q.dtype)
    tiles = [q[:, j * LANES:(j + 1) * LANES] for j in range(GROUP)]
    parts = [jnp.where(is_a, t, zero) for t in tiles] + [jnp.where(is_a, zero, t) for t in tiles]
    return jnp.concatenate(parts, axis=0)


def _store_heads(o_ref, rows, o_g, inv_l, tq):
    for j in range(GROUP):
        cols = slice(j * tq, (j + 1) * tq)
        halves = [o_g[g][0:HEAD_DIM, cols] for g in range(N_KV)]
        if inv_l is not None:
            halves = [h * inv_l[g][:, cols] for g, h in enumerate(halves)]
        tile_t = jnp.concatenate(halves, axis=0)
        o_ref[0, rows, j * LANES:(j + 1) * LANES] = tile_t.T.astype(jnp.bfloat16)


def _group_cols(g, tq):
    return slice(g * GROUP * tq, (g + 1) * GROUP * tq)


def _cmp_kernel(q_ref, kc_ref, vct_ref, ovt_ref, o_ref, bias_ref, *, tq, n_cmp, n_blk):
    qi = pl.program_id(1)
    nq = tq // LANES
    qs = _stack_heads(q_ref[0])
    c_row = lax.broadcasted_iota(jnp.int32, (LANES, tq), 0)
    t_col = qi * tq + lax.broadcasted_iota(jnp.int32, (LANES, tq), 1)
    vis = (c_row * CMP_STRIDE + (CMP_LEN - 1) <= t_col) & (c_row < n_cmp)
    any_vis = (t_col[0:1, :] >= CMP_LEN - 1).astype(jnp.float32)
    s_t = _dot_nt(kc_ref[0], qs)
    p_parts = []
    for cb in range(N_HEADS * nq):
        qq = slice((cb % nq) * LANES, (cb % nq + 1) * LANES)
        s = jnp.where(vis[:, qq], s_t[:, cb * LANES:(cb + 1) * LANES], NEG_INF)
        e = jnp.exp2(s - jnp.max(s, axis=0, keepdims=True))
        p_parts.append(e * (any_vis[:, qq] / jnp.sum(e, axis=0, keepdims=True)))
    p_t = jnp.concatenate(p_parts, axis=1)
    o_g = [_dot(vct_ref[0, g * HEAD_DIM:(g + 1) * HEAD_DIM, :], p_t[:, _group_cols(g, tq)].astype(jnp.bfloat16))
           for g in range(N_KV)]
    _store_heads(o_ref, slice(None), o_g, None, tq)

    j_blk = lax.broadcasted_iota(jnp.int32, (n_blk, tq), 0)
    t_q = qi * tq + lax.broadcasted_iota(jnp.int32, (n_blk, tq), 1)
    cur = jnp.right_shift(t_q, SEL_BLOCK.bit_length() - 1)
    forced = (j_blk == 0) | (j_blk == cur) | (j_blk == cur - 1)
    causal = j_blk <= cur
    n_free = SEL_TOPN - 3
    take_all = cur < SEL_TOPN
    candidate = causal & jnp.logical_not(forced)
    rows = []
    for g in range(N_KV):
        pg = p_t[:, g * GROUP * tq:(g * GROUP + 1) * tq]
        for h in range(1, GROUP):
            pg = pg + p_t[:, (g * GROUP + h) * tq:(g * GROUP + h + 1) * tq]
        hi = pg.astype(jnp.bfloat16)
        lo = (pg - hi.astype(jnp.float32)).astype(jnp.bfloat16)
        imp_t = _dot(ovt_ref[...], hi) + _dot(ovt_ref[...], lo)
        work = jnp.where(candidate, imp_t[0:n_blk, :], NEG_INF)
        picked = jnp.zeros((n_blk, tq), jnp.int32)
        for _ in range(n_free):
            top = jnp.max(work, axis=0, keepdims=True)
            first = jnp.min(jnp.where(work == top, j_blk, n_blk), axis=0, keepdims=True)
            hit = j_blk == first
            picked = jnp.where(hit, 1, picked)
            work = jnp.where(hit, NEG_INF, work)
        keep = causal & (take_all | forced | ((picked > 0) & candidate))
        rows.append(jnp.where(keep, 0.0, NEG_INF))
    rows.append(jnp.zeros((LANES - N_KV * n_blk, tq), jnp.float32))
    bias_t = jnp.concatenate(rows, axis=0)
    bias_ref[0] = bias_t.T.astype(jnp.bfloat16)


def _cmp_attn(qa, kc, vct, ovt, n_cmp, tq=2048):
    bsz, seq, _ = qa.shape
    n_blk = seq // SEL_BLOCK
    kern = functools.partial(_cmp_kernel, tq=tq, n_cmp=n_cmp, n_blk=n_blk)
    kv_spec = pl.BlockSpec((1, LANES, LANES), lambda b, i: (b, 0, 0))
    return pl.pallas_call(
        kern,
        out_shape=[jax.ShapeDtypeStruct((bsz, seq, Q_WIDTH), jnp.bfloat16),
                   jax.ShapeDtypeStruct((bsz, seq, LANES), jnp.bfloat16)],
        grid=(bsz, seq // tq),
        in_specs=[pl.BlockSpec((1, tq, Q_WIDTH), lambda b, i: (b, i, 0)),
                  kv_spec, kv_spec,
                  pl.BlockSpec((LANES, LANES), lambda b, i: (0, 0))],
        out_specs=[pl.BlockSpec((1, tq, Q_WIDTH), lambda b, i: (b, i, 0)),
                   pl.BlockSpec((1, tq, LANES), lambda b, i: (b, i, 0))],
        compiler_params=_cparams(("arbitrary", "arbitrary")),
        name="cmp_attn",
    )(qa, kc, vct, ovt)


def _sel_kernel(q_ref, bias_ref, k_ref, vt_ref, blk_ref, o_ref, kaug_ref, qaug_ref, s0_ref, s1_ref, acc_ref, m_ref,
                *, tq, tk):
    step = pl.program_id(1)
    nq = tq // LANES
    nk = tk // LANES
    per_g = GROUP * nq

    @pl.when(step == 0)
    def _():
        kaug_ref[:, 0:LANES] = k_ref[0]
        kaug_ref[:, LANES:2 * LANES] = blk_ref[...]

    lane = lax.broadcasted_iota(jnp.int32, (1, LANES), 1)
    n_blk = blk_ref.shape[0] // SEL_BLOCK
    bt = bias_ref[0]
    zero = jnp.zeros((), bt.dtype)
    b0 = jnp.where(lane < n_blk, bt, zero)
    b1 = jnp.where((lane >= n_blk) & (lane < 2 * n_blk), bt, zero)
    qaug_ref[:, 0:LANES] = _stack_heads(q_ref[0])
    qaug_ref[:, LANES:2 * LANES] = jnp.concatenate([b0] * GROUP + [b1] * GROUP, axis=0)

    acc_ref[...] = jnp.zeros_like(acc_ref)
    m_ref[...] = jnp.full_like(m_ref, NEG_INF)
    row = lax.broadcasted_iota(jnp.int32, (tk, tk), 0)
    col = lax.broadcasted_iota(jnp.int32, (tk, tk), 1)
    causal_bias = jnp.where(row <= col, 0.0, NEG_INF)

    def scores(c, s_ref):
        k0 = pl.multiple_of(c * tk, tk)
        s_ref[...] = _dot_nt(kaug_ref[pl.ds(k0, tk), :], qaug_ref[...])

    def accumulate(c, s_ref, diag_half):
        k0 = pl.multiple_of(c * tk, tk)
        for g in range(N_KV):
            ps, alphas = [], []
            for i in range(per_g):
                cb = g * per_g + i
                half, within = divmod(cb % nq, nk)
                cols = slice(cb * LANES, (cb + 1) * LANES)
                if diag_half is not None and half < diag_half:
                    ps.append(jnp.zeros((tk, LANES), jnp.bfloat16))
                    alphas.append(jnp.ones((1, LANES), jnp.float32))
                    continue
                s = s_ref[:, cols]
                if diag_half is not None and half == diag_half:
                    s = s + causal_bias[:, within * LANES:(within + 1) * LANES]
                m_old = m_ref[0:1, cols]
                m_new = jnp.maximum(m_old, jnp.max(s, axis=0, keepdims=True))
                m_ref[0:1, cols] = m_new
                alphas.append(jnp.exp2(m_old - m_new))
                ps.append(jnp.exp2(s - m_new).astype(jnp.bfloat16))
            vt_g = vt_ref[0, g * G_ROWS:(g + 1) * G_ROWS, pl.ds(k0, tk)]
            pv = _dot(vt_g, jnp.concatenate(ps, axis=1))
            acc_ref[g] = jnp.concatenate(alphas, axis=1) * acc_ref[g] + pv

    scores(0, s0_ref)

    def body(i, carry):
        c = 2 * i
        scores(c + 1, s1_ref)
        accumulate(c, s0_ref, None)
        scores(c + 2, s0_ref)
        accumulate(c + 1, s1_ref, None)
        return carry

    lax.fori_loop(0, step, body, 0)
    c0 = 2 * step
    scores(c0 + 1, s1_ref)
    accumulate(c0, s0_ref, 0)
    accumulate(c0 + 1, s1_ref, 1)

    o_g = [acc_ref[g] for g in range(N_KV)]
    inv_l = [1.0 / o[HEAD_DIM:HEAD_DIM + 1, :] for o in o_g]
    _store_heads(o_ref, slice(None), o_g, inv_l, tq)


def _sel_attn(qa, bias, ks, vst, blk_onehot, tk=256):
    bsz, seq, _ = qa.shape
    tq = 2 * tk
    kern = functools.partial(_sel_kernel, tq=tq, tk=tk)
    return pl.pallas_call(
        kern,
        out_shape=jax.ShapeDtypeStruct((bsz, seq, Q_WIDTH), jnp.bfloat16),
        grid=(bsz, seq // tq),
        in_specs=[pl.BlockSpec((1, tq, Q_WIDTH), lambda b, i: (b, i, 0)),
                  pl.BlockSpec((1, tq, LANES), lambda b, i: (b, i, 0)),
                  pl.BlockSpec((1, seq, LANES), lambda b, i: (b, 0, 0)),
                  pl.BlockSpec((1, V_ROWS, seq), lambda b, i: (b, 0, 0)),
                  pl.BlockSpec((seq, LANES), lambda b, i: (0, 0))],
        out_specs=pl.BlockSpec((1, tq, Q_WIDTH), lambda b, i: (b, i, 0)),
        scratch_shapes=[pltpu.VMEM((seq, 2 * LANES), jnp.bfloat16),
                        pltpu.VMEM((N_HEADS * tq, 2 * LANES), jnp.bfloat16),
                        pltpu.VMEM((tk, N_HEADS * tq), jnp.float32),
                        pltpu.VMEM((tk, N_HEADS * tq), jnp.float32),
                        pltpu.VMEM((N_KV, G_ROWS, GROUP * tq), jnp.float32),
                        pltpu.VMEM((SUBLANES, N_HEADS * tq), jnp.float32)],
        compiler_params=_cparams(("arbitrary", "arbitrary")),
        name="sel_attn",
    )(qa, bias, ks, vst, blk_onehot)


def _band_kernel(*refs, tq, n_sub, span, window, use_sink):
    if use_sink:
        q_ref, k_ref, vt_ref, sink_ref, o_ref = refs
    else:
        q_ref, k_ref, vt_ref, o_ref = refs
    step = pl.program_id(1)
    seq = k_ref.shape[1]
    nq = tq // LANES
    row = lax.broadcasted_iota(jnp.int32, (span, tq), 0)
    col = lax.broadcasted_iota(jnp.int32, (span, tq), 1)
    diff = col - row

    def scores(sub):
        q0 = (step * n_sub + sub) * tq
        start = pl.multiple_of(jnp.clip(q0 + tq - span, 0, seq - span), LANES)
        rel = diff + (q0 - start)
        bias = jnp.where((rel >= 0) & (rel < window), 0.0, NEG_INF)
        qs = _stack_heads(q_ref[0, sub * tq:(sub + 1) * tq, :])
        return start, bias, _dot_nt(k_ref[0, pl.ds(start, span), :], qs)

    def finish(sub, start, bias, s_t):
        ps, extra = [], []
        for cb in range(N_HEADS * nq):
            s = s_t[:, cb * LANES:(cb + 1) * LANES] + bias[:, (cb % nq) * LANES:(cb % nq + 1) * LANES]
            m = jnp.max(s, axis=0, keepdims=True)
            if use_sink:
                sink = sink_ref[cb // nq] * LOG2E
                m = jnp.maximum(m, sink)
                extra.append(jnp.exp2(sink - m))
            ps.append(jnp.exp2(s - m).astype(jnp.bfloat16))
        o_g, inv_l = [], []
        for g in range(N_KV):
            per_g = GROUP * nq
            vt_g = vt_ref[0, g * G_ROWS:(g + 1) * G_ROWS, pl.ds(start, span)]
            o = _dot(vt_g, jnp.concatenate(ps[g * per_g:(g + 1) * per_g], axis=1))
            l = o[HEAD_DIM:HEAD_DIM + 1, :]
            if use_sink:
                l = l + jnp.concatenate(extra[g * per_g:(g + 1) * per_g], axis=1)
            o_g.append(o)
            inv_l.append(1.0 / l)
        _store_heads(o_ref, slice(sub * tq, (sub + 1) * tq), o_g, inv_l, tq)

    nxt = scores(0)
    for sub in range(n_sub):
        cur = nxt
        if sub + 1 < n_sub:
            nxt = scores(sub + 1)
        finish(sub, *cur)


def _band_attn(q, k, vt, window, sinks=None, tq=128, n_sub=16):
    bsz, seq, _ = q.shape
    span = (-(-(window - 1) // tq)) * tq + tq
    use_sink = sinks is not None
    kern = functools.partial(_band_kernel, tq=tq, n_sub=n_sub, span=span, window=window, use_sink=use_sink)
    ts = tq * n_sub
    in_specs = [pl.BlockSpec((1, ts, Q_WIDTH), lambda b, i: (b, i, 0)),
                pl.BlockSpec((1, seq, LANES), lambda b, i: (b, 0, 0)),
                pl.BlockSpec((1, V_ROWS, seq), lambda b, i: (b, 0, 0))]
    args = [q, k, vt]
    if use_sink:
        in_specs.append(pl.BlockSpec(memory_space=pltpu.SMEM))
        args.append(sinks)
    return pl.pallas_call(
        kern,
        out_shape=jax.ShapeDtypeStruct((bsz, seq, Q_WIDTH), jnp.bfloat16),
        grid=(bsz, seq // ts),
        in_specs=in_specs,
        out_specs=pl.BlockSpec((1, ts, Q_WIDTH), lambda b, i: (b, i, 0)),
        compiler_params=_cparams(("arbitrary", "arbitrary")),
        name="band_attn_sink" if use_sink else "band_attn",
    )(*args)


def _merge_kernel(x_ref, mod_ref, gn_ref, wza_ref, wzb_ref, wgm_ref, gexp_ref, ocmp_ref, osel_ref, owin_ref, ob_ref,
                  wua_ref, wub_ref, wo_ref, lng_ref, lnb_ref, o_ref, *, ts, n_sub):
    shift = mod_ref[0, 0:1, :]
    scale = mod_ref[0, 1:2, :]
    gate = mod_ref[0, 2:3, :]

    def project(r):
        u = (_layer_norm(x_ref[0, r * ts:(r + 1) * ts, :]) * (1.0 + scale) + shift).astype(jnp.bfloat16)
        return _dot(u, wza_ref[...]), _dot(u, wzb_ref[...]), _dot(u, wgm_ref[...])

    def finish(r, z_a, z_b, g_m):
        rows = slice(r * ts, (r + 1) * ts)
        sig = _sigmoid(gn_ref[0, rows, :])
        hi = sig.astype(jnp.bfloat16)
        lo = (sig - hi.astype(jnp.float32)).astype(jnp.bfloat16)
        gx = _dot(jnp.concatenate([hi, lo], axis=1), gexp_ref[...])
        o_a = (gx[:, 0:Q_WIDTH] * ocmp_ref[0, rows, :].astype(jnp.float32)
               + gx[:, Q_WIDTH:2 * Q_WIDTH] * osel_ref[0, rows, :].astype(jnp.float32)
               + gx[:, 2 * Q_WIDTH:3 * Q_WIDTH] * owin_ref[0, rows, :].astype(jnp.float32))
        y_a = (o_a * (z_a * _sigmoid(z_a))).astype(jnp.bfloat16)
        y_b = (ob_ref[0, rows, :].astype(jnp.float32) * (z_b * _sigmoid(z_b))).astype(jnp.bfloat16)
        merged = _sigmoid(g_m[:, 0:D_MODEL]) * _dot(y_a, wua_ref[...])
        merged = merged + _sigmoid(g_m[:, D_MODEL:2 * D_MODEL]) * _dot(y_b, wub_ref[...])
        out = _dot(merged.astype(jnp.bfloat16), wo_ref[...])
        y = DEEPNORM_ALPHA * x_ref[0, rows, :] + gate * out
        o_ref[0, rows, :] = _layer_norm(y) * lng_ref[...] + lnb_ref[...]

    nxt = project(0)
    for r in range(n_sub):
        cur = nxt
        if r + 1 < n_sub:
            nxt = project(r + 1)
        finish(r, *cur)


def _merge_out(x, mod3, g_nsa, wgs, gexp, o_cmp, o_sel, o_win, o_b, wua, wub, wo, ln_g, ln_b, ts=256, n_sub=4):
    bsz, seq, _ = x.shape
    tm = ts * n_sub
    tok = lambda w: pl.BlockSpec((1, tm, w), lambda b, i: (b, i, 0))
    full = lambda a: pl.BlockSpec(a.shape, lambda b, i: (0,) * a.ndim)
    return pl.pallas_call(
        functools.partial(_merge_kernel, ts=ts, n_sub=n_sub),
        out_shape=jax.ShapeDtypeStruct(x.shape, jnp.float32),
        grid=(bsz, seq // tm),
        in_specs=[tok(D_MODEL), pl.BlockSpec((1, 3, D_MODEL), lambda b, i: (b, 0, 0)),
                  tok(LANES), *[full(w) for w in wgs], full(gexp), tok(Q_WIDTH), tok(Q_WIDTH), tok(Q_WIDTH), tok(Q_WIDTH),
                  full(wua), full(wub), full(wo), full(ln_g), full(ln_b)],
        out_specs=tok(D_MODEL),
        compiler_params=_cparams(("arbitrary", "arbitrary")),
        name="merge_out",
    )(x, mod3, g_nsa, *wgs, gexp, o_cmp, o_sel, o_win, o_b, wua, wub, wo, ln_g, ln_b)


def _pair_q_cols(w):
    k = w.shape[0]
    return w.reshape(k, N_KV, GROUP, 2, HALF).transpose(0, 2, 3, 1, 4).reshape(k, Q_WIDTH)


def _pair_k_cols(w):
    k = w.shape[0]
    return w.reshape(k, N_KV, 2, HALF).transpose(0, 2, 1, 3).reshape(k, KV_WIDTH)


def _pair_o_cols(w):
    k = w.shape[0]
    return w.reshape(k, N_KV, GROUP, HEAD_DIM).transpose(0, 2, 1, 3).reshape(k, Q_WIDTH)


def _pair_o_rows(w):
    n = w.shape[1]
    return w.reshape(N_KV, GROUP, HEAD_DIM, n).transpose(1, 0, 2, 3).reshape(Q_WIDTH, n)


def _prep_qkv_weight(w_in):
    q_scale = HEAD_DIM ** -0.5
    sec = lambda off, width: w_in[:, off:off + width]
    cols = [_pair_q_cols(sec(OFF_QA, Q_WIDTH)) * q_scale,
            _pair_k_cols(sec(OFF_KVC, KV_WIDTH)), _pair_k_cols(sec(OFF_KVS, KV_WIDTH)),
            _pair_k_cols(sec(OFF_KVW, KV_WIDTH)),
            _pair_q_cols(sec(OFF_QB, Q_WIDTH)) * q_scale,
            _pair_k_cols(sec(OFF_KVB, KV_WIDTH)),
            sec(OFF_KVC + KV_WIDTH, KV_WIDTH),
            sec(OFF_GN, 3 * N_HEADS), jnp.zeros((D_MODEL, LANES - 3 * N_HEADS), w_in.dtype)]
    v_cols = [sec(OFF_KVS + KV_WIDTH, KV_WIDTH), sec(OFF_KVW + KV_WIDTH, KV_WIDTH),
              sec(OFF_KVB + KV_WIDTH, KV_WIDTH)]
    return (jnp.concatenate(cols, axis=1).astype(jnp.bfloat16),
            jnp.concatenate(v_cols, axis=1).T.astype(jnp.bfloat16))


def _prep_gate_weights(w_in):
    ws = [_pair_o_cols(w_in[:, OFF_ZA:OFF_ZA + Q_WIDTH]), _pair_o_cols(w_in[:, OFF_ZB:OFF_ZB + Q_WIDTH]),
          w_in[:, OFF_GM:OFF_GM + 2 * D_MODEL]]
    return [w.astype(jnp.bfloat16) for w in ws]


def _gate_expand_matrix():
    e = np.zeros((LANES, 3 * Q_WIDTH), np.float32)
    for br in range(3):
        for c in range(Q_WIDTH):
            h = int(O_PERM[c]) // HEAD_DIM
            e[h * 3 + br, br * Q_WIDTH + c] = 1.0
    return jnp.asarray(np.concatenate([e, e], axis=0), jnp.bfloat16)


def _prep_compress(pos, w1, w2, paired):
    half_len = CMP_LEN // 2
    w = w1.reshape(CMP_LEN, HEAD_DIM, CMP_HIDDEN).astype(jnp.bfloat16)
    w2 = w2.astype(jnp.bfloat16)
    pieces = ([(g, slice(h * HALF, (h + 1) * HALF)) for h in range(2) for g in range(N_KV)] if paired
              else [(g, slice(0, HEAD_DIM)) for g in range(N_KV)])
    own = lambda g, blk: jnp.concatenate([blk if k == g else jnp.zeros_like(blk) for k in range(N_KV)], axis=-1)
    w1x = jnp.concatenate([own(g, w[:, d, :]) for g, d in pieces], axis=1)
    first = w1x[:half_len].reshape(half_len * LANES, N_KV * CMP_HIDDEN)
    second = w1x[half_len:].reshape(half_len * LANES, N_KV * CMP_HIDDEN)
    w1e = jnp.concatenate([first, second], axis=1)
    w2e = jnp.concatenate([jnp.concatenate([w2[:, d] if k == g else jnp.zeros_like(w2[:, d]) for g, d in pieces], axis=1)
                           for k in range(N_KV)], axis=0)
    pos_l = jnp.concatenate([pos[:, d] for _, d in pieces], axis=1)
    pos_rows = jnp.concatenate([pos_l[:half_len].reshape(1, -1), pos_l[half_len:].reshape(1, -1),
                                jnp.zeros((SUBLANES - 2, half_len * LANES), pos.dtype)], axis=0).astype(jnp.bfloat16)
    return w1e, pos_rows, w2e


def _overlap_t(n_cmp, n_blk):
    c0 = np.arange(n_cmp) * CMP_STRIDE
    j0 = np.arange(n_blk) * SEL_BLOCK
    overlap = (c0[:, None] < j0[None, :] + SEL_BLOCK) & (c0[:, None] + CMP_LEN > j0[None, :])
    ovt = np.zeros((LANES, LANES), np.float32)
    ovt[:n_blk, :n_cmp] = overlap.T
    return jnp.asarray(ovt, jnp.bfloat16)


def _block_onehot(seq):
    n_blk = seq // SEL_BLOCK
    e = np.zeros((seq, LANES), np.float32)
    pos = np.arange(seq)
    e[pos, pos // SEL_BLOCK] = 1.0
    e[pos, n_blk + pos // SEL_BLOCK] = 1.0
    return jnp.asarray(e, jnp.bfloat16)


def kernel(x, c, positions, w_ada, b_ada, w_in, cmp_pos_k, cmp_w1_k, cmp_w2_k,
           cmp_pos_v, cmp_w1_v, cmp_w2_v, sinks, w_up_a, w_up_b, w_out, ln_g, ln_b):
    bsz, seq, _ = x.shape
    n_cmp = (seq - CMP_LEN) // CMP_STRIDE + 1
    n_blk = seq // SEL_BLOCK
    assert seq // CMP_STRIDE == LANES and 2 * n_blk <= LANES
    cos_t, sin_t = _rope_tables(positions, QKV_ROWS)
    for l in range(DEPTH):
        mod3 = _adaln_mod(c, w_ada[l], b_ada[l]).reshape(bsz, 3, D_MODEL)
        w_qk, w_vt = _prep_qkv_weight(w_in[l])
        qa, k_c, k_s, k_w, qb, k_b, v_c, vt_s, vt_w, vt_b, g_nsa = _qkv_proj(x, mod3, w_qk, w_vt, cos_t, sin_t)

        w1k, posk, w2k = _prep_compress(cmp_pos_k[l], cmp_w1_k[l], cmp_w2_k[l], paired=True)
        w1v, posv, w2v = _prep_compress(cmp_pos_v[l], cmp_w1_v[l], cmp_w2_v[l], paired=False)
        kc, vct = _compress(k_c, v_c, w1k, w1v, posk, posv, w2k, w2v.T, bb=4 if bsz % 4 == 0 else 1)

        o_cmp, sel_bias = _cmp_attn(qa, kc, vct, _overlap_t(n_cmp, n_blk), n_cmp)
        o_sel = _sel_attn(qa, sel_bias, k_s, vt_s, _block_onehot(seq))
        o_win = _band_attn(qa, k_w, vt_w, NSA_WINDOW)
        o_b = _band_attn(qb, k_b, vt_b, SWA_WINDOW, sinks[l].astype(jnp.float32))

        x = _merge_out(x, mod3, g_nsa, _prep_gate_weights(w_in[l]), _gate_expand_matrix(),
                       o_cmp, o_sel, o_win, o_b,
                       _pair_o_rows(w_up_a[l]).astype(jnp.bfloat16), _pair_o_rows(w_up_b[l]).astype(jnp.bfloat16),
                       w_out[l].astype(jnp.bfloat16), ln_g[l].reshape(1, D_MODEL), ln_b[l].reshape(1, D_MODEL))
    return x
```

```python
import functools

import numpy as np
import jax
import jax.numpy as jnp
from jax import lax
from jax.experimental import pallas as pl
from jax.experimental.pallas import tpu as pltpu

D_MODEL = 1024
HEAD_DIM = 64
HALF = HEAD_DIM // 2
N_HEADS = 8
N_KV = 2
GROUP = N_HEADS // N_KV
Q_WIDTH = N_HEADS * HEAD_DIM
KV_WIDTH = N_KV * HEAD_DIM
CMP_LEN = 32
CMP_STRIDE = 16
CMP_HIDDEN = 256
SEL_BLOCK = 64
SEL_TOPN = 8
NSA_WINDOW = 512
SWA_WINDOW = 128
ROPE_THETA = 10000.0
LN_EPS = 1e-5
NEG_INF = -1e30
FORCE_SCORE = 1e9
DEPTH = 1
DEEPNORM_ALPHA = (2 * DEPTH) ** 0.25
LANES = 128
SUBLANES = 8
BF16_SUBLANES = 16
ROPE_PACK = LANES // HALF
QKV_ROWS = 256
LOG2E = 1.4426950408889634
G_ROWS = HEAD_DIM + BF16_SUBLANES
V_ROWS = N_KV * G_ROWS
VMEM_LIMIT = 56 * 1024 * 1024

_SPLITS = [Q_WIDTH, 2 * KV_WIDTH, 2 * KV_WIDTH, 2 * KV_WIDTH, 3 * N_HEADS, Q_WIDTH,
           Q_WIDTH, 2 * KV_WIDTH, Q_WIDTH, 2 * D_MODEL]
_OFFS = [int(o) for o in np.cumsum([0] + _SPLITS)]
(OFF_QA, OFF_KVC, OFF_KVS, OFF_KVW, OFF_GN, OFF_ZA, OFF_QB, OFF_KVB, OFF_ZB, OFF_GM) = _OFFS[:10]


def _q_pair_perm():
    cols = []
    for j in range(GROUP):
        for half in range(2):
            for grp in range(N_KV):
                h = j + GROUP * grp
                cols += [h * HEAD_DIM + half * HALF + i for i in range(HALF)]
    return np.asarray(cols, np.int32)


def _k_pair_perm():
    cols = []
    for half in range(2):
        for grp in range(N_KV):
            cols += [grp * HEAD_DIM + half * HALF + i for i in range(HALF)]
    return np.asarray(cols, np.int32)


def _o_pair_perm():
    cols = []
    for j in range(GROUP):
        for grp in range(N_KV):
            h = j + GROUP * grp
            cols += [h * HEAD_DIM + d for d in range(HEAD_DIM)]
    return np.asarray(cols, np.int32)


Q_PERM = _q_pair_perm()
K_PERM = _k_pair_perm()
O_PERM = _o_pair_perm()


def _cparams(sem):
    return pltpu.CompilerParams(dimension_semantics=sem, vmem_limit_bytes=VMEM_LIMIT)


def _dot(a, b):
    return jnp.dot(a, b, preferred_element_type=jnp.float32)


def _dot_nt(a, b):
    return lax.dot_general(a, b, (((1,), (1,)), ((), ())), preferred_element_type=jnp.float32)


def _layer_norm(x):
    mu = jnp.mean(x, axis=-1, keepdims=True)
    xc = x - mu
    var = jnp.mean(xc * xc, axis=-1, keepdims=True)
    return xc * lax.rsqrt(var + LN_EPS)


def _sigmoid(x):
    return 1.0 / (1.0 + jnp.exp(-x))


def _adaln_kernel(c_ref, w_ref, b_ref, o_ref):
    c = c_ref[...]
    a = c * _sigmoid(c)
    o_ref[...] = jnp.dot(a, w_ref[...], preferred_element_type=jnp.float32,
                         precision=lax.Precision.HIGHEST) + b_ref[...]


def _adaln_mod(c, w_ada, b_ada):
    bsz = c.shape[0]
    n = w_ada.shape[1]
    tn = 512
    return pl.pallas_call(
        _adaln_kernel,
        out_shape=jax.ShapeDtypeStruct((bsz, n), jnp.float32),
        grid=(n // tn,),
        in_specs=[pl.BlockSpec((bsz, D_MODEL), lambda j: (0, 0)),
                  pl.BlockSpec((D_MODEL, tn), lambda j: (0, j)),
                  pl.BlockSpec((1, tn), lambda j: (0, j))],
        out_specs=pl.BlockSpec((bsz, tn), lambda j: (0, j)),
        compiler_params=_cparams(("arbitrary",)),
        name="adaln_mod",
    )(c, w_ada, b_ada.reshape(1, n))


def _rope_table_kernel(pos_ref, invf_ref, cos_ref, sin_ref):
    pos = pos_ref[...]
    rows = pos.shape[0]
    ang = jnp.concatenate([jnp.broadcast_to(pos[:, k:k + 1], (rows, HALF)) for k in range(ROPE_PACK)],
                          axis=1) * invf_ref[...]
    cos_ref[...] = jnp.cos(ang)
    sin_ref[...] = jnp.sin(ang)


def _rope_tables(positions, ts):
    bsz, seq = positions.shape
    sub = ts // ROPE_PACK
    rows = bsz * seq // ROPE_PACK
    pos = positions.astype(jnp.float32).reshape(bsz, seq // ts, ROPE_PACK, sub).transpose(0, 1, 3, 2)
    pos = pos.reshape(rows, ROPE_PACK)
    inv_freq = ROPE_THETA ** (-jnp.arange(HALF, dtype=jnp.float32) / HALF)
    invf = jnp.broadcast_to(inv_freq[None, :], (ROPE_PACK, HALF)).reshape(1, LANES)
    tr = seq // ROPE_PACK
    cos, sin = pl.pallas_call(
        _rope_table_kernel,
        out_shape=[jax.ShapeDtypeStruct((rows, LANES), jnp.float32)] * 2,
        grid=(rows // tr,),
        in_specs=[pl.BlockSpec((tr, ROPE_PACK), lambda i: (i, 0)),
                  pl.BlockSpec((1, LANES), lambda i: (0, 0))],
        out_specs=[pl.BlockSpec((tr, LANES), lambda i: (i, 0))] * 2,
        compiler_params=_cparams(("arbitrary",)),
        name="rope_table",
    )(pos, invf)
    return cos.reshape(bsz, tr, LANES), sin.reshape(bsz, tr, LANES)


N_QK_SLABS = 14
SLAB_QA, SLAB_KC, SLAB_KS, SLAB_KW, SLAB_QB, SLAB_KB, SLAB_VC, SLAB_GN = 0, 4, 5, 6, 7, 11, 12, 13
N_VT = 3


def _qkv_kernel(x_ref, mod_ref, w_ref, wvt_ref, cos_ref, sin_ref,
                qa_ref, kc_ref, ks_ref, kw_ref, qb_ref, kb_ref, vc_ref,
                vst_ref, vwt_ref, vbt_ref, gn_ref, kstage_ref, vstage_ref, *, ts, n_sub):
    shift = mod_ref[0, 0:1, :]
    scale = mod_ref[0, 1:2, :]

    def project(r):
        u = (_layer_norm(x_ref[0, r * ts:(r + 1) * ts, :]) * (1.0 + scale) + shift).astype(jnp.bfloat16)
        return _dot(u, w_ref[...]), _dot_nt(wvt_ref[...], u)

    def finish(r, acc, vt):
        rows = slice(r * ts, (r + 1) * ts)
        sub = ts // ROPE_PACK
        for k in range(ROPE_PACK):
            c32 = cos_ref[0, r * sub:(r + 1) * sub, k * HALF:(k + 1) * HALF]
            s32 = sin_ref[0, r * sub:(r + 1) * sub, k * HALF:(k + 1) * HALF]
            cos = jnp.concatenate([c32, c32, c32, c32], axis=1)
            sin = jnp.concatenate([-s32, -s32, s32, s32], axis=1)
            tok = slice(r * ts + k * sub, r * ts + (k + 1) * sub)

            def rope(slab):
                t = acc[k * sub:(k + 1) * sub, slab * LANES:(slab + 1) * LANES]
                return t * cos + pltpu.roll(t, LANES // 2, 1) * sin

            for j in range(GROUP):
                qa_ref[0, tok, j * LANES:(j + 1) * LANES] = (rope(SLAB_QA + j) * LOG2E).astype(jnp.bfloat16)
                qb_ref[0, tok, j * LANES:(j + 1) * LANES] = (rope(SLAB_QB + j) * LOG2E).astype(jnp.bfloat16)
            for slab, ref in ((SLAB_KS, ks_ref), (SLAB_KW, kw_ref), (SLAB_KB, kb_ref)):
                ref[0, tok, :] = rope(slab).astype(jnp.bfloat16)
            kstage_ref[k * sub:(k + 1) * sub, :] = rope(SLAB_KC)
        vstage_ref[...] = acc[:, SLAB_VC * LANES:(SLAB_VC + 1) * LANES]
        gn_ref[0, rows, :] = acc[:, SLAB_GN * LANES:(SLAB_GN + 1) * LANES]
        n_ch = ts // CMP_STRIDE
        for l in range(CMP_STRIDE):
            for stage, ref in ((kstage_ref, kc_ref), (vstage_ref, vc_ref)):
                ref[0, r * n_ch:(r + 1) * n_ch, l * LANES:(l + 1) * LANES] = (
                    stage[pl.ds(l, n_ch, stride=CMP_STRIDE), :].astype(jnp.bfloat16))
        ones = jnp.ones((G_ROWS - HEAD_DIM, ts), jnp.bfloat16)
        for j, ref in enumerate((vst_ref, vwt_ref, vbt_ref)):
            for g in range(N_KV):
                lo = j * KV_WIDTH + g * HEAD_DIM
                ref[0, g * G_ROWS:g * G_ROWS + HEAD_DIM, rows] = vt[lo:lo + HEAD_DIM].astype(jnp.bfloat16)
                ref[0, g * G_ROWS + HEAD_DIM:(g + 1) * G_ROWS, rows] = ones

    nxt = project(0)
    for r in range(n_sub):
        cur = nxt
        if r + 1 < n_sub:
            nxt = project(r + 1)
        finish(r, *cur)


def _qkv_proj(x, mod3, w_qk, w_vt, cos_t, sin_t, ts=QKV_ROWS, n_sub=8):
    bsz, seq, _ = x.shape
    tm = ts * n_sub
    chunk_w = CMP_STRIDE * KV_WIDTH
    q_shape = jax.ShapeDtypeStruct((bsz, seq, Q_WIDTH), jnp.bfloat16)
    k_shape = jax.ShapeDtypeStruct((bsz, seq, KV_WIDTH), jnp.bfloat16)
    c_shape = jax.ShapeDtypeStruct((bsz, seq // CMP_STRIDE, chunk_w), jnp.bfloat16)
    vt_shape = jax.ShapeDtypeStruct((bsz, V_ROWS, seq), jnp.bfloat16)
    q_spec = pl.BlockSpec((1, tm, Q_WIDTH), lambda b, i: (b, i, 0))
    k_spec = pl.BlockSpec((1, tm, KV_WIDTH), lambda b, i: (b, i, 0))
    c_spec = pl.BlockSpec((1, tm // CMP_STRIDE, chunk_w), lambda b, i: (b, i, 0))
    t_spec = pl.BlockSpec((1, tm // ROPE_PACK, LANES), lambda b, i: (b, i, 0))
    vt_spec = pl.BlockSpec((1, V_ROWS, tm), lambda b, i: (b, 0, i))
    return pl.pallas_call(
        functools.partial(_qkv_kernel, ts=ts, n_sub=n_sub),
        out_shape=[q_shape, c_shape, k_shape, k_shape, q_shape, k_shape, c_shape,
                   vt_shape, vt_shape, vt_shape, jax.ShapeDtypeStruct((bsz, seq, LANES), jnp.float32)],
        grid=(bsz, seq // tm),
        in_specs=[pl.BlockSpec((1, tm, D_MODEL), lambda b, i: (b, i, 0)),
                  pl.BlockSpec((1, 3, D_MODEL), lambda b, i: (b, 0, 0)),
                  pl.BlockSpec((D_MODEL, N_QK_SLABS * LANES), lambda b, i: (0, 0)),
                  pl.BlockSpec((N_VT * KV_WIDTH, D_MODEL), lambda b, i: (0, 0)),
                  t_spec, t_spec],
        out_specs=[q_spec, c_spec, k_spec, k_spec, q_spec, k_spec, c_spec,
                   vt_spec, vt_spec, vt_spec, k_spec],
        scratch_shapes=[pltpu.VMEM((ts, KV_WIDTH), jnp.float32), pltpu.VMEM((ts, KV_WIDTH), jnp.float32)],
        compiler_params=_cparams(("arbitrary", "arbitrary")),
        name="qkv_proj",
    )(x, mod3, w_qk, w_vt, cos_t, sin_t)


def _compress_kernel(xk_ref, xv_ref, w1k_ref, w1v_ref, posk_ref, posv_ref, w2k_ref, w2vt_ref,
                     kc_ref, vct_ref):
    bb, n_chunks, cw = xk_ref.shape

    def hidden(x_ref, w1_ref, pos_ref):
        hw = w1_ref.shape[1] // 2
        p = _dot(x_ref[...].reshape(bb * n_chunks, cw), w1_ref[...])
        pt = _dot(pos_ref[...], w1_ref[...])
        pos_term = pt[0:1, 0:hw] + pt[1:2, hw:]
        first = p[:, 0:hw]
        second = pltpu.roll(p[:, hw:], bb * n_chunks - 1, 0)
        hid = first + second + pos_term
        return (hid * _sigmoid(hid)).astype(jnp.bfloat16)

    kc = _dot(hidden(xk_ref, w1k_ref, posk_ref), w2k_ref[...]).astype(jnp.bfloat16)
    vct = _dot_nt(w2vt_ref[...], hidden(xv_ref, w1v_ref, posv_ref)).astype(jnp.bfloat16)
    for i in range(bb):
        kc_ref[i] = kc[i * n_chunks:(i + 1) * n_chunks]
        vct_ref[i] = vct[:, i * n_chunks:(i + 1) * n_chunks]


def _compress(xk, xv, w1k, w1v, posk, posv, w2k, w2vt, bb=4):
    bsz, n_chunks, cw = xk.shape
    hw2 = w1k.shape[1]
    out = jax.ShapeDtypeStruct((bsz, n_chunks, LANES), jnp.bfloat16)
    x_spec = pl.BlockSpec((bb, n_chunks, cw), lambda b: (b, 0, 0))
    w1_spec = pl.BlockSpec((cw, hw2), lambda b: (0, 0))
    pos_spec = pl.BlockSpec((SUBLANES, cw), lambda b: (0, 0))
    o_spec = pl.BlockSpec((bb, n_chunks, LANES), lambda b: (b, 0, 0))
    return pl.pallas_call(
        _compress_kernel,
        out_shape=[out, out],
        grid=(bsz // bb,),
        in_specs=[x_spec, x_spec, w1_spec, w1_spec, pos_spec, pos_spec,
                  pl.BlockSpec(w2k.shape, lambda b: (0, 0)), pl.BlockSpec(w2vt.shape, lambda b: (0, 0))],
        out_specs=[o_spec, o_spec],
        compiler_params=_cparams(("arbitrary",)),
        name="compress",
    )(xk, xv, w1k, w1v, posk, posv, w2k, w2vt)


def _stack_heads(q):
    lane = lax.broadcasted_iota(jnp.int32, (1, LANES), 1)
    is_a = (lane & (HEAD_DIM - 1)) < HALF
    zero = jnp.zeros((), q.dtype)
    tiles = [q[:, j * LANES:(j + 1) * LANES] for j in range(GROUP)]
    parts = [jnp.where(is_a, t, zero) for t in tiles] + [jnp.where(is_a, zero, t) for t in tiles]
    return jnp.concatenate(parts, axis=0)


def _store_heads(o_ref, rows, o_g, inv_l, tq):
    for j in range(GROUP):
        cols = slice(j * tq, (j + 1) * tq)
        halves = [o_g[g][0:HEAD_DIM, cols] for g in range(N_KV)]
        if inv_l is not None:
            halves = [h * inv_l[g][:, cols] for g, h in enumerate(halves)]
        tile_t = jnp.concatenate(halves, axis=0)
        o_ref[0, rows, j * LANES:(j + 1) * LANES] = tile_t.T.astype(jnp.bfloat16)


def _branch_gates(gn, branch, tq):
    sig = _sigmoid(gn.T[0:3 * N_HEADS, :])
    return [jnp.concatenate([sig[3 * (GROUP * g + j) + branch:3 * (GROUP * g + j) + branch + 1, :]
                             for j in range(GROUP)], axis=1) for g in range(N_KV)]


def _group_cols(g, tq):
    return slice(g * GROUP * tq, (g + 1) * GROUP * tq)


def _cmp_kernel(q_ref, gn_ref, kc_ref, vct_ref, ovt_ref, o_ref, bias_ref, *, tq, n_cmp, n_blk):
    qi = pl.program_id(1)
    nq = tq // LANES
    qs = _stack_heads(q_ref[0])
    c_row = lax.broadcasted_iota(jnp.int32, (LANES, tq), 0)
    t_col = qi * tq + lax.broadcasted_iota(jnp.int32, (LANES, tq), 1)
    vis = (c_row * CMP_STRIDE + (CMP_LEN - 1) <= t_col) & (c_row < n_cmp)
    any_vis = (t_col[0:1, :] >= CMP_LEN - 1).astype(jnp.float32)
    s_t = _dot_nt(kc_ref[0], qs)
    p_parts = []
    for cb in range(N_HEADS * nq):
        qq = slice((cb % nq) * LANES, (cb % nq + 1) * LANES)
        s = jnp.where(vis[:, qq], s_t[:, cb * LANES:(cb + 1) * LANES], NEG_INF)
        e = jnp.exp2(s - jnp.max(s, axis=0, keepdims=True))
        p_parts.append(e * (any_vis[:, qq] / jnp.sum(e, axis=0, keepdims=True)))
    p_t = jnp.concatenate(p_parts, axis=1)
    o_g = [_dot(vct_ref[0, g * HEAD_DIM:(g + 1) * HEAD_DIM, :], p_t[:, _group_cols(g, tq)].astype(jnp.bfloat16))
           for g in range(N_KV)]
    _store_heads(o_ref, slice(None), o_g, _branch_gates(gn_ref[0], 0, tq), tq)

    j_blk = lax.broadcasted_iota(jnp.int32, (n_blk, tq), 0)
    t_q = qi * tq + lax.broadcasted_iota(jnp.int32, (n_blk, tq), 1)
    cur = jnp.right_shift(t_q, SEL_BLOCK.bit_length() - 1)
    forced = (j_blk == 0) | (j_blk == cur) | (j_blk == cur - 1)
    causal = j_blk <= cur
    n_free = SEL_TOPN - 3
    take_all = cur < SEL_TOPN
    candidate = causal & jnp.logical_not(forced)
    rows = []
    for g in range(N_KV):
        pg = p_t[:, g * GROUP * tq:(g * GROUP + 1) * tq]
        for h in range(1, GROUP):
            pg = pg + p_t[:, (g * GROUP + h) * tq:(g * GROUP + h + 1) * tq]
        hi = pg.astype(jnp.bfloat16)
        lo = (pg - hi.astype(jnp.float32)).astype(jnp.bfloat16)
        imp_t = _dot(ovt_ref[...], hi) + _dot(ovt_ref[...], lo)
        work = jnp.where(candidate, imp_t[0:n_blk, :], NEG_INF)
        picked = jnp.zeros((n_blk, tq), jnp.int32)
        for _ in range(n_free):
            top = jnp.max(work, axis=0, keepdims=True)
            first = jnp.min(jnp.where(work == top, j_blk, n_blk), axis=0, keepdims=True)
            hit = j_blk == first
            picked = jnp.where(hit, 1, picked)
            work = jnp.where(hit, NEG_INF, work)
        keep = causal & (take_all | forced | ((picked > 0) & candidate))
        rows.append(jnp.where(keep, 0.0, NEG_INF))
    rows.append(jnp.zeros((LANES - N_KV * n_blk, tq), jnp.float32))
    bias_t = jnp.concatenate(rows, axis=0)
    bias_ref[0] = bias_t.T.astype(jnp.bfloat16)


def _cmp_attn(qa, g_nsa, kc, vct, ovt, n_cmp, tq=2048):
    bsz, seq, _ = qa.shape
    n_blk = seq // SEL_BLOCK
    kern = functools.partial(_cmp_kernel, tq=tq, n_cmp=n_cmp, n_blk=n_blk)
    kv_spec = pl.BlockSpec((1, LANES, LANES), lambda b, i: (b, 0, 0))
    return pl.pallas_call(
        kern,
        out_shape=[jax.ShapeDtypeStruct((bsz, seq, Q_WIDTH), jnp.bfloat16),
                   jax.ShapeDtypeStruct((bsz, seq, LANES), jnp.bfloat16)],
        grid=(bsz, seq // tq),
        in_specs=[pl.BlockSpec((1, tq, Q_WIDTH), lambda b, i: (b, i, 0)),
                  pl.BlockSpec((1, tq, LANES), lambda b, i: (b, i, 0)),
                  kv_spec, kv_spec,
                  pl.BlockSpec((LANES, LANES), lambda b, i: (0, 0))],
        out_specs=[pl.BlockSpec((1, tq, Q_WIDTH), lambda b, i: (b, i, 0)),
                   pl.BlockSpec((1, tq, LANES), lambda b, i: (b, i, 0))],
        compiler_params=_cparams(("arbitrary", "arbitrary")),
        name="cmp_attn",
    )(qa, g_nsa, kc, vct, ovt)


def _sel_kernel(q_ref, gn_ref, bias_ref, k_ref, vt_ref, blk_ref, o_ref, kaug_ref, qaug_ref, s0_ref, s1_ref, acc_ref, m_ref,
                *, tq, tk):
    step = pl.program_id(1)
    nq = tq // LANES
    nk = tk // LANES
    per_g = GROUP * nq

    @pl.when(step == 0)
    def _():
        kaug_ref[:, 0:LANES] = k_ref[0]
        kaug_ref[:, LANES:2 * LANES] = blk_ref[...]

    lane = lax.broadcasted_iota(jnp.int32, (1, LANES), 1)
    n_blk = blk_ref.shape[0] // SEL_BLOCK
    bt = bias_ref[0]
    zero = jnp.zeros((), bt.dtype)
    b0 = jnp.where(lane < n_blk, bt, zero)
    b1 = jnp.where((lane >= n_blk) & (lane < 2 * n_blk), bt, zero)
    qaug_ref[:, 0:LANES] = _stack_heads(q_ref[0])
    qaug_ref[:, LANES:2 * LANES] = jnp.concatenate([b0] * GROUP + [b1] * GROUP, axis=0)

    acc_ref[...] = jnp.zeros_like(acc_ref)
    m_ref[...] = jnp.full_like(m_ref, NEG_INF)
    row = lax.broadcasted_iota(jnp.int32, (tk, tk), 0)
    col = lax.broadcasted_iota(jnp.int32, (tk, tk), 1)
    causal_bias = jnp.where(row <= col, 0.0, NEG_INF)

    def scores(c, s_ref):
        k0 = pl.multiple_of(c * tk, tk)
        s_ref[...] = _dot_nt(kaug_ref[pl.ds(k0, tk), :], qaug_ref[...])

    def accumulate(c, s_ref, diag_half):
        k0 = pl.multiple_of(c * tk, tk)
        for g in range(N_KV):
            ps, alphas = [], []
            for i in range(per_g):
                cb = g * per_g + i
                half, within = divmod(cb % nq, nk)
                cols = slice(cb * LANES, (cb + 1) * LANES)
                if diag_half is not None and half < diag_half:
                    ps.append(jnp.zeros((tk, LANES), jnp.bfloat16))
                    alphas.append(jnp.ones((1, LANES), jnp.float32))
                    continue
                s = s_ref[:, cols]
                if diag_half is not None and half == diag_half:
                    s = s + causal_bias[:, within * LANES:(within + 1) * LANES]
                m_old = m_ref[0:1, cols]
                m_new = jnp.maximum(m_old, jnp.max(s, axis=0, keepdims=True))
                m_ref[0:1, cols] = m_new
                alphas.append(jnp.exp2(m_old - m_new))
                ps.append(jnp.exp2(s - m_new).astype(jnp.bfloat16))
            vt_g = vt_ref[0, g * G_ROWS:(g + 1) * G_ROWS, pl.ds(k0, tk)]
            pv = _dot(vt_g, jnp.concatenate(ps, axis=1))
            acc_ref[g] = jnp.concatenate(alphas, axis=1) * acc_ref[g] + pv

    scores(0, s0_ref)

    def body(i, carry):
        c = 2 * i
        scores(c + 1, s1_ref)
        accumulate(c, s0_ref, None)
        scores(c + 2, s0_ref)
        accumulate(c + 1, s1_ref, None)
        return carry

    lax.fori_loop(0, step, body, 0)
    c0 = 2 * step
    scores(c0 + 1, s1_ref)
    accumulate(c0, s0_ref, 0)
    accumulate(c0 + 1, s1_ref, 1)

    o_g = [acc_ref[g] for g in range(N_KV)]
    gates = _branch_gates(gn_ref[0], 1, tq)
    scale = [gates[g] / o_g[g][HEAD_DIM:HEAD_DIM + 1, :] for g in range(N_KV)]
    _store_heads(o_ref, slice(None), o_g, scale, tq)


def _sel_attn(qa, g_nsa, bias, ks, vst, blk_onehot, tk=256):
    bsz, seq, _ = qa.shape
    tq = 2 * tk
    kern = functools.partial(_sel_kernel, tq=tq, tk=tk)
    return pl.pallas_call(
        kern,
        out_shape=jax.ShapeDtypeStruct((bsz, seq, Q_WIDTH), jnp.bfloat16),
        grid=(bsz, seq // tq),
        in_specs=[pl.BlockSpec((1, tq, Q_WIDTH), lambda b, i: (b, i, 0)),
                  pl.BlockSpec((1, tq, LANES), lambda b, i: (b, i, 0)),
                  pl.BlockSpec((1, tq, LANES), lambda b, i: (b, i, 0)),
                  pl.BlockSpec((1, seq, LANES), lambda b, i: (b, 0, 0)),
                  pl.BlockSpec((1, V_ROWS, seq), lambda b, i: (b, 0, 0)),
                  pl.BlockSpec((seq, LANES), lambda b, i: (0, 0))],
        out_specs=pl.BlockSpec((1, tq, Q_WIDTH), lambda b, i: (b, i, 0)),
        scratch_shapes=[pltpu.VMEM((seq, 2 * LANES), jnp.bfloat16),
                        pltpu.VMEM((N_HEADS * tq, 2 * LANES), jnp.bfloat16),
                        pltpu.VMEM((tk, N_HEADS * tq), jnp.float32),
                        pltpu.VMEM((tk, N_HEADS * tq), jnp.float32),
                        pltpu.VMEM((N_KV, G_ROWS, GROUP * tq), jnp.float32),
                        pltpu.VMEM((SUBLANES, N_HEADS * tq), jnp.float32)],
        compiler_params=_cparams(("arbitrary", "arbitrary")),
        name="sel_attn",
    )(qa, g_nsa, bias, ks, vst, blk_onehot)


def _band_kernel(*refs, tq, n_sub, span, window, use_sink):
    if use_sink:
        q_ref, k_ref, vt_ref, sink_ref, o_ref = refs
    else:
        q_ref, k_ref, vt_ref, gn_ref, o_ref = refs
    step = pl.program_id(1)
    seq = k_ref.shape[1]
    nq = tq // LANES
    row = lax.broadcasted_iota(jnp.int32, (span, tq), 0)
    col = lax.broadcasted_iota(jnp.int32, (span, tq), 1)
    diff = col - row

    def scores(sub):
        q0 = (step * n_sub + sub) * tq
        start = pl.multiple_of(jnp.clip(q0 + tq - span, 0, seq - span), LANES)
        rel = diff + (q0 - start)
        bias = jnp.where((rel >= 0) & (rel < window), 0.0, NEG_INF)
        qs = _stack_heads(q_ref[0, sub * tq:(sub + 1) * tq, :])
        return start, bias, _dot_nt(k_ref[0, pl.ds(start, span), :], qs)

    def finish(sub, start, bias, s_t):
        ps, extra = [], []
        for cb in range(N_HEADS * nq):
            s = s_t[:, cb * LANES:(cb + 1) * LANES] + bias[:, (cb % nq) * LANES:(cb % nq + 1) * LANES]
            m = jnp.max(s, axis=0, keepdims=True)
            if use_sink:
                sink = sink_ref[cb // nq] * LOG2E
                m = jnp.maximum(m, sink)
                extra.append(jnp.exp2(sink - m))
            ps.append(jnp.exp2(s - m).astype(jnp.bfloat16))
        o_g, inv_l = [], []
        for g in range(N_KV):
            per_g = GROUP * nq
            vt_g = vt_ref[0, g * G_ROWS:(g + 1) * G_ROWS, pl.ds(start, span)]
            o = _dot(vt_g, jnp.concatenate(ps[g * per_g:(g + 1) * per_g], axis=1))
            l = o[HEAD_DIM:HEAD_DIM + 1, :]
            if use_sink:
                l = l + jnp.concatenate(extra[g * per_g:(g + 1) * per_g], axis=1)
            o_g.append(o)
            inv_l.append(1.0 / l)
        if not use_sink:
            gates = _branch_gates(gn_ref[0, sub * tq:(sub + 1) * tq, :], 2, tq)
            inv_l = [a * b for a, b in zip(inv_l, gates)]
        _store_heads(o_ref, slice(sub * tq, (sub + 1) * tq), o_g, inv_l, tq)

    nxt = scores(0)
    for sub in range(n_sub):
        cur = nxt
        if sub + 1 < n_sub:
            nxt = scores(sub + 1)
        finish(sub, *cur)


def _band_attn(q, k, vt, window, sinks=None, g_nsa=None, tq=128, n_sub=16):
    bsz, seq, _ = q.shape
    span = (-(-(window - 1) // tq)) * tq + tq
    use_sink = sinks is not None
    kern = functools.partial(_band_kernel, tq=tq, n_sub=n_sub, span=span, window=window, use_sink=use_sink)
    ts = tq * n_sub
    in_specs = [pl.BlockSpec((1, ts, Q_WIDTH), lambda b, i: (b, i, 0)),
                pl.BlockSpec((1, seq, LANES), lambda b, i: (b, 0, 0)),
                pl.BlockSpec((1, V_ROWS, seq), lambda b, i: (b, 0, 0))]
    args = [q, k, vt]
    if use_sink:
        in_specs.append(pl.BlockSpec(memory_space=pltpu.SMEM))
        args.append(sinks)
    else:
        in_specs.append(pl.BlockSpec((1, ts, LANES), lambda b, i: (b, i, 0)))
        args.append(g_nsa)
    return pl.pallas_call(
        kern,
        out_shape=jax.ShapeDtypeStruct((bsz, seq, Q_WIDTH), jnp.bfloat16),
        grid=(bsz, seq // ts),
        in_specs=in_specs,
        out_specs=pl.BlockSpec((1, ts, Q_WIDTH), lambda b, i: (b, i, 0)),
        compiler_params=_cparams(("arbitrary", "arbitrary")),
        name="band_attn_sink" if use_sink else "band_attn",
    )(*args)


def _merge_kernel(x_ref, mod_ref, wza_ref, wzb_ref, wgm_ref, ocmp_ref, osel_ref, owin_ref, ob_ref,
                  wua_ref, wub_ref, wo_ref, lng_ref, lnb_ref, o_ref, *, ts, n_sub):
    shift = mod_ref[0, 0:1, :]
    scale = mod_ref[0, 1:2, :]
    gate = mod_ref[0, 2:3, :]

    def project(r):
        u = (_layer_norm(x_ref[0, r * ts:(r + 1) * ts, :]) * (1.0 + scale) + shift).astype(jnp.bfloat16)
        return _dot(u, wza_ref[...]), _dot(u, wzb_ref[...]), _dot(u, wgm_ref[...])

    def finish(r, z_a, z_b, g_m):
        rows = slice(r * ts, (r + 1) * ts)
        o_a = (ocmp_ref[0, rows, :].astype(jnp.float32) + osel_ref[0, rows, :].astype(jnp.float32)
               + owin_ref[0, rows, :].astype(jnp.float32))
        y_a = (o_a * (z_a * _sigmoid(z_a))).astype(jnp.bfloat16)
        y_b = (ob_ref[0, rows, :].astype(jnp.float32) * (z_b * _sigmoid(z_b))).astype(jnp.bfloat16)
        merged = _sigmoid(g_m[:, 0:D_MODEL]) * _dot(y_a, wua_ref[...])
        merged = merged + _sigmoid(g_m[:, D_MODEL:2 * D_MODEL]) * _dot(y_b, wub_ref[...])
        out = _dot(merged.astype(jnp.bfloat16), wo_ref[...])
        y = DEEPNORM_ALPHA * x_ref[0, rows, :] + gate * out
        o_ref[0, rows, :] = _layer_norm(y) * lng_ref[...] + lnb_ref[...]

    nxt = project(0)
    for r in range(n_sub):
        cur = nxt
        if r + 1 < n_sub:
            nxt = project(r + 1)
        finish(r, *cur)


def _merge_out(x, mod3, wgs, o_cmp, o_sel, o_win, o_b, wua, wub, wo, ln_g, ln_b, ts=256, n_sub=4):
    bsz, seq, _ = x.shape
    tm = ts * n_sub
    tok = lambda w: pl.BlockSpec((1, tm, w), lambda b, i: (b, i, 0))
    full = lambda a: pl.BlockSpec(a.shape, lambda b, i: (0,) * a.ndim)
    return pl.pallas_call(
        functools.partial(_merge_kernel, ts=ts, n_sub=n_sub),
        out_shape=jax.ShapeDtypeStruct(x.shape, jnp.float32),
        grid=(bsz, seq // tm),
        in_specs=[tok(D_MODEL), pl.BlockSpec((1, 3, D_MODEL), lambda b, i: (b, 0, 0)),
                  *[full(w) for w in wgs], tok(Q_WIDTH), tok(Q_WIDTH), tok(Q_WIDTH), tok(Q_WIDTH),
                  full(wua), full(wub), full(wo), full(ln_g), full(ln_b)],
        out_specs=tok(D_MODEL),
        compiler_params=_cparams(("arbitrary", "arbitrary")),
        name="merge_out",
    )(x, mod3, *wgs, o_cmp, o_sel, o_win, o_b, wua, wub, wo, ln_g, ln_b)


def _pair_q_cols(w):
    k = w.shape[0]
    return w.reshape(k, N_KV, GROUP, 2, HALF).transpose(0, 2, 3, 1, 4).reshape(k, Q_WIDTH)


def _pair_k_cols(w):
    k = w.shape[0]
    return w.reshape(k, N_KV, 2, HALF).transpose(0, 2, 1, 3).reshape(k, KV_WIDTH)


def _pair_o_cols(w):
    k = w.shape[0]
    return w.reshape(k, N_KV, GROUP, HEAD_DIM).transpose(0, 2, 1, 3).reshape(k, Q_WIDTH)


def _pair_o_rows(w):
    n = w.shape[1]
    return w.reshape(N_KV, GROUP, HEAD_DIM, n).transpose(1, 0, 2, 3).reshape(Q_WIDTH, n)


def _prep_qkv_weight(w_in):
    q_scale = HEAD_DIM ** -0.5
    sec = lambda off, width: w_in[:, off:off + width]
    cols = [_pair_q_cols(sec(OFF_QA, Q_WIDTH)) * q_scale,
            _pair_k_cols(sec(OFF_KVC, KV_WIDTH)), _pair_k_cols(sec(OFF_KVS, KV_WIDTH)),
            _pair_k_cols(sec(OFF_KVW, KV_WIDTH)),
            _pair_q_cols(sec(OFF_QB, Q_WIDTH)) * q_scale,
            _pair_k_cols(sec(OFF_KVB, KV_WIDTH)),
            sec(OFF_KVC + KV_WIDTH, KV_WIDTH),
            sec(OFF_GN, 3 * N_HEADS), jnp.zeros((D_MODEL, LANES - 3 * N_HEADS), w_in.dtype)]
    v_cols = [sec(OFF_KVS + KV_WIDTH, KV_WIDTH), sec(OFF_KVW + KV_WIDTH, KV_WIDTH),
              sec(OFF_KVB + KV_WIDTH, KV_WIDTH)]
    return (jnp.concatenate(cols, axis=1).astype(jnp.bfloat16),
            jnp.concatenate(v_cols, axis=1).T.astype(jnp.bfloat16))


def _prep_gate_weights(w_in):
    ws = [_pair_o_cols(w_in[:, OFF_ZA:OFF_ZA + Q_WIDTH]), _pair_o_cols(w_in[:, OFF_ZB:OFF_ZB + Q_WIDTH]),
          w_in[:, OFF_GM:OFF_GM + 2 * D_MODEL]]
    return [w.astype(jnp.bfloat16) for w in ws]


def _prep_compress(pos, w1, w2, paired):
    half_len = CMP_LEN // 2
    w = w1.reshape(CMP_LEN, HEAD_DIM, CMP_HIDDEN).astype(jnp.bfloat16)
    w2 = w2.astype(jnp.bfloat16)
    pieces = ([(g, slice(h * HALF, (h + 1) * HALF)) for h in range(2) for g in range(N_KV)] if paired
              else [(g, slice(0, HEAD_DIM)) for g in range(N_KV)])
    own = lambda g, blk: jnp.concatenate([blk if k == g else jnp.zeros_like(blk) for k in range(N_KV)], axis=-1)
    w1x = jnp.concatenate([own(g, w[:, d, :]) for g, d in pieces], axis=1)
    first = w1x[:half_len].reshape(half_len * LANES, N_KV * CMP_HIDDEN)
    second = w1x[half_len:].reshape(half_len * LANES, N_KV * CMP_HIDDEN)
    w1e = jnp.concatenate([first, second], axis=1)
    w2e = jnp.concatenate([jnp.concatenate([w2[:, d] if k == g else jnp.zeros_like(w2[:, d]) for g, d in pieces], axis=1)
                           for k in range(N_KV)], axis=0)
    pos_l = jnp.concatenate([pos[:, d] for _, d in pieces], axis=1)
    pos_rows = jnp.concatenate([pos_l[:half_len].reshape(1, -1), pos_l[half_len:].reshape(1, -1),
                                jnp.zeros((SUBLANES - 2, half_len * LANES), pos.dtype)], axis=0).astype(jnp.bfloat16)
    return w1e, pos_rows, w2e


def _overlap_t(n_cmp, n_blk):
    c0 = np.arange(n_cmp) * CMP_STRIDE
    j0 = np.arange(n_blk) * SEL_BLOCK
    overlap = (c0[:, None] < j0[None, :] + SEL_BLOCK) & (c0[:, None] + CMP_LEN > j0[None, :])
    ovt = np.zeros((LANES, LANES), np.float32)
    ovt[:n_blk, :n_cmp] = overlap.T
    return jnp.asarray(ovt, jnp.bfloat16)


def _block_onehot(seq):
    n_blk = seq // SEL_BLOCK
    e = np.zeros((seq, LANES), np.float32)
    pos = np.arange(seq)
    e[pos, pos // SEL_BLOCK] = 1.0
    e[pos, n_blk + pos // SEL_BLOCK] = 1.0
    return jnp.asarray(e, jnp.bfloat16)


def kernel(x, c, positions, w_ada, b_ada, w_in, cmp_pos_k, cmp_w1_k, cmp_w2_k,
           cmp_pos_v, cmp_w1_v, cmp_w2_v, sinks, w_up_a, w_up_b, w_out, ln_g, ln_b):
    bsz, seq, _ = x.shape
    n_cmp = (seq - CMP_LEN) // CMP_STRIDE + 1
    n_blk = seq // SEL_BLOCK
    assert seq // CMP_STRIDE == LANES and 2 * n_blk <= LANES
    cos_t, sin_t = _rope_tables(positions, QKV_ROWS)
    for l in range(DEPTH):
        mod3 = _adaln_mod(c, w_ada[l], b_ada[l]).reshape(bsz, 3, D_MODEL)
        w_qk, w_vt = _prep_qkv_weight(w_in[l])
        qa, k_c, k_s, k_w, qb, k_b, v_c, vt_s, vt_w, vt_b, g_nsa = _qkv_proj(x, mod3, w_qk, w_vt, cos_t, sin_t)

        w1k, posk, w2k = _prep_compress(cmp_pos_k[l], cmp_w1_k[l], cmp_w2_k[l], paired=True)
        w1v, posv, w2v = _prep_compress(cmp_pos_v[l], cmp_w1_v[l], cmp_w2_v[l], paired=False)
        kc, vct = _compress(k_c, v_c, w1k, w1v, posk, posv, w2k, w2v.T, bb=4 if bsz % 4 == 0 else 1)

        o_cmp, sel_bias = _cmp_attn(qa, g_nsa, kc, vct, _overlap_t(n_cmp, n_blk), n_cmp)
        o_sel = _sel_attn(qa, g_nsa, sel_bias, k_s, vt_s, _block_onehot(seq))
        o_win = _band_attn(qa, k_w, vt_w, NSA_WINDOW, g_nsa=g_nsa)
        o_b = _band_attn(qb, k_b, vt_b, SWA_WINDOW, sinks[l].astype(jnp.float32))

        x = _merge_out(x, mod3, _prep_gate_weights(w_in[l]),
                       o_cmp, o_sel, o_win, o_b,
                       _pair_o_rows(w_up_a[l]).astype(jnp.bfloat16), _pair_o_rows(w_up_b[l]).astype(jnp.bfloat16),
                       w_out[l].astype(jnp.bfloat16), ln_g[l].reshape(1, D_MODEL), ln_b[l].reshape(1, D_MODEL))
    return x
```

```python
import functools

import numpy as np
import jax
import jax.numpy as jnp
from jax import lax
from jax.experimental import pallas as pl
from jax.experimental.pallas import tpu as pltpu

D_MODEL = 1024
HEAD_DIM = 64
HALF = HEAD_DIM // 2
N_HEADS = 8
N_KV = 2
GROUP = N_HEADS // N_KV
Q_WIDTH = N_HEADS * HEAD_DIM
KV_WIDTH = N_KV * HEAD_DIM
CMP_LEN = 32
CMP_STRIDE = 16
CMP_HIDDEN = 256
SEL_BLOCK = 64
SEL_TOPN = 8
NSA_WINDOW = 512
SWA_WINDOW = 128
ROPE_THETA = 10000.0
LN_EPS = 1e-5
NEG_INF = -1e30
FORCE_SCORE = 1e9
DEPTH = 1
DEEPNORM_ALPHA = (2 * DEPTH) ** 0.25
LANES = 128
SUBLANES = 8
BF16_SUBLANES = 16
ROPE_PACK = LANES // HALF
QKV_ROWS = 256
LOG2E = 1.4426950408889634
G_ROWS = HEAD_DIM + BF16_SUBLANES
V_ROWS = N_KV * G_ROWS
VMEM_LIMIT = 56 * 1024 * 1024

_SPLITS = [Q_WIDTH, 2 * KV_WIDTH, 2 * KV_WIDTH, 2 * KV_WIDTH, 3 * N_HEADS, Q_WIDTH,
           Q_WIDTH, 2 * KV_WIDTH, Q_WIDTH, 2 * D_MODEL]
_OFFS = [int(o) for o in np.cumsum([0] + _SPLITS)]
(OFF_QA, OFF_KVC, OFF_KVS, OFF_KVW, OFF_GN, OFF_ZA, OFF_QB, OFF_KVB, OFF_ZB, OFF_GM) = _OFFS[:10]


def _q_pair_perm():
    cols = []
    for j in range(GROUP):
        for half in range(2):
            for grp in range(N_KV):
                h = j + GROUP * grp
                cols += [h * HEAD_DIM + half * HALF + i for i in range(HALF)]
    return np.asarray(cols, np.int32)


def _k_pair_perm():
    cols = []
    for half in range(2):
        for grp in range(N_KV):
            cols += [grp * HEAD_DIM + half * HALF + i for i in range(HALF)]
    return np.asarray(cols, np.int32)


def _o_pair_perm():
    cols = []
    for j in range(GROUP):
        for grp in range(N_KV):
            h = j + GROUP * grp
            cols += [h * HEAD_DIM + d for d in range(HEAD_DIM)]
    return np.asarray(cols, np.int32)


Q_PERM = _q_pair_perm()
K_PERM = _k_pair_perm()
O_PERM = _o_pair_perm()


def _cparams(sem):
    return pltpu.CompilerParams(dimension_semantics=sem, vmem_limit_bytes=VMEM_LIMIT)


def _dot(a, b):
    return jnp.dot(a, b, preferred_element_type=jnp.float32)


def _dot_nt(a, b):
    return lax.dot_general(a, b, (((1,), (1,)), ((), ())), preferred_element_type=jnp.float32)


def _layer_norm(x):
    mu = jnp.mean(x, axis=-1, keepdims=True)
    xc = x - mu
    var = jnp.mean(xc * xc, axis=-1, keepdims=True)
    return xc * lax.rsqrt(var + LN_EPS)


def _sigmoid(x):
    return 1.0 / (1.0 + jnp.exp(-x))


def _adaln_kernel(c_ref, w_ref, b_ref, o_ref):
    c = c_ref[...]
    a = c * _sigmoid(c)
    o_ref[...] = jnp.dot(a, w_ref[...], preferred_element_type=jnp.float32,
                         precision=lax.Precision.HIGHEST) + b_ref[...]


def _adaln_mod(c, w_ada, b_ada):
    bsz = c.shape[0]
    n = w_ada.shape[1]
    tn = 512
    return pl.pallas_call(
        _adaln_kernel,
        out_shape=jax.ShapeDtypeStruct((bsz, n), jnp.float32),
        grid=(n // tn,),
        in_specs=[pl.BlockSpec((bsz, D_MODEL), lambda j: (0, 0)),
                  pl.BlockSpec((D_MODEL, tn), lambda j: (0, j)),
                  pl.BlockSpec((1, tn), lambda j: (0, j))],
        out_specs=pl.BlockSpec((bsz, tn), lambda j: (0, j)),
        compiler_params=_cparams(("arbitrary",)),
        name="adaln_mod",
    )(c, w_ada, b_ada.reshape(1, n))


def _rope_table_kernel(pos_ref, invf_ref, cos_ref, sin_ref):
    pos = pos_ref[...]
    rows = pos.shape[0]
    ang = jnp.concatenate([jnp.broadcast_to(pos[:, k:k + 1], (rows, HALF)) for k in range(ROPE_PACK)],
                          axis=1) * invf_ref[...]
    cos_ref[...] = jnp.cos(ang)
    sin_ref[...] = jnp.sin(ang)


def _rope_tables(positions, ts):
    bsz, seq = positions.shape
    sub = ts // ROPE_PACK
    rows = bsz * seq // ROPE_PACK
    pos = positions.astype(jnp.float32).reshape(bsz, seq // ts, ROPE_PACK, sub).transpose(0, 1, 3, 2)
    pos = pos.reshape(rows, ROPE_PACK)
    inv_freq = ROPE_THETA ** (-jnp.arange(HALF, dtype=jnp.float32) / HALF)
    invf = jnp.broadcast_to(inv_freq[None, :], (ROPE_PACK, HALF)).reshape(1, LANES)
    tr = seq // ROPE_PACK
    cos, sin = pl.pallas_call(
        _rope_table_kernel,
        out_shape=[jax.ShapeDtypeStruct((rows, LANES), jnp.float32)] * 2,
        grid=(rows // tr,),
        in_specs=[pl.BlockSpec((tr, ROPE_PACK), lambda i: (i, 0)),
                  pl.BlockSpec((1, LANES), lambda i: (0, 0))],
        out_specs=[pl.BlockSpec((tr, LANES), lambda i: (i, 0))] * 2,
        compiler_params=_cparams(("arbitrary",)),
        name="rope_table",
    )(pos, invf)
    return cos.reshape(bsz, tr, LANES), sin.reshape(bsz, tr, LANES)


N_QK_SLABS = 14
SLAB_QA, SLAB_KC, SLAB_KS, SLAB_KW, SLAB_QB, SLAB_KB, SLAB_VC, SLAB_GN = 0, 4, 5, 6, 7, 11, 12, 13
N_VT = 3


def _qkv_kernel(x_ref, mod_ref, w_ref, wvt_ref, cos_ref, sin_ref,
                qa_ref, kc_ref, ks_ref, kw_ref, qb_ref, kb_ref, vc_ref,
                vst_ref, vwt_ref, vbt_ref, gn_ref, kstage_ref, vstage_ref, *, ts, n_sub):
    shift = mod_ref[0, 0:1, :]
    scale = mod_ref[0, 1:2, :]

    def project(r):
        u = (_layer_norm(x_ref[0, r * ts:(r + 1) * ts, :]) * (1.0 + scale) + shift).astype(jnp.bfloat16)
        return _dot(u, w_ref[...]), _dot_nt(wvt_ref[...], u)

    def finish(r, acc, vt):
        rows = slice(r * ts, (r + 1) * ts)
        sub = ts // ROPE_PACK
        for k in range(ROPE_PACK):
            c32 = cos_ref[0, r * sub:(r + 1) * sub, k * HALF:(k + 1) * HALF]
            s32 = sin_ref[0, r * sub:(r + 1) * sub, k * HALF:(k + 1) * HALF]
            cos = jnp.concatenate([c32, c32, c32, c32], axis=1)
            sin = jnp.concatenate([-s32, -s32, s32, s32], axis=1)
            tok = slice(r * ts + k * sub, r * ts + (k + 1) * sub)

            def rope(slab):
                t = acc[k * sub:(k + 1) * sub, slab * LANES:(slab + 1) * LANES]
                return t * cos + pltpu.roll(t, LANES // 2, 1) * sin

            for j in range(GROUP):
                qa_ref[0, tok, j * LANES:(j + 1) * LANES] = (rope(SLAB_QA + j) * LOG2E).astype(jnp.bfloat16)
                qb_ref[0, tok, j * LANES:(j + 1) * LANES] = (rope(SLAB_QB + j) * LOG2E).astype(jnp.bfloat16)
            for slab, ref in ((SLAB_KS, ks_ref), (SLAB_KW, kw_ref), (SLAB_KB, kb_ref)):
                ref[0, tok, :] = rope(slab).astype(jnp.bfloat16)
            kstage_ref[k * sub:(k + 1) * sub, :] = rope(SLAB_KC)
        vstage_ref[...] = acc[:, SLAB_VC * LANES:(SLAB_VC + 1) * LANES]
        gn_ref[0, rows, :] = acc[:, SLAB_GN * LANES:(SLAB_GN + 1) * LANES]
        n_ch = ts // CMP_STRIDE
        for l in range(CMP_STRIDE):
            for stage, ref in ((kstage_ref, kc_ref), (vstage_ref, vc_ref)):
                ref[0, r * n_ch:(r + 1) * n_ch, l * LANES:(l + 1) * LANES] = (
                    stage[pl.ds(l, n_ch, stride=CMP_STRIDE), :].astype(jnp.bfloat16))
        ones = jnp.ones((G_ROWS - HEAD_DIM, ts), jnp.bfloat16)
        for j, ref in enumerate((vst_ref, vwt_ref, vbt_ref)):
            for g in range(N_KV):
                lo = j * KV_WIDTH + g * HEAD_DIM
                ref[0, g * G_ROWS:g * G_ROWS + HEAD_DIM, rows] = vt[lo:lo + HEAD_DIM].astype(jnp.bfloat16)
                ref[0, g * G_ROWS + HEAD_DIM:(g + 1) * G_ROWS, rows] = ones

    nxt = project(0)
    for r in range(n_sub):
        cur = nxt
        if r + 1 < n_sub:
            nxt = project(r + 1)
        finish(r, *cur)


def _qkv_proj(x, mod3, w_qk, w_vt, cos_t, sin_t, ts=QKV_ROWS, n_sub=8):
    bsz, seq, _ = x.shape
    tm = ts * n_sub
    chunk_w = CMP_STRIDE * KV_WIDTH
    q_shape = jax.ShapeDtypeStruct((bsz, seq, Q_WIDTH), jnp.bfloat16)
    k_shape = jax.ShapeDtypeStruct((bsz, seq, KV_WIDTH), jnp.bfloat16)
    c_shape = jax.ShapeDtypeStruct((bsz, seq // CMP_STRIDE, chunk_w), jnp.bfloat16)
    vt_shape = jax.ShapeDtypeStruct((bsz, V_ROWS, seq), jnp.bfloat16)
    q_spec = pl.BlockSpec((1, tm, Q_WIDTH), lambda b, i: (b, i, 0))
    k_spec = pl.BlockSpec((1, tm, KV_WIDTH), lambda b, i: (b, i, 0))
    c_spec = pl.BlockSpec((1, tm // CMP_STRIDE, chunk_w), lambda b, i: (b, i, 0))
    t_spec = pl.BlockSpec((1, tm // ROPE_PACK, LANES), lambda b, i: (b, i, 0))
    vt_spec = pl.BlockSpec((1, V_ROWS, tm), lambda b, i: (b, 0, i))
    return pl.pallas_call(
        functools.partial(_qkv_kernel, ts=ts, n_sub=n_sub),
        out_shape=[q_shape, c_shape, k_shape, k_shape, q_shape, k_shape, c_shape,
                   vt_shape, vt_shape, vt_shape, jax.ShapeDtypeStruct((bsz, seq, LANES), jnp.float32)],
        grid=(bsz, seq // tm),
        in_specs=[pl.BlockSpec((1, tm, D_MODEL), lambda b, i: (b, i, 0)),
                  pl.BlockSpec((1, 3, D_MODEL), lambda b, i: (b, 0, 0)),
                  pl.BlockSpec((D_MODEL, N_QK_SLABS * LANES), lambda b, i: (0, 0)),
                  pl.BlockSpec((N_VT * KV_WIDTH, D_MODEL), lambda b, i: (0, 0)),
                  t_spec, t_spec],
        out_specs=[q_spec, c_spec, k_spec, k_spec, q_spec, k_spec, c_spec,
                   vt_spec, vt_spec, vt_spec, k_spec],
        scratch_shapes=[pltpu.VMEM((ts, KV_WIDTH), jnp.float32), pltpu.VMEM((ts, KV_WIDTH), jnp.float32)],
        compiler_params=_cparams(("arbitrary", "arbitrary")),
        name="qkv_proj",
    )(x, mod3, w_qk, w_vt, cos_t, sin_t)


def _compress_kernel(xk_ref, xv_ref, w1k_ref, w1v_ref, posk_ref, posv_ref, w2k_ref, w2vt_ref,
                     kc_ref, vct_ref):
    bb, n_chunks, cw = xk_ref.shape

    def hidden(x_ref, w1_ref, pos_ref):
        hw = w1_ref.shape[1] // 2
        p = _dot(x_ref[...].reshape(bb * n_chunks, cw), w1_ref[...])
        pt = _dot(pos_ref[...], w1_ref[...])
        pos_term = pt[0:1, 0:hw] + pt[1:2, hw:]
        first = p[:, 0:hw]
        second = pltpu.roll(p[:, hw:], bb * n_chunks - 1, 0)
        hid = first + second + pos_term
        return (hid * _sigmoid(hid)).astype(jnp.bfloat16)

    kc = _dot(hidden(xk_ref, w1k_ref, posk_ref), w2k_ref[...]).astype(jnp.bfloat16)
    vct = _dot_nt(w2vt_ref[...], hidden(xv_ref, w1v_ref, posv_ref)).astype(jnp.bfloat16)
    for i in range(bb):
        kc_ref[i] = kc[i * n_chunks:(i + 1) * n_chunks]
        vct_ref[i] = vct[:, i * n_chunks:(i + 1) * n_chunks]


def _compress(xk, xv, w1k, w1v, posk, posv, w2k, w2vt, bb=4):
    bsz, n_chunks, cw = xk.shape
    hw2 = w1k.shape[1]
    out = jax.ShapeDtypeStruct((bsz, n_chunks, LANES), jnp.bfloat16)
    x_spec = pl.BlockSpec((bb, n_chunks, cw), lambda b: (b, 0, 0))
    w1_spec = pl.BlockSpec((cw, hw2), lambda b: (0, 0))
    pos_spec = pl.BlockSpec((SUBLANES, cw), lambda b: (0, 0))
    o_spec = pl.BlockSpec((bb, n_chunks, LANES), lambda b: (b, 0, 0))
    return pl.pallas_call(
        _compress_kernel,
        out_shape=[out, out],
        grid=(bsz // bb,),
        in_specs=[x_spec, x_spec, w1_spec, w1_spec, pos_spec, pos_spec,
                  pl.BlockSpec(w2k.shape, lambda b: (0, 0)), pl.BlockSpec(w2vt.shape, lambda b: (0, 0))],
        out_specs=[o_spec, o_spec],
        compiler_params=_cparams(("arbitrary",)),
        name="compress",
    )(xk, xv, w1k, w1v, posk, posv, w2k, w2vt)


def _stack_heads(q):
    lane = lax.broadcasted_iota(jnp.int32, (1, LANES), 1)
    is_a = (lane & (HEAD_DIM - 1)) < HALF
    zero = jnp.zeros((), q.dtype)
    tiles = [q[:, j * LANES:(j + 1) * LANES] for j in range(GROUP)]
    parts = [jnp.where(is_a, t, zero) for t in tiles] + [jnp.where(is_a, zero, t) for t in tiles]
    return jnp.concatenate(parts, axis=0)


def _store_heads(o_ref, rows, o_g, inv_l, tq):
    for j in range(GROUP):
        cols = slice(j * tq, (j + 1) * tq)
        halves = [o_g[g][0:HEAD_DIM, cols] for g in range(N_KV)]
        if inv_l is not None:
            halves = [h * inv_l[g][:, cols] for g, h in enumerate(halves)]
        tile_t = jnp.concatenate(halves, axis=0)
        o_ref[0, rows, j * LANES:(j + 1) * LANES] = tile_t.T.astype(jnp.bfloat16)


def _branch_gates(gn, branch, tq):
    sig = _sigmoid(gn.T[0:3 * N_HEADS, :])
    return [jnp.concatenate([sig[3 * (GROUP * g + j) + branch:3 * (GROUP * g + j) + branch + 1, :]
                             for j in range(GROUP)], axis=1) for g in range(N_KV)]


def _group_cols(g, tq):
    return slice(g * GROUP * tq, (g + 1) * GROUP * tq)


def _cmp_kernel(q_ref, gn_ref, kc_ref, vct_ref, ovt_ref, o_ref, bias_ref, *, tq, n_cmp, n_blk):
    qi = pl.program_id(1)
    nq = tq // LANES
    qs = _stack_heads(q_ref[0])
    c_row = lax.broadcasted_iota(jnp.int32, (LANES, tq), 0)
    t_col = qi * tq + lax.broadcasted_iota(jnp.int32, (LANES, tq), 1)
    vis = (c_row * CMP_STRIDE + (CMP_LEN - 1) <= t_col) & (c_row < n_cmp)
    any_vis = (t_col[0:1, :] >= CMP_LEN - 1).astype(jnp.float32)
    s_t = _dot_nt(kc_ref[0], qs)
    p_parts = []
    for cb in range(N_HEADS * nq):
        qq = slice((cb % nq) * LANES, (cb % nq + 1) * LANES)
        s = jnp.where(vis[:, qq], s_t[:, cb * LANES:(cb + 1) * LANES], NEG_INF)
        e = jnp.exp2(s - jnp.max(s, axis=0, keepdims=True))
        p_parts.append(e * (any_vis[:, qq] / jnp.sum(e, axis=0, keepdims=True)))
    p_t = jnp.concatenate(p_parts, axis=1)
    o_g = [_dot(vct_ref[0, g * HEAD_DIM:(g + 1) * HEAD_DIM, :], p_t[:, _group_cols(g, tq)].astype(jnp.bfloat16))
           for g in range(N_KV)]
    _store_heads(o_ref, slice(None), o_g, _branch_gates(gn_ref[0], 0, tq), tq)

    j_blk = lax.broadcasted_iota(jnp.int32, (n_blk, tq), 0)
    t_q = qi * tq + lax.broadcasted_iota(jnp.int32, (n_blk, tq), 1)
    cur = jnp.right_shift(t_q, SEL_BLOCK.bit_length() - 1)
    forced = (j_blk == 0) | (j_blk == cur) | (j_blk == cur - 1)
    causal = j_blk <= cur
    n_free = SEL_TOPN - 3
    take_all = cur < SEL_TOPN
    candidate = causal & jnp.logical_not(forced)
    rows = []
    for g in range(N_KV):
        pg = p_t[:, g * GROUP * tq:(g * GROUP + 1) * tq]
        for h in range(1, GROUP):
            pg = pg + p_t[:, (g * GROUP + h) * tq:(g * GROUP + h + 1) * tq]
        hi = pg.astype(jnp.bfloat16)
        lo = (pg - hi.astype(jnp.float32)).astype(jnp.bfloat16)
        imp_t = _dot(ovt_ref[...], hi) + _dot(ovt_ref[...], lo)
        work = jnp.where(candidate, imp_t[0:n_blk, :], NEG_INF)
        picked = jnp.zeros((n_blk, tq), jnp.int32)
        for _ in range(n_free):
            top = jnp.max(work, axis=0, keepdims=True)
            first = jnp.min(jnp.where(work == top, j_blk, n_blk), axis=0, keepdims=True)
            hit = j_blk == first
            picked = jnp.where(hit, 1, picked)
            work = jnp.where(hit, NEG_INF, work)
        keep = causal & (take_all | forced | ((picked > 0) & candidate))
        rows.append(jnp.where(keep, 0.0, NEG_INF))
    rows.append(jnp.zeros((LANES - N_KV * n_blk, tq), jnp.float32))
    bias_t = jnp.concatenate(rows, axis=0)
    bias_ref[0] = bias_t.T.astype(jnp.bfloat16)


def _cmp_attn(qa, g_nsa, kc, vct, ovt, n_cmp, tq=2048):
    bsz, seq, _ = qa.shape
    n_blk = seq // SEL_BLOCK
    kern = functools.partial(_cmp_kernel, tq=tq, n_cmp=n_cmp, n_blk=n_blk)
    kv_spec = pl.BlockSpec((1, LANES, LANES), lambda b, i: (b, 0, 0))
    return pl.pallas_call(
        kern,
        out_shape=[jax.ShapeDtypeStruct((bsz, seq, Q_WIDTH), jnp.bfloat16),
                   jax.ShapeDtypeStruct((bsz, seq, LANES), jnp.bfloat16)],
        grid=(bsz, seq // tq),
        in_specs=[pl.BlockSpec((1, tq, Q_WIDTH), lambda b, i: (b, i, 0)),
                  pl.BlockSpec((1, tq, LANES), lambda b, i: (b, i, 0)),
                  kv_spec, kv_spec,
                  pl.BlockSpec((LANES, LANES), lambda b, i: (0, 0))],
        out_specs=[pl.BlockSpec((1, tq, Q_WIDTH), lambda b, i: (b, i, 0)),
                   pl.BlockSpec((1, tq, LANES), lambda b, i: (b, i, 0))],
        compiler_params=_cparams(("arbitrary", "arbitrary")),
        name="cmp_attn",
    )(qa, g_nsa, kc, vct, ovt)


def _sel_kernel(q_ref, gn_ref, bias_ref, k_ref, vt_ref, blk_ref, o_ref, kaug_ref, qaug_ref, s0_ref, s1_ref, acc_ref, m_ref,
                *, tq, tk):
    step = pl.program_id(1)
    nq = tq // LANES
    nk = tk // LANES
    per_g = GROUP * nq

    @pl.when(step == 0)
    def _():
        kaug_ref[:, 0:LANES] = k_ref[0]
        kaug_ref[:, LANES:2 * LANES] = blk_ref[...]

    lane = lax.broadcasted_iota(jnp.int32, (1, LANES), 1)
    n_blk = blk_ref.shape[0] // SEL_BLOCK
    bt = bias_ref[0]
    zero = jnp.zeros((), bt.dtype)
    b0 = jnp.where(lane < n_blk, bt, zero)
    b1 = jnp.where((lane >= n_blk) & (lane < 2 * n_blk), bt, zero)
    qaug_ref[:, 0:LANES] = _stack_heads(q_ref[0])
    qaug_ref[:, LANES:2 * LANES] = jnp.concatenate([b0] * GROUP + [b1] * GROUP, axis=0)

    acc_ref[...] = jnp.zeros_like(acc_ref)
    m_ref[...] = jnp.full_like(m_ref, NEG_INF)
    row = lax.broadcasted_iota(jnp.int32, (tk, tk), 0)
    col = lax.broadcasted_iota(jnp.int32, (tk, tk), 1)
    causal_bias = jnp.where(row <= col, 0.0, NEG_INF)

    def scores(c, s_ref):
        k0 = pl.multiple_of(c * tk, tk)
        s_ref[...] = _dot_nt(kaug_ref[pl.ds(k0, tk), :], qaug_ref[...])

    def accumulate(c, s_ref, diag_half):
        k0 = pl.multiple_of(c * tk, tk)
        for g in range(N_KV):
            ps, alphas = [], []
            for i in range(per_g):
                cb = g * per_g + i
                half, within = divmod(cb % nq, nk)
                cols = slice(cb * LANES, (cb + 1) * LANES)
                if diag_half is not None and half < diag_half:
                    ps.append(jnp.zeros((tk, LANES), jnp.bfloat16))
                    alphas.append(jnp.ones((1, LANES), jnp.float32))
                    continue
                s = s_ref[:, cols]
                if diag_half is not None and half == diag_half:
                    s = s + causal_bias[:, within * LANES:(within + 1) * LANES]
                m_old = m_ref[0:1, cols]
                m_new = jnp.maximum(m_old, jnp.max(s, axis=0, keepdims=True))
                m_ref[0:1, cols] = m_new
                alphas.append(jnp.exp2(m_old - m_new))
                ps.append(jnp.exp2(s - m_new).astype(jnp.bfloat16))
            vt_g = vt_ref[0, g * G_ROWS:(g + 1) * G_ROWS, pl.ds(k0, tk)]
            pv = _dot(vt_g, jnp.concatenate(ps, axis=1))
            acc_ref[g] = jnp.concatenate(alphas, axis=1) * acc_ref[g] + pv

    scores(0, s0_ref)

    def body(i, carry):
        c = 2 * i
        scores(c + 1, s1_ref)
        accumulate(c, s0_ref, None)
        scores(c + 2, s0_ref)
        accumulate(c + 1, s1_ref, None)
        return carry

    lax.fori_loop(0, step, body, 0)
    c0 = 2 * step
    scores(c0 + 1, s1_ref)
    accumulate(c0, s0_ref, 0)
    accumulate(c0 + 1, s1_ref, 1)

    o_g = [acc_ref[g] for g in range(N_KV)]
    gates = _branch_gates(gn_ref[0], 1, tq)
    scale = [gates[g] / o_g[g][HEAD_DIM:HEAD_DIM + 1, :] for g in range(N_KV)]
    _store_heads(o_ref, slice(None), o_g, scale, tq)


def _sel_attn(qa, g_nsa, bias, ks, vst, blk_onehot, tk=256):
    bsz, seq, _ = qa.shape
    tq = 2 * tk
    kern = functools.partial(_sel_kernel, tq=tq, tk=tk)
    return pl.pallas_call(
        kern,
        out_shape=jax.ShapeDtypeStruct((bsz, seq, Q_WIDTH), jnp.bfloat16),
        grid=(bsz, seq // tq),
        in_specs=[pl.BlockSpec((1, tq, Q_WIDTH), lambda b, i: (b, i, 0)),
                  pl.BlockSpec((1, tq, LANES), lambda b, i: (b, i, 0)),
                  pl.BlockSpec((1, tq, LANES), lambda b, i: (b, i, 0)),
                  pl.BlockSpec((1, seq, LANES), lambda b, i: (b, 0, 0)),
                  pl.BlockSpec((1, V_ROWS, seq), lambda b, i: (b, 0, 0)),
                  pl.BlockSpec((seq, LANES), lambda b, i: (0, 0))],
        out_specs=pl.BlockSpec((1, tq, Q_WIDTH), lambda b, i: (b, i, 0)),
        scratch_shapes=[pltpu.VMEM((seq, 2 * LANES), jnp.bfloat16),
                        pltpu.VMEM((N_HEADS * tq, 2 * LANES), jnp.bfloat16),
                        pltpu.VMEM((tk, N_HEADS * tq), jnp.float32),
                        pltpu.VMEM((tk, N_HEADS * tq), jnp.float32),
                        pltpu.VMEM((N_KV, G_ROWS, GROUP * tq), jnp.float32),
                        pltpu.VMEM((SUBLANES, N_HEADS * tq), jnp.float32)],
        compiler_params=_cparams(("arbitrary", "arbitrary")),
        name="sel_attn",
    )(qa, g_nsa, bias, ks, vst, blk_onehot)


def _band_kernel(*refs, tq, n_sub, span, window, use_sink):
    if use_sink:
        q_ref, k_ref, vt_ref, sink_ref, o_ref = refs
    else:
        q_ref, k_ref, vt_ref, gn_ref, o_ref = refs
    step = pl.program_id(1)
    seq = k_ref.shape[1]
    nq = tq // LANES
    row = lax.broadcasted_iota(jnp.int32, (span, tq), 0)
    col = lax.broadcasted_iota(jnp.int32, (span, tq), 1)
    diff = col - row

    def scores(sub):
        q0 = (step * n_sub + sub) * tq
        start = pl.multiple_of(jnp.clip(q0 + tq - span, 0, seq - span), LANES)
        rel = diff + (q0 - start)
        bias = jnp.where((rel >= 0) & (rel < window), 0.0, NEG_INF)
        qs = _stack_heads(q_ref[0, sub * tq:(sub + 1) * tq, :])
        return start, bias, _dot_nt(k_ref[0, pl.ds(start, span), :], qs)

    def finish(sub, start, bias, s_t):
        ps, extra = [], []
        for cb in range(N_HEADS * nq):
            s = s_t[:, cb * LANES:(cb + 1) * LANES] + bias[:, (cb % nq) * LANES:(cb % nq + 1) * LANES]
            m = jnp.max(s, axis=0, keepdims=True)
            if use_sink:
                sink = sink_ref[cb // nq] * LOG2E
                m = jnp.maximum(m, sink)
                extra.append(jnp.exp2(sink - m))
            ps.append(jnp.exp2(s - m).astype(jnp.bfloat16))
        o_g, inv_l = [], []
        for g in range(N_KV):
            per_g = GROUP * nq
            vt_g = vt_ref[0, g * G_ROWS:(g + 1) * G_ROWS, pl.ds(start, span)]
            o = _dot(vt_g, jnp.concatenate(ps[g * per_g:(g + 1) * per_g], axis=1))
            l = o[HEAD_DIM:HEAD_DIM + 1, :]
            if use_sink:
                l = l + jnp.concatenate(extra[g * per_g:(g + 1) * per_g], axis=1)
            o_g.append(o)
            inv_l.append(1.0 / l)
        if not use_sink:
            gates = _branch_gates(gn_ref[0, sub * tq:(sub + 1) * tq, :], 2, tq)
            inv_l = [a * b for a, b in zip(inv_l, gates)]
        _store_heads(o_ref, slice(sub * tq, (sub + 1) * tq), o_g, inv_l, tq)

    nxt = scores(0)
    for sub in range(n_sub):
        cur = nxt
        if sub + 1 < n_sub:
            nxt = scores(sub + 1)
        finish(sub, *cur)


def _band_attn(q, k, vt, window, sinks=None, g_nsa=None, tq=128, n_sub=16):
    bsz, seq, _ = q.shape
    span = (-(-(window - 1) // tq)) * tq + tq
    use_sink = sinks is not None
    kern = functools.partial(_band_kernel, tq=tq, n_sub=n_sub, span=span, window=window, use_sink=use_sink)
    ts = tq * n_sub
    in_specs = [pl.BlockSpec((1, ts, Q_WIDTH), lambda b, i: (b, i, 0)),
                pl.BlockSpec((1, seq, LANES), lambda b, i: (b, 0, 0)),
                pl.BlockSpec((1, V_ROWS, seq), lambda b, i: (b, 0, 0))]
    args = [q, k, vt]
    if use_sink:
        in_specs.append(pl.BlockSpec(memory_space=pltpu.SMEM))
        args.append(sinks)
    else:
        in_specs.append(pl.BlockSpec((1, ts, LANES), lambda b, i: (b, i, 0)))
        args.append(g_nsa)
    return pl.pallas_call(
        kern,
        out_shape=jax.ShapeDtypeStruct((bsz, seq, Q_WIDTH), jnp.bfloat16),
        grid=(bsz, seq // ts),
        in_specs=in_specs,
        out_specs=pl.BlockSpec((1, ts, Q_WIDTH), lambda b, i: (b, i, 0)),
        compiler_params=_cparams(("arbitrary", "arbitrary")),
        name="band_attn_sink" if use_sink else "band_attn",
    )(*args)


def _merge_kernel(x_ref, mod_ref, wza_ref, wzb_ref, wgm_ref, ocmp_ref, osel_ref, owin_ref, ob_ref,
                  wua_ref, wub_ref, wo_ref, lng_ref, lnb_ref, o_ref, *, ts, n_sub):
    shift = mod_ref[0, 0:1, :]
    scale = mod_ref[0, 1:2, :]
    gate = mod_ref[0, 2:3, :]

    def project(r):
        u = (_layer_norm(x_ref[0, r * ts:(r + 1) * ts, :]) * (1.0 + scale) + shift).astype(jnp.bfloat16)
        return _dot(u, wza_ref[...]), _dot(u, wzb_ref[...]), _dot_nt(u, wgm_ref[...])

    def finish(r, z_a, z_b, g_m):
        rows = slice(r * ts, (r + 1) * ts)
        o_a = (ocmp_ref[0, rows, :].astype(jnp.float32) + osel_ref[0, rows, :].astype(jnp.float32)
               + owin_ref[0, rows, :].astype(jnp.float32))
        y_a = (o_a * (z_a * _sigmoid(z_a))).astype(jnp.bfloat16)
        y_b = (ob_ref[0, rows, :].astype(jnp.float32) * (z_b * _sigmoid(z_b))).astype(jnp.bfloat16)
        merged = _sigmoid(g_m[:, 0:D_MODEL]) * _dot(y_a, wua_ref[...])
        merged = merged + _sigmoid(g_m[:, D_MODEL:2 * D_MODEL]) * _dot(y_b, wub_ref[...])
        out = _dot(merged.astype(jnp.bfloat16), wo_ref[...])
        y = DEEPNORM_ALPHA * x_ref[0, rows, :] + gate * out
        o_ref[0, rows, :] = _layer_norm(y) * lng_ref[...] + lnb_ref[...]

    nxt = project(0)
    for r in range(n_sub):
        cur = nxt
        if r + 1 < n_sub:
            nxt = project(r + 1)
        finish(r, *cur)


def _merge_out(x, mod3, wgs, o_cmp, o_sel, o_win, o_b, wua, wub, wo, ln_g, ln_b, ts=256, n_sub=4):
    bsz, seq, _ = x.shape
    tm = ts * n_sub
    tok = lambda w: pl.BlockSpec((1, tm, w), lambda b, i: (b, i, 0))
    full = lambda a: pl.BlockSpec(a.shape, lambda b, i: (0,) * a.ndim)
    return pl.pallas_call(
        functools.partial(_merge_kernel, ts=ts, n_sub=n_sub),
        out_shape=jax.ShapeDtypeStruct(x.shape, jnp.float32),
        grid=(bsz, seq // tm),
        in_specs=[tok(D_MODEL), pl.BlockSpec((1, 3, D_MODEL), lambda b, i: (b, 0, 0)),
                  *[full(w) for w in wgs], tok(Q_WIDTH), tok(Q_WIDTH), tok(Q_WIDTH), tok(Q_WIDTH),
                  full(wua), full(wub), full(wo), full(ln_g), full(ln_b)],
        out_specs=tok(D_MODEL),
        compiler_params=_cparams(("arbitrary", "arbitrary")),
        name="merge_out",
    )(x, mod3, *wgs, o_cmp, o_sel, o_win, o_b, wua, wub, wo, ln_g, ln_b)


def _pair_q_cols(w):
    k = w.shape[0]
    return w.reshape(k, N_KV, GROUP, 2, HALF).transpose(0, 2, 3, 1, 4).reshape(k, Q_WIDTH)


def _pair_k_cols(w):
    k = w.shape[0]
    return w.reshape(k, N_KV, 2, HALF).transpose(0, 2, 1, 3).reshape(k, KV_WIDTH)


def _pair_o_cols(w):
    k = w.shape[0]
    return w.reshape(k, N_KV, GROUP, HEAD_DIM).transpose(0, 2, 1, 3).reshape(k, Q_WIDTH)


def _pair_o_rows(w):
    n = w.shape[1]
    return w.reshape(N_KV, GROUP, HEAD_DIM, n).transpose(1, 0, 2, 3).reshape(Q_WIDTH, n)


def _prep_qkv_weight(w_t):
    q_scale = HEAD_DIM ** -0.5
    sec = lambda off, width: w_t[off:off + width].T
    cols = [_pair_q_cols(sec(OFF_QA, Q_WIDTH)) * q_scale,
            _pair_k_cols(sec(OFF_KVC, KV_WIDTH)), _pair_k_cols(sec(OFF_KVS, KV_WIDTH)),
            _pair_k_cols(sec(OFF_KVW, KV_WIDTH)),
            _pair_q_cols(sec(OFF_QB, Q_WIDTH)) * q_scale,
            _pair_k_cols(sec(OFF_KVB, KV_WIDTH)),
            sec(OFF_KVC + KV_WIDTH, KV_WIDTH),
            sec(OFF_GN, 3 * N_HEADS), jnp.zeros((D_MODEL, LANES - 3 * N_HEADS), w_t.dtype)]
    v_cols = [sec(OFF_KVS + KV_WIDTH, KV_WIDTH), sec(OFF_KVW + KV_WIDTH, KV_WIDTH),
              sec(OFF_KVB + KV_WIDTH, KV_WIDTH)]
    return jnp.concatenate(cols, axis=1), jnp.concatenate(v_cols, axis=1).T


def _prep_gate_weights(w_t):
    return [_pair_o_cols(w_t[OFF_ZA:OFF_ZA + Q_WIDTH].T), _pair_o_cols(w_t[OFF_ZB:OFF_ZB + Q_WIDTH].T),
            w_t[OFF_GM:OFF_GM + 2 * D_MODEL]]


def _prep_compress(pos, w1, w2, paired):
    half_len = CMP_LEN // 2
    w = w1.reshape(CMP_LEN, HEAD_DIM, CMP_HIDDEN).astype(jnp.bfloat16)
    w2 = w2.astype(jnp.bfloat16)
    pieces = ([(g, slice(h * HALF, (h + 1) * HALF)) for h in range(2) for g in range(N_KV)] if paired
              else [(g, slice(0, HEAD_DIM)) for g in range(N_KV)])
    own = lambda g, blk: jnp.concatenate([blk if k == g else jnp.zeros_like(blk) for k in range(N_KV)], axis=-1)
    w1x = jnp.concatenate([own(g, w[:, d, :]) for g, d in pieces], axis=1)
    first = w1x[:half_len].reshape(half_len * LANES, N_KV * CMP_HIDDEN)
    second = w1x[half_len:].reshape(half_len * LANES, N_KV * CMP_HIDDEN)
    w1e = jnp.concatenate([first, second], axis=1)
    w2e = jnp.concatenate([jnp.concatenate([w2[:, d] if k == g else jnp.zeros_like(w2[:, d]) for g, d in pieces], axis=1)
                           for k in range(N_KV)], axis=0)
    pos_l = jnp.concatenate([pos[:, d] for _, d in pieces], axis=1)
    pos_rows = jnp.concatenate([pos_l[:half_len].reshape(1, -1), pos_l[half_len:].reshape(1, -1),
                                jnp.zeros((SUBLANES - 2, half_len * LANES), pos.dtype)], axis=0).astype(jnp.bfloat16)
    return w1e, pos_rows, w2e


def _overlap_t(n_cmp, n_blk):
    c0 = np.arange(n_cmp) * CMP_STRIDE
    j0 = np.arange(n_blk) * SEL_BLOCK
    overlap = (c0[:, None] < j0[None, :] + SEL_BLOCK) & (c0[:, None] + CMP_LEN > j0[None, :])
    ovt = np.zeros((LANES, LANES), np.float32)
    ovt[:n_blk, :n_cmp] = overlap.T
    return jnp.asarray(ovt, jnp.bfloat16)


def _block_onehot(seq):
    n_blk = seq // SEL_BLOCK
    e = np.zeros((seq, LANES), np.float32)
    pos = np.arange(seq)
    e[pos, pos // SEL_BLOCK] = 1.0
    e[pos, n_blk + pos // SEL_BLOCK] = 1.0
    return jnp.asarray(e, jnp.bfloat16)


def kernel(x, c, positions, w_ada, b_ada, w_in, cmp_pos_k, cmp_w1_k, cmp_w2_k,
           cmp_pos_v, cmp_w1_v, cmp_w2_v, sinks, w_up_a, w_up_b, w_out, ln_g, ln_b):
    bsz, seq, _ = x.shape
    n_cmp = (seq - CMP_LEN) // CMP_STRIDE + 1
    n_blk = seq // SEL_BLOCK
    assert seq // CMP_STRIDE == LANES and 2 * n_blk <= LANES
    cos_t, sin_t = _rope_tables(positions, QKV_ROWS)
    for l in range(DEPTH):
        mod3 = _adaln_mod(c, w_ada[l], b_ada[l]).reshape(bsz, 3, D_MODEL)
        w_t = jnp.swapaxes(w_in[l], 0, 1).astype(jnp.bfloat16)
        w_qk, w_vt = _prep_qkv_weight(w_t)
        qa, k_c, k_s, k_w, qb, k_b, v_c, vt_s, vt_w, vt_b, g_nsa = _qkv_proj(x, mod3, w_qk, w_vt, cos_t, sin_t)

        w1k, posk, w2k = _prep_compress(cmp_pos_k[l], cmp_w1_k[l], cmp_w2_k[l], paired=True)
        w1v, posv, w2v = _prep_compress(cmp_pos_v[l], cmp_w1_v[l], cmp_w2_v[l], paired=False)
        kc, vct = _compress(k_c, v_c, w1k, w1v, posk, posv, w2k, w2v.T, bb=4 if bsz % 4 == 0 else 1)

        o_cmp, sel_bias = _cmp_attn(qa, g_nsa, kc, vct, _overlap_t(n_cmp, n_blk), n_cmp)
        o_sel = _sel_attn(qa, g_nsa, sel_bias, k_s, vt_s, _block_onehot(seq))
        o_win = _band_attn(qa, k_w, vt_w, NSA_WINDOW, g_nsa=g_nsa)
        o_b = _band_attn(qb, k_b, vt_b, SWA_WINDOW, sinks[l].astype(jnp.float32))

        x = _merge_out(x, mod3, _prep_gate_weights(w_t),
                       o_cmp, o_sel, o_win, o_b,
                       _pair_o_rows(w_up_a[l]).astype(jnp.bfloat16), _pair_o_rows(w_up_b[l]).astype(jnp.bfloat16),
                       w_out[l].astype(jnp.bfloat16), ln_g[l].reshape(1, D_MODEL), ln_b[l].reshape(1, D_MODEL))
    return x
```

```python
import functools

import numpy as np
import jax
import jax.numpy as jnp
from jax import lax
from jax.experimental import pallas as pl
from jax.experimental.pallas import tpu as pltpu

D_MODEL = 1024
HEAD_DIM = 64
HALF = HEAD_DIM // 2
N_HEADS = 8
N_KV = 2
GROUP = N_HEADS // N_KV
Q_WIDTH = N_HEADS * HEAD_DIM
KV_WIDTH = N_KV * HEAD_DIM
CMP_LEN = 32
CMP_STRIDE = 16
CMP_HIDDEN = 256
SEL_BLOCK = 64
SEL_TOPN = 8
NSA_WINDOW = 512
SWA_WINDOW = 128
ROPE_THETA = 10000.0
LN_EPS = 1e-5
NEG_INF = -1e30
FORCE_SCORE = 1e9
DEPTH = 1
DEEPNORM_ALPHA = (2 * DEPTH) ** 0.25
LANES = 128
SUBLANES = 8
BF16_SUBLANES = 16
ROPE_PACK = LANES // HALF
QKV_ROWS = 256
LOG2E = 1.4426950408889634
G_ROWS = HEAD_DIM + BF16_SUBLANES
V_ROWS = N_KV * G_ROWS
VMEM_LIMIT = 56 * 1024 * 1024

_SPLITS = [Q_WIDTH, 2 * KV_WIDTH, 2 * KV_WIDTH, 2 * KV_WIDTH, 3 * N_HEADS, Q_WIDTH,
           Q_WIDTH, 2 * KV_WIDTH, Q_WIDTH, 2 * D_MODEL]
_OFFS = [int(o) for o in np.cumsum([0] + _SPLITS)]
(OFF_QA, OFF_KVC, OFF_KVS, OFF_KVW, OFF_GN, OFF_ZA, OFF_QB, OFF_KVB, OFF_ZB, OFF_GM) = _OFFS[:10]


def _cparams(sem):
    return pltpu.CompilerParams(dimension_semantics=sem, vmem_limit_bytes=VMEM_LIMIT)


def _dot(a, b):
    return jnp.dot(a, b, preferred_element_type=jnp.float32)


def _dot_nt(a, b):
    return lax.dot_general(a, b, (((1,), (1,)), ((), ())), preferred_element_type=jnp.float32)


def _layer_norm(x):
    mu = jnp.mean(x, axis=-1, keepdims=True)
    xc = x - mu
    var = jnp.mean(xc * xc, axis=-1, keepdims=True)
    return xc * lax.rsqrt(var + LN_EPS)


def _sigmoid(x):
    return 1.0 / (1.0 + jnp.exp(-x))


def _adaln_kernel(c_ref, w_ref, b_ref, o_ref):
    c = c_ref[...]
    a = c * _sigmoid(c)
    o_ref[...] = jnp.dot(a, w_ref[...], preferred_element_type=jnp.float32,
                         precision=lax.Precision.HIGHEST) + b_ref[...]


def _adaln_mod(c, w_ada, b_ada):
    bsz = c.shape[0]
    n = w_ada.shape[1]
    tn = 1024
    return pl.pallas_call(
        _adaln_kernel,
        out_shape=jax.ShapeDtypeStruct((bsz, n), jnp.float32),
        grid=(n // tn,),
        in_specs=[pl.BlockSpec((bsz, D_MODEL), lambda j: (0, 0)),
                  pl.BlockSpec((D_MODEL, tn), lambda j: (0, j)),
                  pl.BlockSpec((1, tn), lambda j: (0, j))],
        out_specs=pl.BlockSpec((bsz, tn), lambda j: (0, j)),
        compiler_params=_cparams(("arbitrary",)),
        name="adaln_mod",
    )(c, w_ada, b_ada.reshape(1, n))


def _rope_table_kernel(pos_ref, invf_ref, cos_ref, sin_ref):
    pos = pos_ref[...]
    rows = pos.shape[0]
    ang = jnp.concatenate([jnp.broadcast_to(pos[:, k:k + 1], (rows, HALF)) for k in range(ROPE_PACK)],
                          axis=1) * invf_ref[...]
    cos_ref[...] = jnp.cos(ang)
    sin_ref[...] = jnp.sin(ang)


def _rope_tables(positions, ts):
    bsz, seq = positions.shape
    sub = ts // ROPE_PACK
    rows = bsz * seq // ROPE_PACK
    pos = positions.astype(jnp.float32).reshape(bsz, seq // ts, ROPE_PACK, sub).transpose(0, 1, 3, 2)
    pos = pos.reshape(rows, ROPE_PACK)
    inv_freq = ROPE_THETA ** (-jnp.arange(HALF, dtype=jnp.float32) / HALF)
    invf = jnp.broadcast_to(inv_freq[None, :], (ROPE_PACK, HALF)).reshape(1, LANES)
    tr = seq // ROPE_PACK
    cos, sin = pl.pallas_call(
        _rope_table_kernel,
        out_shape=[jax.ShapeDtypeStruct((rows, LANES), jnp.float32)] * 2,
        grid=(rows // tr,),
        in_specs=[pl.BlockSpec((tr, ROPE_PACK), lambda i: (i, 0)),
                  pl.BlockSpec((1, LANES), lambda i: (0, 0))],
        out_specs=[pl.BlockSpec((tr, LANES), lambda i: (i, 0))] * 2,
        compiler_params=_cparams(("arbitrary",)),
        name="rope_table",
    )(pos, invf)
    return cos.reshape(bsz, tr, LANES), sin.reshape(bsz, tr, LANES)


N_QK_SLABS = 14
SLAB_QA, SLAB_KC, SLAB_KS, SLAB_KW, SLAB_QB, SLAB_KB, SLAB_VC, SLAB_GN = 0, 4, 5, 6, 7, 11, 12, 13
N_VT = 3


def _qkv_kernel(x_ref, mod_ref, w_ref, wvt_ref, cos_ref, sin_ref,
                qa_ref, kc_ref, ks_ref, kw_ref, qb_ref, kb_ref, vc_ref,
                vst_ref, vwt_ref, vbt_ref, gn_ref, kstage_ref, vstage_ref, *, ts, n_sub):
    shift = mod_ref[0, 0:1, :]
    scale = mod_ref[0, 1:2, :]

    def project(r):
        u = (_layer_norm(x_ref[0, r * ts:(r + 1) * ts, :]) * (1.0 + scale) + shift).astype(jnp.bfloat16)
        return _dot(u, w_ref[...]), _dot_nt(wvt_ref[...], u)

    def finish(r, acc, vt):
        rows = slice(r * ts, (r + 1) * ts)
        sub = ts // ROPE_PACK
        for k in range(ROPE_PACK):
            c32 = cos_ref[0, r * sub:(r + 1) * sub, k * HALF:(k + 1) * HALF]
            s32 = sin_ref[0, r * sub:(r + 1) * sub, k * HALF:(k + 1) * HALF]
            cos = jnp.concatenate([c32, c32, c32, c32], axis=1)
            sin = jnp.concatenate([-s32, -s32, s32, s32], axis=1)
            tok = slice(r * ts + k * sub, r * ts + (k + 1) * sub)

            def rope(slab):
                t = acc[k * sub:(k + 1) * sub, slab * LANES:(slab + 1) * LANES]
                return t * cos + pltpu.roll(t, LANES // 2, 1) * sin

            for j in range(GROUP):
                qa_ref[0, tok, j * LANES:(j + 1) * LANES] = (rope(SLAB_QA + j) * LOG2E).astype(jnp.bfloat16)
                qb_ref[0, tok, j * LANES:(j + 1) * LANES] = (rope(SLAB_QB + j) * LOG2E).astype(jnp.bfloat16)
            for slab, ref in ((SLAB_KS, ks_ref), (SLAB_KW, kw_ref), (SLAB_KB, kb_ref)):
                ref[0, tok, :] = rope(slab).astype(jnp.bfloat16)
            kstage_ref[k * sub:(k + 1) * sub, :] = rope(SLAB_KC)
        vstage_ref[...] = acc[:, SLAB_VC * LANES:(SLAB_VC + 1) * LANES]
        gn_ref[0, rows, :] = acc[:, SLAB_GN * LANES:(SLAB_GN + 1) * LANES]
        n_ch = ts // CMP_STRIDE
        for l in range(CMP_STRIDE):
            for stage, ref in ((kstage_ref, kc_ref), (vstage_ref, vc_ref)):
                ref[0, r * n_ch:(r + 1) * n_ch, l * LANES:(l + 1) * LANES] = (
                    stage[pl.ds(l, n_ch, stride=CMP_STRIDE), :].astype(jnp.bfloat16))
        ones = jnp.ones((G_ROWS - HEAD_DIM, ts), jnp.bfloat16)
        for j, ref in enumerate((vst_ref, vwt_ref, vbt_ref)):
            for g in range(N_KV):
                lo = j * KV_WIDTH + g * HEAD_DIM
                ref[0, g * G_ROWS:g * G_ROWS + HEAD_DIM, rows] = vt[lo:lo + HEAD_DIM].astype(jnp.bfloat16)
                ref[0, g * G_ROWS + HEAD_DIM:(g + 1) * G_ROWS, rows] = ones

    nxt = project(0)
    for r in range(n_sub):
        cur = nxt
        if r + 1 < n_sub:
            nxt = project(r + 1)
        finish(r, *cur)


def _qkv_proj(x, mod3, w_qk, w_vt, cos_t, sin_t, ts=QKV_ROWS, n_sub=8):
    bsz, seq, _ = x.shape
    tm = ts * n_sub
    chunk_w = CMP_STRIDE * KV_WIDTH
    q_shape = jax.ShapeDtypeStruct((bsz, seq, Q_WIDTH), jnp.bfloat16)
    k_shape = jax.ShapeDtypeStruct((bsz, seq, KV_WIDTH), jnp.bfloat16)
    c_shape = jax.ShapeDtypeStruct((bsz, seq // CMP_STRIDE, chunk_w), jnp.bfloat16)
    vt_shape = jax.ShapeDtypeStruct((bsz, V_ROWS, seq), jnp.bfloat16)
    q_spec = pl.BlockSpec((1, tm, Q_WIDTH), lambda b, i: (b, i, 0))
    k_spec = pl.BlockSpec((1, tm, KV_WIDTH), lambda b, i: (b, i, 0))
    c_spec = pl.BlockSpec((1, tm // CMP_STRIDE, chunk_w), lambda b, i: (b, i, 0))
    t_spec = pl.BlockSpec((1, tm // ROPE_PACK, LANES), lambda b, i: (b, i, 0))
    vt_spec = pl.BlockSpec((1, V_ROWS, tm), lambda b, i: (b, 0, i))
    return pl.pallas_call(
        functools.partial(_qkv_kernel, ts=ts, n_sub=n_sub),
        out_shape=[q_shape, c_shape, k_shape, k_shape, q_shape, k_shape, c_shape,
                   vt_shape, vt_shape, vt_shape, jax.ShapeDtypeStruct((bsz, seq, LANES), jnp.float32)],
        grid=(bsz, seq // tm),
        in_specs=[pl.BlockSpec((1, tm, D_MODEL), lambda b, i: (b, i, 0)),
                  pl.BlockSpec((1, 3, D_MODEL), lambda b, i: (b, 0, 0)),
                  pl.BlockSpec((D_MODEL, N_QK_SLABS * LANES), lambda b, i: (0, 0)),
                  pl.BlockSpec((N_VT * KV_WIDTH, D_MODEL), lambda b, i: (0, 0)),
                  t_spec, t_spec],
        out_specs=[q_spec, c_spec, k_spec, k_spec, q_spec, k_spec, c_spec,
                   vt_spec, vt_spec, vt_spec, k_spec],
        scratch_shapes=[pltpu.VMEM((ts, KV_WIDTH), jnp.float32), pltpu.VMEM((ts, KV_WIDTH), jnp.float32)],
        compiler_params=_cparams(("arbitrary", "arbitrary")),
        name="qkv_proj",
    )(x, mod3, w_qk, w_vt, cos_t, sin_t)


def _compress_kernel(xk_ref, xv_ref, w1k_ref, w1v_ref, posk_ref, posv_ref, w2k_ref, w2vt_ref,
                     kc_ref, vct_ref):
    bb, n_chunks, cw = xk_ref.shape

    def hidden(x_ref, w1_ref, pos_ref):
        hw = w1_ref.shape[1] // 2
        p = _dot(x_ref[...].reshape(bb * n_chunks, cw), w1_ref[...])
        pt = _dot(pos_ref[...], w1_ref[...])
        pos_term = pt[0:1, 0:hw] + pt[1:2, hw:]
        first = p[:, 0:hw]
        second = pltpu.roll(p[:, hw:], bb * n_chunks - 1, 0)
        hid = first + second + pos_term
        return (hid * _sigmoid(hid)).astype(jnp.bfloat16)

    kc = _dot(hidden(xk_ref, w1k_ref, posk_ref), w2k_ref[...]).astype(jnp.bfloat16)
    vct = _dot_nt(w2vt_ref[...], hidden(xv_ref, w1v_ref, posv_ref)).astype(jnp.bfloat16)
    for i in range(bb):
        kc_ref[i] = kc[i * n_chunks:(i + 1) * n_chunks]
        vct_ref[i] = vct[:, i * n_chunks:(i + 1) * n_chunks]


def _compress(xk, xv, w1k, w1v, posk, posv, w2k, w2vt, bb=4):
    bsz, n_chunks, cw = xk.shape
    hw2 = w1k.shape[1]
    out = jax.ShapeDtypeStruct((bsz, n_chunks, LANES), jnp.bfloat16)
    x_spec = pl.BlockSpec((bb, n_chunks, cw), lambda b: (b, 0, 0))
    w1_spec = pl.BlockSpec((cw, hw2), lambda b: (0, 0))
    pos_spec = pl.BlockSpec((SUBLANES, cw), lambda b: (0, 0))
    o_spec = pl.BlockSpec((bb, n_chunks, LANES), lambda b: (b, 0, 0))
    return pl.pallas_call(
        _compress_kernel,
        out_shape=[out, out],
        grid=(bsz // bb,),
        in_specs=[x_spec, x_spec, w1_spec, w1_spec, pos_spec, pos_spec,
                  pl.BlockSpec(w2k.shape, lambda b: (0, 0)), pl.BlockSpec(w2vt.shape, lambda b: (0, 0))],
        out_specs=[o_spec, o_spec],
        compiler_params=_cparams(("arbitrary",)),
        name="compress",
    )(xk, xv, w1k, w1v, posk, posv, w2k, w2vt)


def _stack_heads(q):
    lane = lax.broadcasted_iota(jnp.int32, (1, LANES), 1)
    is_a = (lane & (HEAD_DIM - 1)) < HALF
    zero = jnp.zeros((), q.dtype)
    tiles = [q[:, j * LANES:(j + 1) * LANES] for j in range(GROUP)]
    parts = [jnp.where(is_a, t, zero) for t in tiles] + [jnp.where(is_a, zero, t) for t in tiles]
    return jnp.concatenate(parts, axis=0)


def _store_heads(o_ref, rows, o_g, inv_l, tq):
    for j in range(GROUP):
        cols = slice(j * tq, (j + 1) * tq)
        halves = [o_g[g][0:HEAD_DIM, cols] for g in range(N_KV)]
        if inv_l is not None:
            halves = [h * inv_l[g][:, cols] for g, h in enumerate(halves)]
        tile_t = jnp.concatenate(halves, axis=0)
        o_ref[0, rows, j * LANES:(j + 1) * LANES] = tile_t.T.astype(jnp.bfloat16)


def _branch_gates(gn, branch, tq):
    sig = _sigmoid(gn.T[0:3 * N_HEADS, :])
    return [jnp.concatenate([sig[3 * (GROUP * g + j) + branch:3 * (GROUP * g + j) + branch + 1, :]
                             for j in range(GROUP)], axis=1) for g in range(N_KV)]


def _group_cols(g, tq):
    return slice(g * GROUP * tq, (g + 1) * GROUP * tq)


def _cmp_kernel(q_ref, gn_ref, kc_ref, vct_ref, ovt_ref, o_ref, bias_ref, *, tq, n_cmp, n_blk):
    qi = pl.program_id(1)
    nq = tq // LANES
    qs = _stack_heads(q_ref[0])
    c_row = lax.broadcasted_iota(jnp.int32, (LANES, tq), 0)
    t_col = qi * tq + lax.broadcasted_iota(jnp.int32, (LANES, tq), 1)
    vis = (c_row * CMP_STRIDE + (CMP_LEN - 1) <= t_col) & (c_row < n_cmp)
    any_vis = (t_col[0:1, :] >= CMP_LEN - 1).astype(jnp.float32)
    s_t = _dot_nt(kc_ref[0], qs)
    p_parts = []
    for cb in range(N_HEADS * nq):
        qq = slice((cb % nq) * LANES, (cb % nq + 1) * LANES)
        s = jnp.where(vis[:, qq], s_t[:, cb * LANES:(cb + 1) * LANES], NEG_INF)
        e = jnp.exp2(s - jnp.max(s, axis=0, keepdims=True))
        p_parts.append(e * (any_vis[:, qq] / jnp.sum(e, axis=0, keepdims=True)))
    p_t = jnp.concatenate(p_parts, axis=1)
    o_g = [_dot(vct_ref[0, g * HEAD_DIM:(g + 1) * HEAD_DIM, :], p_t[:, _group_cols(g, tq)].astype(jnp.bfloat16))
           for g in range(N_KV)]
    _store_heads(o_ref, slice(None), o_g, _branch_gates(gn_ref[0], 0, tq), tq)

    j_blk = lax.broadcasted_iota(jnp.int32, (n_blk, tq), 0)
    t_q = qi * tq + lax.broadcasted_iota(jnp.int32, (n_blk, tq), 1)
    cur = jnp.right_shift(t_q, SEL_BLOCK.bit_length() - 1)
    forced = (j_blk == 0) | (j_blk == cur) | (j_blk == cur - 1)
    causal = j_blk <= cur
    n_free = SEL_TOPN - 3
    take_all = cur < SEL_TOPN
    candidate = causal & jnp.logical_not(forced)
    rows = []
    for g in range(N_KV):
        pg = p_t[:, g * GROUP * tq:(g * GROUP + 1) * tq]
        for h in range(1, GROUP):
            pg = pg + p_t[:, (g * GROUP + h) * tq:(g * GROUP + h + 1) * tq]
        hi = pg.astype(jnp.bfloat16)
        lo = (pg - hi.astype(jnp.float32)).astype(jnp.bfloat16)
        imp_t = _dot(ovt_ref[...], hi) + _dot(ovt_ref[...], lo)
        work = jnp.where(candidate, imp_t[0:n_blk, :], NEG_INF)
        picked = jnp.zeros((n_blk, tq), jnp.int32)
        for _ in range(n_free):
            top = jnp.max(work, axis=0, keepdims=True)
            first = jnp.min(jnp.where(work == top, j_blk, n_blk), axis=0, keepdims=True)
            hit = j_blk == first
            picked = jnp.where(hit, 1, picked)
            work = jnp.where(hit, NEG_INF, work)
        keep = causal & (take_all | forced | ((picked > 0) & candidate))
        rows.append(jnp.where(keep, 0.0, NEG_INF))
    rows.append(jnp.zeros((LANES - N_KV * n_blk, tq), jnp.float32))
    bias_t = jnp.concatenate(rows, axis=0)
    bias_ref[0] = bias_t.T.astype(jnp.bfloat16)


def _cmp_attn(qa, g_nsa, kc, vct, ovt, n_cmp, tq=2048):
    bsz, seq, _ = qa.shape
    n_blk = seq // SEL_BLOCK
    kern = functools.partial(_cmp_kernel, tq=tq, n_cmp=n_cmp, n_blk=n_blk)
    kv_spec = pl.BlockSpec((1, LANES, LANES), lambda b, i: (b, 0, 0))
    return pl.pallas_call(
        kern,
        out_shape=[jax.ShapeDtypeStruct((bsz, seq, Q_WIDTH), jnp.bfloat16),
                   jax.ShapeDtypeStruct((bsz, seq, LANES), jnp.bfloat16)],
        grid=(bsz, seq // tq),
        in_specs=[pl.BlockSpec((1, tq, Q_WIDTH), lambda b, i: (b, i, 0)),
                  pl.BlockSpec((1, tq, LANES), lambda b, i: (b, i, 0)),
                  kv_spec, kv_spec,
                  pl.BlockSpec((LANES, LANES), lambda b, i: (0, 0))],
        out_specs=[pl.BlockSpec((1, tq, Q_WIDTH), lambda b, i: (b, i, 0)),
                   pl.BlockSpec((1, tq, LANES), lambda b, i: (b, i, 0))],
        compiler_params=_cparams(("arbitrary", "arbitrary")),
        name="cmp_attn",
    )(qa, g_nsa, kc, vct, ovt)


def _sel_kernel(q_ref, gn_ref, bias_ref, k_ref, vt_ref, blk_ref, o_ref, kaug_ref, qaug_ref, s0_ref, s1_ref, acc_ref, m_ref,
                *, tq, tk):
    step = pl.program_id(1)
    nq = tq // LANES
    nk = tk // LANES
    per_g = GROUP * nq

    @pl.when(step == 0)
    def _():
        kaug_ref[:, 0:LANES] = k_ref[0]
        kaug_ref[:, LANES:2 * LANES] = blk_ref[...]

    lane = lax.broadcasted_iota(jnp.int32, (1, LANES), 1)
    n_blk = blk_ref.shape[0] // SEL_BLOCK
    bt = bias_ref[0]
    zero = jnp.zeros((), bt.dtype)
    b0 = jnp.where(lane < n_blk, bt, zero)
    b1 = jnp.where((lane >= n_blk) & (lane < 2 * n_blk), bt, zero)
    qaug_ref[:, 0:LANES] = _stack_heads(q_ref[0])
    qaug_ref[:, LANES:2 * LANES] = jnp.concatenate([b0] * GROUP + [b1] * GROUP, axis=0)

    acc_ref[...] = jnp.zeros_like(acc_ref)
    m_ref[...] = jnp.full_like(m_ref, NEG_INF)
    row = lax.broadcasted_iota(jnp.int32, (tk, tk), 0)
    col = lax.broadcasted_iota(jnp.int32, (tk, tk), 1)
    causal_bias = jnp.where(row <= col, 0.0, NEG_INF)

    def scores(c, s_ref):
        k0 = pl.multiple_of(c * tk, tk)
        s_ref[...] = _dot_nt(kaug_ref[pl.ds(k0, tk), :], qaug_ref[...])

    def accumulate(c, s_ref, diag_half):
        k0 = pl.multiple_of(c * tk, tk)
        for g in range(N_KV):
            ps, alphas = [], []
            for i in range(per_g):
                cb = g * per_g + i
                half, within = divmod(cb % nq, nk)
                cols = slice(cb * LANES, (cb + 1) * LANES)
                if diag_half is not None and half < diag_half:
                    ps.append(jnp.zeros((tk, LANES), jnp.bfloat16))
                    alphas.append(jnp.ones((1, LANES), jnp.float32))
                    continue
                s = s_ref[:, cols]
                if diag_half is not None and half == diag_half:
                    s = s + causal_bias[:, within * LANES:(within + 1) * LANES]
                m_old = m_ref[0:1, cols]
                m_new = jnp.maximum(m_old, jnp.max(s, axis=0, keepdims=True))
                m_ref[0:1, cols] = m_new
                alphas.append(jnp.exp2(m_old - m_new))
                ps.append(jnp.exp2(s - m_new).astype(jnp.bfloat16))
            vt_g = vt_ref[0, g * G_ROWS:(g + 1) * G_ROWS, pl.ds(k0, tk)]
            pv = _dot(vt_g, jnp.concatenate(ps, axis=1))
            acc_ref[g] = jnp.concatenate(alphas, axis=1) * acc_ref[g] + pv

    scores(0, s0_ref)

    def body(i, carry):
        c = 2 * i
        scores(c + 1, s1_ref)
        accumulate(c, s0_ref, None)
        scores(c + 2, s0_ref)
        accumulate(c + 1, s1_ref, None)
        return carry

    lax.fori_loop(0, step, body, 0)
    c0 = 2 * step
    scores(c0 + 1, s1_ref)
    accumulate(c0, s0_ref, 0)
    accumulate(c0 + 1, s1_ref, 1)

    o_g = [acc_ref[g] for g in range(N_KV)]
    gates = _branch_gates(gn_ref[0], 1, tq)
    scale = [gates[g] / o_g[g][HEAD_DIM:HEAD_DIM + 1, :] for g in range(N_KV)]
    _store_heads(o_ref, slice(None), o_g, scale, tq)


def _sel_attn(qa, g_nsa, bias, ks, vst, blk_onehot, tk=256):
    bsz, seq, _ = qa.shape
    tq = 2 * tk
    kern = functools.partial(_sel_kernel, tq=tq, tk=tk)
    return pl.pallas_call(
        kern,
        out_shape=jax.ShapeDtypeStruct((bsz, seq, Q_WIDTH), jnp.bfloat16),
        grid=(bsz, seq // tq),
        in_specs=[pl.BlockSpec((1, tq, Q_WIDTH), lambda b, i: (b, i, 0)),
                  pl.BlockSpec((1, tq, LANES), lambda b, i: (b, i, 0)),
                  pl.BlockSpec((1, tq, LANES), lambda b, i: (b, i, 0)),
                  pl.BlockSpec((1, seq, LANES), lambda b, i: (b, 0, 0)),
                  pl.BlockSpec((1, V_ROWS, seq), lambda b, i: (b, 0, 0)),
                  pl.BlockSpec((seq, LANES), lambda b, i: (0, 0))],
        out_specs=pl.BlockSpec((1, tq, Q_WIDTH), lambda b, i: (b, i, 0)),
        scratch_shapes=[pltpu.VMEM((seq, 2 * LANES), jnp.bfloat16),
                        pltpu.VMEM((N_HEADS * tq, 2 * LANES), jnp.bfloat16),
                        pltpu.VMEM((tk, N_HEADS * tq), jnp.float32),
                        pltpu.VMEM((tk, N_HEADS * tq), jnp.float32),
                        pltpu.VMEM((N_KV, G_ROWS, GROUP * tq), jnp.float32),
                        pltpu.VMEM((SUBLANES, N_HEADS * tq), jnp.float32)],
        compiler_params=_cparams(("arbitrary", "arbitrary")),
        name="sel_attn",
    )(qa, g_nsa, bias, ks, vst, blk_onehot)


def _band_kernel(*refs, tq, n_sub, span, window, use_sink):
    if use_sink:
        q_ref, k_ref, vt_ref, sink_ref, o_ref = refs
    else:
        q_ref, k_ref, vt_ref, gn_ref, o_ref = refs
    step = pl.program_id(1)
    seq = k_ref.shape[1]
    nq = tq // LANES
    row = lax.broadcasted_iota(jnp.int32, (span, tq), 0)
    col = lax.broadcasted_iota(jnp.int32, (span, tq), 1)
    diff = col - row

    def scores(sub):
        q0 = (step * n_sub + sub) * tq
        start = pl.multiple_of(jnp.clip(q0 + tq - span, 0, seq - span), LANES)
        rel = diff + (q0 - start)
        bias = jnp.where((rel >= 0) & (rel < window), 0.0, NEG_INF)
        qs = _stack_heads(q_ref[0, sub * tq:(sub + 1) * tq, :])
        return start, bias, _dot_nt(k_ref[0, pl.ds(start, span), :], qs)

    def finish(sub, start, bias, s_t):
        ps, extra = [], []
        for cb in range(N_HEADS * nq):
            s = s_t[:, cb * LANES:(cb + 1) * LANES] + bias[:, (cb % nq) * LANES:(cb % nq + 1) * LANES]
            m = jnp.max(s, axis=0, keepdims=True)
            if use_sink:
                sink = sink_ref[cb // nq] * LOG2E
                m = jnp.maximum(m, sink)
                extra.append(jnp.exp2(sink - m))
            ps.append(jnp.exp2(s - m).astype(jnp.bfloat16))
        o_g, inv_l = [], []
        for g in range(N_KV):
            per_g = GROUP * nq
            vt_g = vt_ref[0, g * G_ROWS:(g + 1) * G_ROWS, pl.ds(start, span)]
            o = _dot(vt_g, jnp.concatenate(ps[g * per_g:(g + 1) * per_g], axis=1))
            l = o[HEAD_DIM:HEAD_DIM + 1, :]
            if use_sink:
                l = l + jnp.concatenate(extra[g * per_g:(g + 1) * per_g], axis=1)
            o_g.append(o)
            inv_l.append(1.0 / l)
        if not use_sink:
            gates = _branch_gates(gn_ref[0, sub * tq:(sub + 1) * tq, :], 2, tq)
            inv_l = [a * b for a, b in zip(inv_l, gates)]
        _store_heads(o_ref, slice(sub * tq, (sub + 1) * tq), o_g, inv_l, tq)

    nxt = scores(0)
    for sub in range(n_sub):
        cur = nxt
        if sub + 1 < n_sub:
            nxt = scores(sub + 1)
        finish(sub, *cur)


def _band_attn(q, k, vt, window, sinks=None, g_nsa=None, tq=128, n_sub=16):
    bsz, seq, _ = q.shape
    span = (-(-(window - 1) // tq)) * tq + tq
    use_sink = sinks is not None
    kern = functools.partial(_band_kernel, tq=tq, n_sub=n_sub, span=span, window=window, use_sink=use_sink)
    ts = tq * n_sub
    in_specs = [pl.BlockSpec((1, ts, Q_WIDTH), lambda b, i: (b, i, 0)),
                pl.BlockSpec((1, seq, LANES), lambda b, i: (b, 0, 0)),
                pl.BlockSpec((1, V_ROWS, seq), lambda b, i: (b, 0, 0))]
    args = [q, k, vt]
    if use_sink:
        in_specs.append(pl.BlockSpec(memory_space=pltpu.SMEM))
        args.append(sinks)
    else:
        in_specs.append(pl.BlockSpec((1, ts, LANES), lambda b, i: (b, i, 0)))
        args.append(g_nsa)
    return pl.pallas_call(
        kern,
        out_shape=jax.ShapeDtypeStruct((bsz, seq, Q_WIDTH), jnp.bfloat16),
        grid=(bsz, seq // ts),
        in_specs=in_specs,
        out_specs=pl.BlockSpec((1, ts, Q_WIDTH), lambda b, i: (b, i, 0)),
        compiler_params=_cparams(("arbitrary", "arbitrary")),
        name="band_attn_sink" if use_sink else "band_attn",
    )(*args)


def _merge_kernel(x_ref, mod_ref, wza_ref, wzb_ref, wgm_ref, ocmp_ref, osel_ref, owin_ref, ob_ref,
                  wua_ref, wub_ref, wo_ref, lng_ref, lnb_ref, o_ref, *, ts, n_sub):
    shift = mod_ref[0, 0:1, :]
    scale = mod_ref[0, 1:2, :]
    gate = mod_ref[0, 2:3, :]

    def project(r):
        u = (_layer_norm(x_ref[0, r * ts:(r + 1) * ts, :]) * (1.0 + scale) + shift).astype(jnp.bfloat16)
        return _dot(u, wza_ref[...]), _dot(u, wzb_ref[...]), _dot_nt(u, wgm_ref[...])

    def finish(r, z_a, z_b, g_m):
        rows = slice(r * ts, (r + 1) * ts)
        o_a = (ocmp_ref[0, rows, :].astype(jnp.float32) + osel_ref[0, rows, :].astype(jnp.float32)
               + owin_ref[0, rows, :].astype(jnp.float32))
        y_a = (o_a * (z_a * _sigmoid(z_a))).astype(jnp.bfloat16)
        y_b = (ob_ref[0, rows, :].astype(jnp.float32) * (z_b * _sigmoid(z_b))).astype(jnp.bfloat16)
        merged = _sigmoid(g_m[:, 0:D_MODEL]) * _dot(y_a, wua_ref[...])
        merged = merged + _sigmoid(g_m[:, D_MODEL:2 * D_MODEL]) * _dot(y_b, wub_ref[...])
        out = _dot(merged.astype(jnp.bfloat16), wo_ref[...])
        y = DEEPNORM_ALPHA * x_ref[0, rows, :] + gate * out
        o_ref[0, rows, :] = _layer_norm(y) * lng_ref[...] + lnb_ref[...]

    nxt = project(0)
    for r in range(n_sub):
        cur = nxt
        if r + 1 < n_sub:
            nxt = project(r + 1)
        finish(r, *cur)


def _merge_out(x, mod3, wgs, o_cmp, o_sel, o_win, o_b, wua, wub, wo, ln_g, ln_b, ts=256, n_sub=4):
    bsz, seq, _ = x.shape
    tm = ts * n_sub
    tok = lambda w: pl.BlockSpec((1, tm, w), lambda b, i: (b, i, 0))
    full = lambda a: pl.BlockSpec(a.shape, lambda b, i: (0,) * a.ndim)
    return pl.pallas_call(
        functools.partial(_merge_kernel, ts=ts, n_sub=n_sub),
        out_shape=jax.ShapeDtypeStruct(x.shape, jnp.float32),
        grid=(bsz, seq // tm),
        in_specs=[tok(D_MODEL), pl.BlockSpec((1, 3, D_MODEL), lambda b, i: (b, 0, 0)),
                  *[full(w) for w in wgs], tok(Q_WIDTH), tok(Q_WIDTH), tok(Q_WIDTH), tok(Q_WIDTH),
                  full(wua), full(wub), full(wo), full(ln_g), full(ln_b)],
        out_specs=tok(D_MODEL),
        compiler_params=_cparams(("arbitrary", "arbitrary")),
        name="merge_out",
    )(x, mod3, *wgs, o_cmp, o_sel, o_win, o_b, wua, wub, wo, ln_g, ln_b)


def _pair_q_cols(w):
    k = w.shape[0]
    return w.reshape(k, N_KV, GROUP, 2, HALF).transpose(0, 2, 3, 1, 4).reshape(k, Q_WIDTH)


def _pair_k_cols(w):
    k = w.shape[0]
    return w.reshape(k, N_KV, 2, HALF).transpose(0, 2, 1, 3).reshape(k, KV_WIDTH)


def _pair_o_cols(w):
    k = w.shape[0]
    return w.reshape(k, N_KV, GROUP, HEAD_DIM).transpose(0, 2, 1, 3).reshape(k, Q_WIDTH)


def _pair_o_rows(w):
    n = w.shape[1]
    return w.reshape(N_KV, GROUP, HEAD_DIM, n).transpose(1, 0, 2, 3).reshape(Q_WIDTH, n)


def _prep_qkv_weight(w_t):
    q_scale = HEAD_DIM ** -0.5
    sec = lambda off, width: w_t[off:off + width].T
    cols = [_pair_q_cols(sec(OFF_QA, Q_WIDTH)) * q_scale,
            _pair_k_cols(sec(OFF_KVC, KV_WIDTH)), _pair_k_cols(sec(OFF_KVS, KV_WIDTH)),
            _pair_k_cols(sec(OFF_KVW, KV_WIDTH)),
            _pair_q_cols(sec(OFF_QB, Q_WIDTH)) * q_scale,
            _pair_k_cols(sec(OFF_KVB, KV_WIDTH)),
            sec(OFF_KVC + KV_WIDTH, KV_WIDTH),
            sec(OFF_GN, 3 * N_HEADS), jnp.zeros((D_MODEL, LANES - 3 * N_HEADS), w_t.dtype)]
    v_cols = [sec(OFF_KVS + KV_WIDTH, KV_WIDTH), sec(OFF_KVW + KV_WIDTH, KV_WIDTH),
              sec(OFF_KVB + KV_WIDTH, KV_WIDTH)]
    return jnp.concatenate(cols, axis=1), jnp.concatenate(v_cols, axis=1).T


def _prep_gate_weights(w_t):
    return [_pair_o_cols(w_t[OFF_ZA:OFF_ZA + Q_WIDTH].T), _pair_o_cols(w_t[OFF_ZB:OFF_ZB + Q_WIDTH].T),
            w_t[OFF_GM:OFF_GM + 2 * D_MODEL]]


def _prep_compress(pos, w1, w2, paired):
    half_len = CMP_LEN // 2
    w = w1.reshape(CMP_LEN, HEAD_DIM, CMP_HIDDEN).astype(jnp.bfloat16)
    w2 = w2.astype(jnp.bfloat16)
    pieces = ([(g, slice(h * HALF, (h + 1) * HALF)) for h in range(2) for g in range(N_KV)] if paired
              else [(g, slice(0, HEAD_DIM)) for g in range(N_KV)])
    own = lambda g, blk: jnp.concatenate([blk if k == g else jnp.zeros_like(blk) for k in range(N_KV)], axis=-1)
    w1x = jnp.concatenate([own(g, w[:, d, :]) for g, d in pieces], axis=1)
    first = w1x[:half_len].reshape(half_len * LANES, N_KV * CMP_HIDDEN)
    second = w1x[half_len:].reshape(half_len * LANES, N_KV * CMP_HIDDEN)
    w1e = jnp.concatenate([first, second], axis=1)
    w2e = jnp.concatenate([jnp.concatenate([w2[:, d] if k == g else jnp.zeros_like(w2[:, d]) for g, d in pieces], axis=1)
                           for k in range(N_KV)], axis=0)
    pos_l = jnp.concatenate([pos[:, d] for _, d in pieces], axis=1)
    pos_rows = jnp.concatenate([pos_l[:half_len].reshape(1, -1), pos_l[half_len:].reshape(1, -1),
                                jnp.zeros((SUBLANES - 2, half_len * LANES), pos.dtype)], axis=0).astype(jnp.bfloat16)
    return w1e, pos_rows, w2e


def _overlap_t(n_cmp, n_blk):
    c0 = np.arange(n_cmp) * CMP_STRIDE
    j0 = np.arange(n_blk) * SEL_BLOCK
    overlap = (c0[:, None] < j0[None, :] + SEL_BLOCK) & (c0[:, None] + CMP_LEN > j0[None, :])
    ovt = np.zeros((LANES, LANES), np.float32)
    ovt[:n_blk, :n_cmp] = overlap.T
    return jnp.asarray(ovt, jnp.bfloat16)


def _block_onehot(seq):
    n_blk = seq // SEL_BLOCK
    e = np.zeros((seq, LANES), np.float32)
    pos = np.arange(seq)
    e[pos, pos // SEL_BLOCK] = 1.0
    e[pos, n_blk + pos // SEL_BLOCK] = 1.0
    return jnp.asarray(e, jnp.bfloat16)


def kernel(x, c, positions, w_ada, b_ada, w_in, cmp_pos_k, cmp_w1_k, cmp_w2_k,
           cmp_pos_v, cmp_w1_v, cmp_w2_v, sinks, w_up_a, w_up_b, w_out, ln_g, ln_b):
    bsz, seq, _ = x.shape
    n_cmp = (seq - CMP_LEN) // CMP_STRIDE + 1
    n_blk = seq // SEL_BLOCK
    assert seq // CMP_STRIDE == LANES and 2 * n_blk <= LANES
    cos_t, sin_t = _rope_tables(positions, QKV_ROWS)
    for l in range(DEPTH):
        mod3 = _adaln_mod(c, w_ada[l], b_ada[l]).reshape(bsz, 3, D_MODEL)
        w_t = jnp.swapaxes(w_in[l], 0, 1).astype(jnp.bfloat16)
        w_qk, w_vt = _prep_qkv_weight(w_t)
        qa, k_c, k_s, k_w, qb, k_b, v_c, vt_s, vt_w, vt_b, g_nsa = _qkv_proj(x, mod3, w_qk, w_vt, cos_t, sin_t)

        w1k, posk, w2k = _prep_compress(cmp_pos_k[l], cmp_w1_k[l], cmp_w2_k[l], paired=True)
        w1v, posv, w2v = _prep_compress(cmp_pos_v[l], cmp_w1_v[l], cmp_w2_v[l], paired=False)
        kc, vct = _compress(k_c, v_c, w1k, w1v, posk, posv, w2k, w2v.T, bb=8 if bsz % 8 == 0 else 1)

        o_cmp, sel_bias = _cmp_attn(qa, g_nsa, kc, vct, _overlap_t(n_cmp, n_blk), n_cmp)
        o_sel = _sel_attn(qa, g_nsa, sel_bias, k_s, vt_s, _block_onehot(seq))
        o_win = _band_attn(qa, k_w, vt_w, NSA_WINDOW, g_nsa=g_nsa)
        o_b = _band_attn(qb, k_b, vt_b, SWA_WINDOW, sinks[l].astype(jnp.float32))

        x = _merge_out(x, mod3, _prep_gate_weights(w_t),
                       o_cmp, o_sel, o_win, o_b,
                       _pair_o_rows(w_up_a[l]).astype(jnp.bfloat16), _pair_o_rows(w_up_b[l]).astype(jnp.bfloat16),
                       w_out[l].astype(jnp.bfloat16), ln_g[l].reshape(1, D_MODEL), ln_b[l].reshape(1, D_MODEL))
    return x
```

```python
import functools

import numpy as np
import jax
import jax.numpy as jnp
from jax import lax
from jax.experimental import pallas as pl
from jax.experimental.pallas import tpu as pltpu

D_MODEL = 1024
HEAD_DIM = 64
HALF = HEAD_DIM // 2
N_HEADS = 8
N_KV = 2
GROUP = N_HEADS // N_KV
Q_WIDTH = N_HEADS * HEAD_DIM
KV_WIDTH = N_KV * HEAD_DIM
CMP_LEN = 32
CMP_STRIDE = 16
CMP_HIDDEN = 256
SEL_BLOCK = 64
SEL_TOPN = 8
NSA_WINDOW = 512
SWA_WINDOW = 128
ROPE_THETA = 10000.0
LN_EPS = 1e-5
NEG_INF = -1e30
FORCE_SCORE = 1e9
DEPTH = 1
DEEPNORM_ALPHA = (2 * DEPTH) ** 0.25
LANES = 128
SUBLANES = 8
BF16_SUBLANES = 16
ROPE_PACK = LANES // HALF
QKV_ROWS = 256
LOG2E = 1.4426950408889634
G_ROWS = HEAD_DIM + BF16_SUBLANES
V_ROWS = N_KV * G_ROWS
VMEM_LIMIT = 56 * 1024 * 1024

_SPLITS = [Q_WIDTH, 2 * KV_WIDTH, 2 * KV_WIDTH, 2 * KV_WIDTH, 3 * N_HEADS, Q_WIDTH,
           Q_WIDTH, 2 * KV_WIDTH, Q_WIDTH, 2 * D_MODEL]
_OFFS = [int(o) for o in np.cumsum([0] + _SPLITS)]
(OFF_QA, OFF_KVC, OFF_KVS, OFF_KVW, OFF_GN, OFF_ZA, OFF_QB, OFF_KVB, OFF_ZB, OFF_GM) = _OFFS[:10]


def _cparams(sem):
    return pltpu.CompilerParams(dimension_semantics=sem, vmem_limit_bytes=VMEM_LIMIT)


def _dot(a, b):
    return jnp.dot(a, b, preferred_element_type=jnp.float32)


def _dot_nt(a, b):
    return lax.dot_general(a, b, (((1,), (1,)), ((), ())), preferred_element_type=jnp.float32)


def _layer_norm(x):
    mu = jnp.mean(x, axis=-1, keepdims=True)
    xc = x - mu
    var = jnp.mean(xc * xc, axis=-1, keepdims=True)
    return xc * lax.rsqrt(var + LN_EPS)


def _sigmoid(x):
    return 1.0 / (1.0 + jnp.exp(-x))


def _adaln_kernel(c_ref, w_ref, b_ref, o_ref):
    c = c_ref[...]
    a = c * _sigmoid(c)
    o_ref[...] = jnp.dot(a, w_ref[...], preferred_element_type=jnp.float32,
                         precision=lax.Precision.HIGHEST) + b_ref[...]


def _adaln_mod(c, w_ada, b_ada):
    bsz = c.shape[0]
    n = w_ada.shape[1]
    tn = 1024
    return pl.pallas_call(
        _adaln_kernel,
        out_shape=jax.ShapeDtypeStruct((bsz, n), jnp.float32),
        grid=(n // tn,),
        in_specs=[pl.BlockSpec((bsz, D_MODEL), lambda j: (0, 0)),
                  pl.BlockSpec((D_MODEL, tn), lambda j: (0, j)),
                  pl.BlockSpec((1, tn), lambda j: (0, j))],
        out_specs=pl.BlockSpec((bsz, tn), lambda j: (0, j)),
        compiler_params=_cparams(("arbitrary",)),
        name="adaln_mod",
    )(c, w_ada, b_ada.reshape(1, n))


def _rope_table_kernel(pos_ref, invf_ref, cos_ref, sin_ref):
    pos = pos_ref[...]
    rows = pos.shape[0]
    ang = jnp.concatenate([jnp.broadcast_to(pos[:, k:k + 1], (rows, HALF)) for k in range(ROPE_PACK)],
                          axis=1) * invf_ref[...]
    cos_ref[...] = jnp.cos(ang)
    sin_ref[...] = jnp.sin(ang)


def _rope_tables(positions, ts):
    bsz, seq = positions.shape
    sub = ts // ROPE_PACK
    rows = bsz * seq // ROPE_PACK
    pos = positions.astype(jnp.float32).reshape(bsz, seq // ts, ROPE_PACK, sub).transpose(0, 1, 3, 2)
    pos = pos.reshape(rows, ROPE_PACK)
    inv_freq = ROPE_THETA ** (-jnp.arange(HALF, dtype=jnp.float32) / HALF)
    invf = jnp.broadcast_to(inv_freq[None, :], (ROPE_PACK, HALF)).reshape(1, LANES)
    tr = seq // ROPE_PACK
    cos, sin = pl.pallas_call(
        _rope_table_kernel,
        out_shape=[jax.ShapeDtypeStruct((rows, LANES), jnp.float32)] * 2,
        grid=(rows // tr,),
        in_specs=[pl.BlockSpec((tr, ROPE_PACK), lambda i: (i, 0)),
                  pl.BlockSpec((1, LANES), lambda i: (0, 0))],
        out_specs=[pl.BlockSpec((tr, LANES), lambda i: (i, 0))] * 2,
        compiler_params=_cparams(("arbitrary",)),
        name="rope_table",
    )(pos, invf)
    return cos.reshape(bsz, tr, LANES), sin.reshape(bsz, tr, LANES)


N_QK_SLABS = 14
SLAB_QA, SLAB_KC, SLAB_KS, SLAB_KW, SLAB_QB, SLAB_KB, SLAB_VC, SLAB_GN = 0, 4, 5, 6, 7, 11, 12, 13
N_VT = 3


def _qkv_kernel(x_ref, mod_ref, w_ref, wvt_ref, cos_ref, sin_ref,
                qa_ref, kc_ref, ks_ref, kw_ref, qb_ref, kb_ref, vc_ref,
                vst_ref, vwt_ref, vbt_ref, gn_ref, kstage_ref, vstage_ref, *, ts, n_sub):
    shift = mod_ref[0, 0:1, :]
    scale = mod_ref[0, 1:2, :]

    def project(r):
        u = (_layer_norm(x_ref[0, r * ts:(r + 1) * ts, :]) * (1.0 + scale) + shift).astype(jnp.bfloat16)
        return _dot(u, w_ref[...]), _dot_nt(wvt_ref[...], u)

    def finish(r, acc, vt):
        rows = slice(r * ts, (r + 1) * ts)
        sub = ts // ROPE_PACK
        for k in range(ROPE_PACK):
            c32 = cos_ref[0, r * sub:(r + 1) * sub, k * HALF:(k + 1) * HALF]
            s32 = sin_ref[0, r * sub:(r + 1) * sub, k * HALF:(k + 1) * HALF]
            cos = jnp.concatenate([c32, c32, c32, c32], axis=1)
            sin = jnp.concatenate([-s32, -s32, s32, s32], axis=1)
            tok = slice(r * ts + k * sub, r * ts + (k + 1) * sub)

            def rope(slab):
                t = acc[k * sub:(k + 1) * sub, slab * LANES:(slab + 1) * LANES]
                return t * cos + pltpu.roll(t, LANES // 2, 1) * sin

            for j in range(GROUP):
                qa_ref[0, tok, j * LANES:(j + 1) * LANES] = (rope(SLAB_QA + j) * LOG2E).astype(jnp.bfloat16)
                qb_ref[0, tok, j * LANES:(j + 1) * LANES] = (rope(SLAB_QB + j) * LOG2E).astype(jnp.bfloat16)
            for slab, ref in ((SLAB_KS, ks_ref), (SLAB_KW, kw_ref), (SLAB_KB, kb_ref)):
                ref[0, tok, :] = rope(slab).astype(jnp.bfloat16)
            kstage_ref[k * sub:(k + 1) * sub, :] = rope(SLAB_KC)
        vstage_ref[...] = acc[:, SLAB_VC * LANES:(SLAB_VC + 1) * LANES]
        gn_ref[0, rows, :] = acc[:, SLAB_GN * LANES:(SLAB_GN + 1) * LANES]
        n_ch = ts // CMP_STRIDE
        for l in range(CMP_STRIDE):
            for stage, ref in ((kstage_ref, kc_ref), (vstage_ref, vc_ref)):
                ref[0, r * n_ch:(r + 1) * n_ch, l * LANES:(l + 1) * LANES] = (
                    stage[pl.ds(l, n_ch, stride=CMP_STRIDE), :].astype(jnp.bfloat16))
        ones = jnp.ones((G_ROWS - HEAD_DIM, ts), jnp.bfloat16)
        for j, ref in enumerate((vst_ref, vwt_ref, vbt_ref)):
            for g in range(N_KV):
                lo = j * KV_WIDTH + g * HEAD_DIM
                ref[0, g * G_ROWS:g * G_ROWS + HEAD_DIM, rows] = vt[lo:lo + HEAD_DIM].astype(jnp.bfloat16)
                ref[0, g * G_ROWS + HEAD_DIM:(g + 1) * G_ROWS, rows] = ones

    nxt = project(0)
    for r in range(n_sub):
        cur = nxt
        if r + 1 < n_sub:
            nxt = project(r + 1)
        finish(r, *cur)


def _qkv_proj(x, mod3, w_qk, w_vt, cos_t, sin_t, ts=QKV_ROWS, n_sub=8):
    bsz, seq, _ = x.shape
    tm = ts * n_sub
    chunk_w = CMP_STRIDE * KV_WIDTH
    q_shape = jax.ShapeDtypeStruct((bsz, seq, Q_WIDTH), jnp.bfloat16)
    k_shape = jax.ShapeDtypeStruct((bsz, seq, KV_WIDTH), jnp.bfloat16)
    c_shape = jax.ShapeDtypeStruct((bsz, seq // CMP_STRIDE, chunk_w), jnp.bfloat16)
    vt_shape = jax.ShapeDtypeStruct((bsz, V_ROWS, seq), jnp.bfloat16)
    q_spec = pl.BlockSpec((1, tm, Q_WIDTH), lambda b, i: (b, i, 0))
    k_spec = pl.BlockSpec((1, tm, KV_WIDTH), lambda b, i: (b, i, 0))
    c_spec = pl.BlockSpec((1, tm // CMP_STRIDE, chunk_w), lambda b, i: (b, i, 0))
    t_spec = pl.BlockSpec((1, tm // ROPE_PACK, LANES), lambda b, i: (b, i, 0))
    vt_spec = pl.BlockSpec((1, V_ROWS, tm), lambda b, i: (b, 0, i))
    return pl.pallas_call(
        functools.partial(_qkv_kernel, ts=ts, n_sub=n_sub),
        out_shape=[q_shape, c_shape, k_shape, k_shape, q_shape, k_shape, c_shape,
                   vt_shape, vt_shape, vt_shape, jax.ShapeDtypeStruct((bsz, seq, LANES), jnp.float32)],
        grid=(bsz, seq // tm),
        in_specs=[pl.BlockSpec((1, tm, D_MODEL), lambda b, i: (b, i, 0)),
                  pl.BlockSpec((1, 3, D_MODEL), lambda b, i: (b, 0, 0)),
                  pl.BlockSpec((D_MODEL, N_QK_SLABS * LANES), lambda b, i: (0, 0)),
                  pl.BlockSpec((N_VT * KV_WIDTH, D_MODEL), lambda b, i: (0, 0)),
                  t_spec, t_spec],
        out_specs=[q_spec, c_spec, k_spec, k_spec, q_spec, k_spec, c_spec,
                   vt_spec, vt_spec, vt_spec, k_spec],
        scratch_shapes=[pltpu.VMEM((ts, KV_WIDTH), jnp.float32), pltpu.VMEM((ts, KV_WIDTH), jnp.float32)],
        compiler_params=_cparams(("arbitrary", "arbitrary")),
        name="qkv_proj",
    )(x, mod3, w_qk, w_vt, cos_t, sin_t)


def _compress_kernel(xk_ref, xv_ref, w1k_ref, w1v_ref, posk_ref, posv_ref, w2k_ref, w2vt_ref,
                     kc_ref, vct_ref):
    bb, n_chunks, cw = xk_ref.shape

    def hidden(x_ref, w1_ref, pos_ref):
        hw = w1_ref.shape[1] // 2
        p = _dot(x_ref[...].reshape(bb * n_chunks, cw), w1_ref[...])
        pt = _dot(pos_ref[...], w1_ref[...])
        pos_term = pt[0:1, 0:hw] + pt[1:2, hw:]
        first = p[:, 0:hw]
        second = pltpu.roll(p[:, hw:], bb * n_chunks - 1, 0)
        hid = first + second + pos_term
        return (hid * _sigmoid(hid)).astype(jnp.bfloat16)

    kc = _dot(hidden(xk_ref, w1k_ref, posk_ref), w2k_ref[...]).astype(jnp.bfloat16)
    vct = _dot_nt(w2vt_ref[...], hidden(xv_ref, w1v_ref, posv_ref)).astype(jnp.bfloat16)
    for i in range(bb):
        kc_ref[i] = kc[i * n_chunks:(i + 1) * n_chunks]
        vct_ref[i] = vct[:, i * n_chunks:(i + 1) * n_chunks]


def _compress(xk, xv, w1k, w1v, posk, posv, w2k, w2vt, bb=4):
    bsz, n_chunks, cw = xk.shape
    hw2 = w1k.shape[1]
    out = jax.ShapeDtypeStruct((bsz, n_chunks, LANES), jnp.bfloat16)
    x_spec = pl.BlockSpec((bb, n_chunks, cw), lambda b: (b, 0, 0))
    w1_spec = pl.BlockSpec((cw, hw2), lambda b: (0, 0))
    pos_spec = pl.BlockSpec((SUBLANES, cw), lambda b: (0, 0))
    o_spec = pl.BlockSpec((bb, n_chunks, LANES), lambda b: (b, 0, 0))
    return pl.pallas_call(
        _compress_kernel,
        out_shape=[out, out],
        grid=(bsz // bb,),
        in_specs=[x_spec, x_spec, w1_spec, w1_spec, pos_spec, pos_spec,
                  pl.BlockSpec(w2k.shape, lambda b: (0, 0)), pl.BlockSpec(w2vt.shape, lambda b: (0, 0))],
        out_specs=[o_spec, o_spec],
        compiler_params=_cparams(("arbitrary",)),
        name="compress",
    )(xk, xv, w1k, w1v, posk, posv, w2k, w2vt)


def _stack_heads(q):
    lane = lax.broadcasted_iota(jnp.int32, (1, LANES), 1)
    is_a = (lane & (HEAD_DIM - 1)) < HALF
    zero = jnp.zeros((), q.dtype)
    tiles = [q[:, j * LANES:(j + 1) * LANES] for j in range(GROUP)]
    parts = [jnp.where(is_a, t, zero) for t in tiles] + [jnp.where(is_a, zero, t) for t in tiles]
    return jnp.concatenate(parts, axis=0)


def _store_heads(o_ref, rows, o_g, inv_l, tq):
    for j in range(GROUP):
        cols = slice(j * tq, (j + 1) * tq)
        halves = [o_g[g][0:HEAD_DIM, cols] for g in range(N_KV)]
        if inv_l is not None:
            halves = [h * inv_l[g][:, cols] for g, h in enumerate(halves)]
        tile_t = jnp.concatenate(halves, axis=0)
        o_ref[0, rows, j * LANES:(j + 1) * LANES] = tile_t.T.astype(jnp.bfloat16)


def _branch_gates(gn, branch, tq):
    sig = _sigmoid(gn.T[0:3 * N_HEADS, :])
    return [jnp.concatenate([sig[3 * (GROUP * g + j) + branch:3 * (GROUP * g + j) + branch + 1, :]
                             for j in range(GROUP)], axis=1) for g in range(N_KV)]


def _group_cols(g, tq):
    return slice(g * GROUP * tq, (g + 1) * GROUP * tq)


def _cmp_kernel(q_ref, gn_ref, kc_ref, vct_ref, ovt_ref, o_ref, bias_ref, *, tq, n_cmp, n_blk):
    qi = pl.program_id(1)
    nq = tq // LANES
    qs = _stack_heads(q_ref[0])
    c_row = lax.broadcasted_iota(jnp.int32, (LANES, tq), 0)
    t_col = qi * tq + lax.broadcasted_iota(jnp.int32, (LANES, tq), 1)
    vis = (c_row * CMP_STRIDE + (CMP_LEN - 1) <= t_col) & (c_row < n_cmp)
    any_vis = (t_col[0:1, :] >= CMP_LEN - 1).astype(jnp.float32)
    s_t = _dot_nt(kc_ref[0], qs)
    p_parts = []
    for cb in range(N_HEADS * nq):
        qq = slice((cb % nq) * LANES, (cb % nq + 1) * LANES)
        s = jnp.where(vis[:, qq], s_t[:, cb * LANES:(cb + 1) * LANES], NEG_INF)
        e = jnp.exp2(s - jnp.max(s, axis=0, keepdims=True))
        p_parts.append(e * (any_vis[:, qq] / jnp.sum(e, axis=0, keepdims=True)))
    p_t = jnp.concatenate(p_parts, axis=1)
    o_g = [_dot(vct_ref[0, g * HEAD_DIM:(g + 1) * HEAD_DIM, :], p_t[:, _group_cols(g, tq)].astype(jnp.bfloat16))
           for g in range(N_KV)]
    _store_heads(o_ref, slice(None), o_g, _branch_gates(gn_ref[0], 0, tq), tq)

    j_blk = lax.broadcasted_iota(jnp.int32, (n_blk, tq), 0)
    t_q = qi * tq + lax.broadcasted_iota(jnp.int32, (n_blk, tq), 1)
    cur = jnp.right_shift(t_q, SEL_BLOCK.bit_length() - 1)
    forced = (j_blk == 0) | (j_blk == cur) | (j_blk == cur - 1)
    causal = j_blk <= cur
    n_free = SEL_TOPN - 3
    take_all = cur < SEL_TOPN
    candidate = causal & jnp.logical_not(forced)
    rows = []
    for g in range(N_KV):
        pg = p_t[:, g * GROUP * tq:(g * GROUP + 1) * tq]
        for h in range(1, GROUP):
            pg = pg + p_t[:, (g * GROUP + h) * tq:(g * GROUP + h + 1) * tq]
        hi = pg.astype(jnp.bfloat16)
        lo = (pg - hi.astype(jnp.float32)).astype(jnp.bfloat16)
        imp_t = _dot(ovt_ref[...], hi) + _dot(ovt_ref[...], lo)
        work = jnp.where(candidate, imp_t[0:n_blk, :], NEG_INF)
        picked = jnp.zeros((n_blk, tq), jnp.int32)
        for _ in range(n_free):
            top = jnp.max(work, axis=0, keepdims=True)
            first = jnp.min(jnp.where(work == top, j_blk, n_blk), axis=0, keepdims=True)
            hit = j_blk == first
            picked = jnp.where(hit, 1, picked)
            work = jnp.where(hit, NEG_INF, work)
        keep = causal & (take_all | forced | ((picked > 0) & candidate))
        rows.append(jnp.where(keep, 0.0, NEG_INF))
    rows.append(jnp.zeros((LANES - N_KV * n_blk, tq), jnp.float32))
    bias_t = jnp.concatenate(rows, axis=0)
    bias_ref[0] = bias_t.T.astype(jnp.bfloat16)


def _cmp_attn(qa, g_nsa, kc, vct, ovt, n_cmp, tq=2048):
    bsz, seq, _ = qa.shape
    n_blk = seq // SEL_BLOCK
    kern = functools.partial(_cmp_kernel, tq=tq, n_cmp=n_cmp, n_blk=n_blk)
    kv_spec = pl.BlockSpec((1, LANES, LANES), lambda b, i: (b, 0, 0))
    return pl.pallas_call(
        kern,
        out_shape=[jax.ShapeDtypeStruct((bsz, seq, Q_WIDTH), jnp.bfloat16),
                   jax.ShapeDtypeStruct((bsz, seq, LANES), jnp.bfloat16)],
        grid=(bsz, seq // tq),
        in_specs=[pl.BlockSpec((1, tq, Q_WIDTH), lambda b, i: (b, i, 0)),
                  pl.BlockSpec((1, tq, LANES), lambda b, i: (b, i, 0)),
                  kv_spec, kv_spec,
                  pl.BlockSpec((LANES, LANES), lambda b, i: (0, 0))],
        out_specs=[pl.BlockSpec((1, tq, Q_WIDTH), lambda b, i: (b, i, 0)),
                   pl.BlockSpec((1, tq, LANES), lambda b, i: (b, i, 0))],
        compiler_params=_cparams(("arbitrary", "arbitrary")),
        name="cmp_attn",
    )(qa, g_nsa, kc, vct, ovt)


def _sel_kernel(q_ref, gn_ref, bias_ref, k_ref, vt_ref, blk_ref, o_ref, kaug_ref, qaug_ref, s0_ref, s1_ref, acc_ref, m_ref,
                *, tq, tk):
    step = pl.program_id(1)
    nq = tq // LANES
    nk = tk // LANES
    per_g = GROUP * nq

    @pl.when(step == 0)
    def _():
        kaug_ref[:, 0:LANES] = k_ref[0]
        kaug_ref[:, LANES:2 * LANES] = blk_ref[...]

    lane = lax.broadcasted_iota(jnp.int32, (1, LANES), 1)
    n_blk = blk_ref.shape[0] // SEL_BLOCK
    bt = bias_ref[0]
    zero = jnp.zeros((), bt.dtype)
    b0 = jnp.where(lane < n_blk, bt, zero)
    b1 = jnp.where((lane >= n_blk) & (lane < 2 * n_blk), bt, zero)
    qaug_ref[:, 0:LANES] = _stack_heads(q_ref[0])
    qaug_ref[:, LANES:2 * LANES] = jnp.concatenate([b0] * GROUP + [b1] * GROUP, axis=0)

    acc_ref[...] = jnp.zeros_like(acc_ref)
    m_ref[...] = jnp.full_like(m_ref, NEG_INF)
    row = lax.broadcasted_iota(jnp.int32, (tk, tk), 0)
    col = lax.broadcasted_iota(jnp.int32, (tk, tk), 1)
    causal_bias = jnp.where(row <= col, 0.0, NEG_INF)

    def scores(c, s_ref):
        k0 = pl.multiple_of(c * tk, tk)
        s_ref[...] = _dot_nt(kaug_ref[pl.ds(k0, tk), :], qaug_ref[...])

    def accumulate(c, s_ref, diag_half):
        k0 = pl.multiple_of(c * tk, tk)
        for g in range(N_KV):
            ps, alphas = [], []
            for i in range(per_g):
                cb = g * per_g + i
                half, within = divmod(cb % nq, nk)
                cols = slice(cb * LANES, (cb + 1) * LANES)
                if diag_half is not None and half < diag_half:
                    ps.append(jnp.zeros((tk, LANES), jnp.bfloat16))
                    alphas.append(jnp.ones((1, LANES), jnp.float32))
                    continue
                s = s_ref[:, cols]
                if diag_half is not None and half == diag_half:
                    s = s + causal_bias[:, within * LANES:(within + 1) * LANES]
                m_old = m_ref[0:1, cols]
                m_new = jnp.maximum(m_old, jnp.max(s, axis=0, keepdims=True))
                m_ref[0:1, cols] = m_new
                alphas.append(jnp.exp2(m_old - m_new))
                ps.append(jnp.exp2(s - m_new).astype(jnp.bfloat16))
            vt_g = vt_ref[0, g * G_ROWS:(g + 1) * G_ROWS, pl.ds(k0, tk)]
            pv = _dot(vt_g, jnp.concatenate(ps, axis=1))
            acc_ref[g] = jnp.concatenate(alphas, axis=1) * acc_ref[g] + pv

    scores(0, s0_ref)

    def body(i, carry):
        c = 2 * i
        scores(c + 1, s1_ref)
        accumulate(c, s0_ref, None)
        scores(c + 2, s0_ref)
        accumulate(c + 1, s1_ref, None)
        return carry

    lax.fori_loop(0, step, body, 0)
    c0 = 2 * step
    scores(c0 + 1, s1_ref)
    accumulate(c0, s0_ref, 0)
    accumulate(c0 + 1, s1_ref, 1)

    o_g = [acc_ref[g] for g in range(N_KV)]
    gates = _branch_gates(gn_ref[0], 1, tq)
    scale = [gates[g] / o_g[g][HEAD_DIM:HEAD_DIM + 1, :] for g in range(N_KV)]
    _store_heads(o_ref, slice(None), o_g, scale, tq)


def _sel_attn(qa, g_nsa, bias, ks, vst, blk_onehot, tk=256):
    bsz, seq, _ = qa.shape
    tq = 2 * tk
    kern = functools.partial(_sel_kernel, tq=tq, tk=tk)
    return pl.pallas_call(
        kern,
        out_shape=jax.ShapeDtypeStruct((bsz, seq, Q_WIDTH), jnp.bfloat16),
        grid=(bsz, seq // tq),
        in_specs=[pl.BlockSpec((1, tq, Q_WIDTH), lambda b, i: (b, i, 0)),
                  pl.BlockSpec((1, tq, LANES), lambda b, i: (b, i, 0)),
                  pl.BlockSpec((1, tq, LANES), lambda b, i: (b, i, 0)),
                  pl.BlockSpec((1, seq, LANES), lambda b, i: (b, 0, 0)),
                  pl.BlockSpec((1, V_ROWS, seq), lambda b, i: (b, 0, 0)),
                  pl.BlockSpec((seq, LANES), lambda b, i: (0, 0))],
        out_specs=pl.BlockSpec((1, tq, Q_WIDTH), lambda b, i: (b, i, 0)),
        scratch_shapes=[pltpu.VMEM((seq, 2 * LANES), jnp.bfloat16),
                        pltpu.VMEM((N_HEADS * tq, 2 * LANES), jnp.bfloat16),
                        pltpu.VMEM((tk, N_HEADS * tq), jnp.float32),
                        pltpu.VMEM((tk, N_HEADS * tq), jnp.float32),
                        pltpu.VMEM((N_KV, G_ROWS, GROUP * tq), jnp.float32),
                        pltpu.VMEM((SUBLANES, N_HEADS * tq), jnp.float32)],
        compiler_params=_cparams(("arbitrary", "arbitrary")),
        name="sel_attn",
    )(qa, g_nsa, bias, ks, vst, blk_onehot)


def _band_kernel(qa_ref, kw_ref, vtw_ref, gn_ref, qb_ref, kb_ref, vtb_ref, sink_ref, ow_ref, ob_ref,
                 *, tq, n_sub, spans, windows):
    step = pl.program_id(1)
    seq = kw_ref.shape[1]
    nq = tq // LANES
    jobs = (dict(q=qa_ref, k=kw_ref, vt=vtw_ref, o=ow_ref, span=spans[0], window=windows[0], sink=False),
            dict(q=qb_ref, k=kb_ref, vt=vtb_ref, o=ob_ref, span=spans[1], window=windows[1], sink=True))
    diffs = []
    for job in jobs:
        row = lax.broadcasted_iota(jnp.int32, (job["span"], tq), 0)
        col = lax.broadcasted_iota(jnp.int32, (job["span"], tq), 1)
        diffs.append(col - row)

    def scores(unit):
        j, sub = unit
        job = jobs[j]
        span = job["span"]
        q0 = (step * n_sub + sub) * tq
        start = pl.multiple_of(jnp.clip(q0 + tq - span, 0, seq - span), LANES)
        rel = diffs[j] + (q0 - start)
        bias = jnp.where((rel >= 0) & (rel < job["window"]), 0.0, NEG_INF)
        qs = _stack_heads(job["q"][0, sub * tq:(sub + 1) * tq, :])
        return start, bias, _dot_nt(job["k"][0, pl.ds(start, span), :], qs)

    def finish(unit, start, bias, s_t):
        j, sub = unit
        job = jobs[j]
        span = job["span"]
        ps, extra = [], []
        for cb in range(N_HEADS * nq):
            s = s_t[:, cb * LANES:(cb + 1) * LANES] + bias[:, (cb % nq) * LANES:(cb % nq + 1) * LANES]
            m = jnp.max(s, axis=0, keepdims=True)
            if job["sink"]:
                sink = sink_ref[cb // nq] * LOG2E
                m = jnp.maximum(m, sink)
                extra.append(jnp.exp2(sink - m))
            ps.append(jnp.exp2(s - m).astype(jnp.bfloat16))
        o_g, inv_l = [], []
        for g in range(N_KV):
            per_g = GROUP * nq
            vt_g = job["vt"][0, g * G_ROWS:(g + 1) * G_ROWS, pl.ds(start, span)]
            o = _dot(vt_g, jnp.concatenate(ps[g * per_g:(g + 1) * per_g], axis=1))
            l = o[HEAD_DIM:HEAD_DIM + 1, :]
            if job["sink"]:
                l = l + jnp.concatenate(extra[g * per_g:(g + 1) * per_g], axis=1)
            o_g.append(o)
            inv_l.append(1.0 / l)
        if not job["sink"]:
            gates = _branch_gates(gn_ref[0, sub * tq:(sub + 1) * tq, :], 2, tq)
            inv_l = [a * b for a, b in zip(inv_l, gates)]
        _store_heads(job["o"], slice(sub * tq, (sub + 1) * tq), o_g, inv_l, tq)

    units = [(j, sub) for sub in range(n_sub) for j in range(len(jobs))]
    nxt = scores(units[0])
    for u, unit in enumerate(units):
        cur = nxt
        if u + 1 < len(units):
            nxt = scores(units[u + 1])
        finish(unit, *cur)


def _band_attn(qa, k_w, vt_w, g_nsa, qb, k_b, vt_b, sinks, tq=128, n_sub=16):
    bsz, seq, _ = qa.shape
    windows = (NSA_WINDOW, SWA_WINDOW)
    spans = tuple((-(-(w - 1) // tq)) * tq + tq for w in windows)
    kern = functools.partial(_band_kernel, tq=tq, n_sub=n_sub, spans=spans, windows=windows)
    ts = tq * n_sub
    q_spec = pl.BlockSpec((1, ts, Q_WIDTH), lambda b, i: (b, i, 0))
    k_spec = pl.BlockSpec((1, seq, LANES), lambda b, i: (b, 0, 0))
    vt_spec = pl.BlockSpec((1, V_ROWS, seq), lambda b, i: (b, 0, 0))
    out = jax.ShapeDtypeStruct((bsz, seq, Q_WIDTH), jnp.bfloat16)
    return pl.pallas_call(
        kern,
        out_shape=[out, out],
        grid=(bsz, seq // ts),
        in_specs=[q_spec, k_spec, vt_spec, pl.BlockSpec((1, ts, LANES), lambda b, i: (b, i, 0)),
                  q_spec, k_spec, vt_spec, pl.BlockSpec(memory_space=pltpu.SMEM)],
        out_specs=[q_spec, q_spec],
        compiler_params=_cparams(("arbitrary", "arbitrary")),
        name="band_attn",
    )(qa, k_w, vt_w, g_nsa, qb, k_b, vt_b, sinks)


def _merge_kernel(x_ref, mod_ref, wza_ref, wzb_ref, wgm_ref, ocmp_ref, osel_ref, owin_ref, ob_ref,
                  wua_ref, wub_ref, wo_ref, lng_ref, lnb_ref, o_ref, *, ts, n_sub):
    shift = mod_ref[0, 0:1, :]
    scale = mod_ref[0, 1:2, :]
    gate = mod_ref[0, 2:3, :]

    def project(r):
        u = (_layer_norm(x_ref[0, r * ts:(r + 1) * ts, :]) * (1.0 + scale) + shift).astype(jnp.bfloat16)
        return _dot(u, wza_ref[...]), _dot(u, wzb_ref[...]), _dot_nt(u, wgm_ref[...])

    def finish(r, z_a, z_b, g_m):
        rows = slice(r * ts, (r + 1) * ts)
        o_a = (ocmp_ref[0, rows, :].astype(jnp.float32) + osel_ref[0, rows, :].astype(jnp.float32)
               + owin_ref[0, rows, :].astype(jnp.float32))
        y_a = (o_a * (z_a * _sigmoid(z_a))).astype(jnp.bfloat16)
        y_b = (ob_ref[0, rows, :].astype(jnp.float32) * (z_b * _sigmoid(z_b))).astype(jnp.bfloat16)
        merged = _sigmoid(g_m[:, 0:D_MODEL]) * _dot(y_a, wua_ref[...])
        merged = merged + _sigmoid(g_m[:, D_MODEL:2 * D_MODEL]) * _dot(y_b, wub_ref[...])
        out = _dot(merged.astype(jnp.bfloat16), wo_ref[...])
        y = DEEPNORM_ALPHA * x_ref[0, rows, :] + gate * out
        o_ref[0, rows, :] = _layer_norm(y) * lng_ref[...] + lnb_ref[...]

    nxt = project(0)
    for r in range(n_sub):
        cur = nxt
        if r + 1 < n_sub:
            nxt = project(r + 1)
        finish(r, *cur)


def _merge_out(x, mod3, wgs, o_cmp, o_sel, o_win, o_b, wua, wub, wo, ln_g, ln_b, ts=256, n_sub=4):
    bsz, seq, _ = x.shape
    tm = ts * n_sub
    tok = lambda w: pl.BlockSpec((1, tm, w), lambda b, i: (b, i, 0))
    full = lambda a: pl.BlockSpec(a.shape, lambda b, i: (0,) * a.ndim)
    return pl.pallas_call(
        functools.partial(_merge_kernel, ts=ts, n_sub=n_sub),
        out_shape=jax.ShapeDtypeStruct(x.shape, jnp.float32),
        grid=(bsz, seq // tm),
        in_specs=[tok(D_MODEL), pl.BlockSpec((1, 3, D_MODEL), lambda b, i: (b, 0, 0)),
                  *[full(w) for w in wgs], tok(Q_WIDTH), tok(Q_WIDTH), tok(Q_WIDTH), tok(Q_WIDTH),
                  full(wua), full(wub), full(wo), full(ln_g), full(ln_b)],
        out_specs=tok(D_MODEL),
        compiler_params=_cparams(("arbitrary", "arbitrary")),
        name="merge_out",
    )(x, mod3, *wgs, o_cmp, o_sel, o_win, o_b, wua, wub, wo, ln_g, ln_b)


def _pair_q_cols(w):
    k = w.shape[0]
    return w.reshape(k, N_KV, GROUP, 2, HALF).transpose(0, 2, 3, 1, 4).reshape(k, Q_WIDTH)


def _pair_k_cols(w):
    k = w.shape[0]
    return w.reshape(k, N_KV, 2, HALF).transpose(0, 2, 1, 3).reshape(k, KV_WIDTH)


def _pair_o_cols(w):
    k = w.shape[0]
    return w.reshape(k, N_KV, GROUP, HEAD_DIM).transpose(0, 2, 1, 3).reshape(k, Q_WIDTH)


def _pair_o_rows(w):
    n = w.shape[1]
    return w.reshape(N_KV, GROUP, HEAD_DIM, n).transpose(1, 0, 2, 3).reshape(Q_WIDTH, n)


def _prep_qkv_weight(w_t):
    q_scale = HEAD_DIM ** -0.5
    sec = lambda off, width: w_t[off:off + width].T
    cols = [_pair_q_cols(sec(OFF_QA, Q_WIDTH)) * q_scale,
            _pair_k_cols(sec(OFF_KVC, KV_WIDTH)), _pair_k_cols(sec(OFF_KVS, KV_WIDTH)),
            _pair_k_cols(sec(OFF_KVW, KV_WIDTH)),
            _pair_q_cols(sec(OFF_QB, Q_WIDTH)) * q_scale,
            _pair_k_cols(sec(OFF_KVB, KV_WIDTH)),
            sec(OFF_KVC + KV_WIDTH, KV_WIDTH),
            sec(OFF_GN, 3 * N_HEADS), jnp.zeros((D_MODEL, LANES - 3 * N_HEADS), w_t.dtype)]
    v_cols = [sec(OFF_KVS + KV_WIDTH, KV_WIDTH), sec(OFF_KVW + KV_WIDTH, KV_WIDTH),
              sec(OFF_KVB + KV_WIDTH, KV_WIDTH)]
    return jnp.concatenate(cols, axis=1), jnp.concatenate(v_cols, axis=1).T


def _prep_gate_weights(w_t):
    return [_pair_o_cols(w_t[OFF_ZA:OFF_ZA + Q_WIDTH].T), _pair_o_cols(w_t[OFF_ZB:OFF_ZB + Q_WIDTH].T),
            w_t[OFF_GM:OFF_GM + 2 * D_MODEL]]


def _prep_compress(pos, w1, w2, paired):
    half_len = CMP_LEN // 2
    w = w1.reshape(CMP_LEN, HEAD_DIM, CMP_HIDDEN).astype(jnp.bfloat16)
    w2 = w2.astype(jnp.bfloat16)
    pieces = ([(g, slice(h * HALF, (h + 1) * HALF)) for h in range(2) for g in range(N_KV)] if paired
              else [(g, slice(0, HEAD_DIM)) for g in range(N_KV)])
    own = lambda g, blk: jnp.concatenate([blk if k == g else jnp.zeros_like(blk) for k in range(N_KV)], axis=-1)
    w1x = jnp.concatenate([own(g, w[:, d, :]) for g, d in pieces], axis=1)
    first = w1x[:half_len].reshape(half_len * LANES, N_KV * CMP_HIDDEN)
    second = w1x[half_len:].reshape(half_len * LANES, N_KV * CMP_HIDDEN)
    w1e = jnp.concatenate([first, second], axis=1)
    w2e = jnp.concatenate([jnp.concatenate([w2[:, d] if k == g else jnp.zeros_like(w2[:, d]) for g, d in pieces], axis=1)
                           for k in range(N_KV)], axis=0)
    pos_l = jnp.concatenate([pos[:, d] for _, d in pieces], axis=1)
    pos_rows = jnp.concatenate([pos_l[:half_len].reshape(1, -1), pos_l[half_len:].reshape(1, -1),
                                jnp.zeros((SUBLANES - 2, half_len * LANES), pos.dtype)], axis=0).astype(jnp.bfloat16)
    return w1e, pos_rows, w2e


def _overlap_t(n_cmp, n_blk):
    c0 = np.arange(n_cmp) * CMP_STRIDE
    j0 = np.arange(n_blk) * SEL_BLOCK
    overlap = (c0[:, None] < j0[None, :] + SEL_BLOCK) & (c0[:, None] + CMP_LEN > j0[None, :])
    ovt = np.zeros((LANES, LANES), np.float32)
    ovt[:n_blk, :n_cmp] = overlap.T
    return jnp.asarray(ovt, jnp.bfloat16)


def _block_onehot(seq):
    n_blk = seq // SEL_BLOCK
    e = np.zeros((seq, LANES), np.float32)
    pos = np.arange(seq)
    e[pos, pos // SEL_BLOCK] = 1.0
    e[pos, n_blk + pos // SEL_BLOCK] = 1.0
    return jnp.asarray(e, jnp.bfloat16)


def kernel(x, c, positions, w_ada, b_ada, w_in, cmp_pos_k, cmp_w1_k, cmp_w2_k,
           cmp_pos_v, cmp_w1_v, cmp_w2_v, sinks, w_up_a, w_up_b, w_out, ln_g, ln_b):
    bsz, seq, _ = x.shape
    n_cmp = (seq - CMP_LEN) // CMP_STRIDE + 1
    n_blk = seq // SEL_BLOCK
    assert seq // CMP_STRIDE == LANES and 2 * n_blk <= LANES
    cos_t, sin_t = _rope_tables(positions, QKV_ROWS)
    for l in range(DEPTH):
        mod3 = _adaln_mod(c, w_ada[l], b_ada[l]).reshape(bsz, 3, D_MODEL)
        w_t = jnp.swapaxes(w_in[l], 0, 1).astype(jnp.bfloat16)
        w_qk, w_vt = _prep_qkv_weight(w_t)
        qa, k_c, k_s, k_w, qb, k_b, v_c, vt_s, vt_w, vt_b, g_nsa = _qkv_proj(x, mod3, w_qk, w_vt, cos_t, sin_t)

        w1k, posk, w2k = _prep_compress(cmp_pos_k[l], cmp_w1_k[l], cmp_w2_k[l], paired=True)
        w1v, posv, w2v = _prep_compress(cmp_pos_v[l], cmp_w1_v[l], cmp_w2_v[l], paired=False)
        kc, vct = _compress(k_c, v_c, w1k, w1v, posk, posv, w2k, w2v.T, bb=8 if bsz % 8 == 0 else 1)

        o_cmp, sel_bias = _cmp_attn(qa, g_nsa, kc, vct, _overlap_t(n_cmp, n_blk), n_cmp)
        o_sel = _sel_attn(qa, g_nsa, sel_bias, k_s, vt_s, _block_onehot(seq))
        o_win, o_b = _band_attn(qa, k_w, vt_w, g_nsa, qb, k_b, vt_b, sinks[l].astype(jnp.float32))

        x = _merge_out(x, mod3, _prep_gate_weights(w_t),
                       o_cmp, o_sel, o_win, o_b,
                       _pair_o_rows(w_up_a[l]).astype(jnp.bfloat16), _pair_o_rows(w_up_b[l]).astype(jnp.bfloat16),
                       w_out[l].astype(jnp.bfloat16), ln_g[l].reshape(1, D_MODEL), ln_b[l].reshape(1, D_MODEL))
    return x
```
